```python
import math
import jax, jax.numpy as jnp
from jax import lax
import numpy as np

D_MODEL = 1024
BATCH = 2
SEQ = 8192
DEPTH = 4
DEC_BATCH = 128
DEC_SEQ = 4
PAST_LEN = 2048
PAGE_SIZE = 128

HEAD_DIM = 64
N_Q_HEADS = 8
N_KV_HEADS = 2
HPG = N_Q_HEADS // N_KV_HEADS
NSA_W = N_Q_HEADS * HEAD_DIM
KV_W = N_KV_HEADS * HEAD_DIM
CMP_BLOCK = 32
CMP_STRIDE = 16
CMP_HIDDEN = HEAD_DIM
SEL_BLOCK = 64
TOP_K = 8
WINDOW = 512
Q_BLOCK = 128
N_BUCKETS = 32
MAX_EXACT = N_BUCKETS // 2
MAX_DISTANCE = 128
POOL_WINDOWS = (2, 4, 8, 16)
N_POOL_GROUPS = len(POOL_WINDOWS)
POOL_W = D_MODEL // 4
POOL_GC = POOL_W // N_POOL_GROUPS
POOL_STATE = max(POOL_WINDOWS) - 1
CONV_W = D_MODEL // 4
CONV_K = 3
D_FF = 128 * (-(-8 * D_MODEL // (3 * 128)))
N_KV_SLOTS = 4
RMS_EPS = 1e-6
PROJ_SIZES = (NSA_W, 6 * KV_W, 3 * N_Q_HEADS, POOL_W, 3 * CONV_W, 3 * D_MODEL)
D_IN = sum(PROJ_SIZES)
SPLITS = [int(s) for s in np.cumsum(PROJ_SIZES)[:-1]]

kernel_name = 'nsa_pool_conv_macaron_decoder_step'


def rms_norm(x, g):
    xf = x.astype(jnp.float32)
    y = xf * lax.rsqrt(jnp.mean(xf * xf, axis=-1, keepdims=True) + RMS_EPS)
    return (y * g.astype(jnp.float32)).astype(x.dtype)


def swiglu(h, wg, wu, wd):
    return (jax.nn.silu(h @ wg) * (h @ wu)) @ wd


def rel_bucket(dist):
    n = jnp.maximum(dist, 0)
    nf = jnp.maximum(n, 1).astype(jnp.float32)
    large = MAX_EXACT + (jnp.log(nf / MAX_EXACT) / math.log(MAX_DISTANCE / MAX_EXACT)
                         * (N_BUCKETS - MAX_EXACT)).astype(jnp.int32)
    large = jnp.minimum(large, N_BUCKETS - 1)
    return jnp.where(n < MAX_EXACT, n, large)


def masked_softmax(s, mask):
    s = jnp.where(mask, s.astype(jnp.float32), -jnp.inf)
    m = jnp.max(s, axis=-1, keepdims=True)
    m = jnp.where(jnp.isfinite(m), m, 0.0)
    e = jnp.where(mask, jnp.exp(s - m), 0.0)
    den = jnp.sum(e, axis=-1, keepdims=True)
    return e / jnp.where(den > 0, den, 1.0)


def compress(x_raw, pe, w1, w2):
    B, L = x_raw.shape[:2]
    r_cmp = CMP_BLOCK // CMP_STRIDE
    n_ch = L // CMP_STRIDE
    ch = x_raw[:, :n_ch * CMP_STRIDE].reshape(B, n_ch, CMP_STRIDE, N_KV_HEADS, HEAD_DIM)
    n_c = n_ch - r_cmp + 1
    blocks = jnp.concatenate([ch[:, r:r + n_c] for r in range(r_cmp)], axis=2)
    hid = jax.nn.silu(jnp.einsum('bnlgd,lde->bnge', blocks + pe[None, None, :, None, :], w1))
    return jnp.einsum('bnge,ed->bngd', hid, w2)


def to_sel_blocks(x):
    B, L = x.shape[:2]
    n_s = -(-L // SEL_BLOCK)
    x = jnp.pad(x, ((0, 0), (0, n_s * SEL_BLOCK - L), (0, 0), (0, 0)))
    return x.reshape(B, n_s, SEL_BLOCK, N_KV_HEADS, HEAD_DIM).transpose(0, 3, 1, 2, 4)


def slc_map(n_sel, n_cmp):
    r_sel = SEL_BLOCK // CMP_STRIDE
    r_cmp = CMP_BLOCK // CMP_STRIDE
    offs = np.array([m - n for m in range(r_sel) for n in range(r_cmp)])
    idx = r_sel * np.arange(n_sel)[:, None] + offs[None, :]
    ok = (idx >= 0) & (idx < n_cmp)
    return np.clip(idx, 0, n_cmp - 1), ok


def nsa_attend(q, q_pos, kc, vc, ks_blk, vs_blk, kw, vw, w_pos, gates, rel_bias):
    B, Tq = q.shape[:2]
    tab = rel_bias.reshape(N_BUCKETS, N_KV_HEADS, HPG)
    qf = q * (HEAD_DIM ** -0.5)
    gi = jnp.arange(N_KV_HEADS)
    n_c = kc.shape[1]
    c_end = jnp.arange(n_c) * CMP_STRIDE + CMP_BLOCK - 1
    dist_c = q_pos[:, None] - c_end[None, :]
    bias_c = tab[rel_bucket(dist_c)].transpose(0, 2, 3, 1)[None]
    s_c = jnp.einsum('btghd,bngd->btghn', qf, kc).astype(jnp.float32) + bias_c
    p_c = masked_softmax(s_c, (dist_c >= 0)[None, :, None, None, :])
    o_c = jnp.einsum('btghn,bngd->btghd', p_c.astype(vc.dtype), vc)
    imp = jnp.sum(p_c, axis=3)
    n_s = ks_blk.shape[2]
    sidx, sok = slc_map(n_s, n_c)
    score = jnp.sum(jnp.where(sok, imp[..., sidx], 0.0), axis=-1)
    jn = jnp.arange(n_s)
    cur = q_pos // SEL_BLOCK
    forced = (jn[None, :] == 0) | (jn[None, :] == cur[:, None]) | (jn[None, :] == cur[:, None] - 1)
    allowed = (jn[None, :] * SEL_BLOCK) <= q_pos[:, None]
    score = jnp.where(forced[None, :, None, :], jnp.inf, score)
    score = jnp.where(allowed[None, :, None, :], score, -jnp.inf)
    k_sel = min(TOP_K, n_s)
    _, idx = lax.top_k(score, k_sel)
    bi = jnp.arange(B)[:, None, None, None]
    kb = ks_blk[bi, gi[None, None, :, None], idx]
    vb = vs_blk[bi, gi[None, None, :, None], idx]
    pos_s = idx[..., None] * SEL_BLOCK + jnp.arange(SEL_BLOCK)
    dist_s = q_pos[None, :, None, None, None] - pos_s
    bias_s = tab[rel_bucket(dist_s), gi[None, None, :, None, None]].transpose(0, 1, 2, 5, 3, 4)
    s_s = jnp.einsum('btghd,btgksd->btghks', qf, kb).astype(jnp.float32) + bias_s
    n_sk = k_sel * SEL_BLOCK
    mask_s = (dist_s >= 0).reshape(B, Tq, N_KV_HEADS, 1, n_sk)
    p_s = masked_softmax(s_s.reshape(B, Tq, N_KV_HEADS, HPG, n_sk), mask_s)
    o_s = jnp.einsum('btghn,btgnd->btghd', p_s.astype(vb.dtype),
                     vb.reshape(B, Tq, N_KV_HEADS, n_sk, HEAD_DIM))
    dist_w = q_pos[:, None] - w_pos[None, :]
    mask_w = (dist_w >= 0) & (dist_w < WINDOW) & (w_pos[None, :] >= 0)
    bias_w = tab[rel_bucket(dist_w)].transpose(0, 2, 3, 1)[None]
    s_w = jnp.einsum('btghd,bsgd->btghs', qf, kw).astype(jnp.float32) + bias_w
    p_w = masked_softmax(s_w, mask_w[None, :, None, None, :])
    o_w = jnp.einsum('btghs,bsgd->btghd', p_w.astype(vw.dtype), vw)
    g = jax.nn.sigmoid(gates.astype(jnp.float32))
    out = g[..., 0:1] * o_c + g[..., 1:2] * o_s + g[..., 2:3] * o_w
    return out.astype(q.dtype)


def nsa_prompt(q, kc, vc, ks, vs, kw, vw, gates, rel_bias):
    B, T = q.shape[:2]
    ks_blk, vs_blk = to_sel_blocks(ks), to_sel_blocks(vs)
    pad = ((0, 0), (WINDOW, 0), (0, 0), (0, 0))
    kw_pad, vw_pad = jnp.pad(kw, pad), jnp.pad(vw, pad)
    n_wk = WINDOW + Q_BLOCK

    def block(i):
        start = i * Q_BLOCK
        q_pos = start + jnp.arange(Q_BLOCK)
        w_pos = start - WINDOW + jnp.arange(n_wk)
        return nsa_attend(lax.dynamic_slice_in_dim(q, start, Q_BLOCK, axis=1), q_pos, kc, vc,
                          ks_blk, vs_blk,
                          lax.dynamic_slice_in_dim(kw_pad, start, n_wk, axis=1),
                          lax.dynamic_slice_in_dim(vw_pad, start, n_wk, axis=1), w_pos,
                          lax.dynamic_slice_in_dim(gates, start, Q_BLOCK, axis=1), rel_bias)

    out = lax.map(block, jnp.arange(T // Q_BLOCK))
    return jnp.moveaxis(out, 0, 1).reshape(B, T, NSA_W)


def project(h, w_in, q_g, k_g):
    B, T = h.shape[:2]
    q, kv, ng, u, cv, mg = jnp.split(h @ w_in, SPLITS, axis=-1)
    q = rms_norm(q.reshape(B, T, N_KV_HEADS, HPG, HEAD_DIM), q_g)
    kv = kv.reshape(B, T, 6, N_KV_HEADS, HEAD_DIM)
    k_cmp, v_cmp = kv[:, :, 0], kv[:, :, 1]
    k_sel, v_sel = rms_norm(kv[:, :, 2], k_g[1]), kv[:, :, 3]
    k_win, v_win = rms_norm(kv[:, :, 4], k_g[2]), kv[:, :, 5]
    ng = ng.reshape(B, T, N_KV_HEADS, HPG, 3)
    cb, cc, ch = jnp.split(cv, 3, axis=-1)
    return q, k_cmp, v_cmp, k_sel, v_sel, k_win, v_win, ng, u, cb, cc * ch, mg


def compress_kv(k_raw, v_raw, pe, w1, w2, k_g):
    kc = rms_norm(compress(k_raw, pe[0], w1[0], w2[0]), k_g[0])
    vc = compress(v_raw, pe[1], w1[1], w2[1])
    return kc, vc


def pool_mix(u_ext, T, w_pool, pool_scale):
    B, L = u_ext.shape[:2]
    P = L - T
    cs = jnp.cumsum(u_ext.astype(jnp.float32), axis=1)
    cs0 = jnp.concatenate([jnp.zeros_like(cs[:, :1]), cs], axis=1)
    cs0 = cs0.reshape(B, L + 1, N_POOL_GROUPS, POOL_GC)
    r = P + jnp.arange(T)
    win = jnp.array(POOL_WINDOWS, jnp.int32)
    lo = jnp.maximum(r[:, None] + 1 - win[None, :], 0)
    cnt = (r[:, None] + 1 - lo).astype(jnp.float32)
    hi_v = cs0[:, r + 1]
    lo_v = cs0[:, lo, jnp.arange(N_POOL_GROUPS)[None, :]]
    u_t = u_ext[:, P:].reshape(B, T, N_POOL_GROUPS, POOL_GC).astype(jnp.float32)
    d = ((hi_v - lo_v) / cnt[None, :, :, None] - u_t).astype(u_ext.dtype)
    return jnp.einsum('btgc,gce->btge', d, w_pool).reshape(B, T, D_MODEL) * pool_scale


def conv_mix(z_ext, cb, conv_w, w_out):
    T = cb.shape[1]
    y = sum(conv_w[j] * z_ext[:, j:j + T] for j in range(CONV_K))
    return (cb * y) @ w_out


def merge_branches(attn, pool_o, conv_o, mg, w_a, w_o):
    ga, gp, gc = jnp.split(jax.nn.sigmoid(mg.astype(jnp.float32)), 3, axis=-1)
    m = ga * (attn @ w_a) + gp * pool_o + gc * conv_o
    return m.astype(attn.dtype) @ w_o


def setup_inputs(seed: int = 0) -> dict:
    key = jax.random.key(seed)
    ks = jax.random.split(key, 32)

    def nrm(k, shape, s):
        return s * jax.random.normal(k, shape, jnp.float32)

    n_pages = PAST_LEN // PAGE_SIZE
    n_used = DEC_BATCH * n_pages
    n_pool = n_used + n_used // 4
    w_keep = min(WINDOW, PAST_LEN)
    page_table = jax.random.permutation(ks[0], n_pool)[:n_used].reshape(DEC_BATCH, n_pages).astype(jnp.int32)
    return {
        'x_prompt': nrm(ks[1], (BATCH, SEQ, D_MODEL), 1.0),
        'x_sample': nrm(ks[2], (DEC_BATCH, DEC_SEQ, D_MODEL), 1.0),
        'cache_kv': nrm(ks[3], (DEPTH, n_pool, PAGE_SIZE, N_KV_SLOTS, N_KV_HEADS, HEAD_DIM), 1.0),
        'state_win': nrm(ks[4], (DEPTH, DEC_BATCH, w_keep, 2, N_KV_HEADS, HEAD_DIM), 1.0),
        'state_pool': nrm(ks[5], (DEPTH, DEC_BATCH, POOL_STATE, POOL_W), 1.0),
        'state_conv': nrm(ks[6], (DEPTH, DEC_BATCH, CONV_K - 1, CONV_W), 1.0),
        'page_table': page_table,
        'rel_bias': nrm(ks[7], (N_BUCKETS, N_Q_HEADS), 0.5),
        'ln_ffn1': 1.0 + nrm(ks[8], (DEPTH, D_MODEL), 0.05),
        'w_ffn1_gate': nrm(ks[9], (DEPTH, D_MODEL, D_FF), D_MODEL ** -0.5),
        'w_ffn1_up': nrm(ks[10], (DEPTH, D_MODEL, D_FF), D_MODEL ** -0.5),
        'w_ffn1_down': nrm(ks[11], (DEPTH, D_FF, D_MODEL), D_FF ** -0.5),
        'ln_mix': 1.0 + nrm(ks[12], (DEPTH, D_MODEL), 0.05),
        'w_in': nrm(ks[13], (DEPTH, D_MODEL, D_IN), D_MODEL ** -0.5),
        'q_norm': 1.0 + nrm(ks[14], (DEPTH, HEAD_DIM), 0.05),
        'k_norm': 1.0 + nrm(ks[15], (DEPTH, 3, HEAD_DIM), 0.05),
        'cmp_pe': nrm(ks[16], (DEPTH, 2, CMP_BLOCK, HEAD_DIM), 0.2),
        'cmp_w1': nrm(ks[17], (DEPTH, 2, CMP_BLOCK, HEAD_DIM, CMP_HIDDEN), (CMP_BLOCK * HEAD_DIM) ** -0.5),
        'cmp_w2': nrm(ks[18], (DEPTH, 2, CMP_HIDDEN, HEAD_DIM), CMP_HIDDEN ** -0.5),
        'w_nsa_out': nrm(ks[19], (DEPTH, NSA_W, D_MODEL), NSA_W ** -0.5),
        'w_pool': nrm(ks[20], (DEPTH, N_POOL_GROUPS, POOL_GC, D_MODEL // N_POOL_GROUPS), POOL_GC ** -0.5),
        'pool_scale': 1.0 + nrm(ks[21], (DEPTH, D_MODEL), 0.1),
        'conv_w': nrm(ks[22], (DEPTH, CONV_K, CONV_W), CONV_K ** -0.5),
        'w_conv_out': nrm(ks[23], (DEPTH, CONV_W, D_MODEL), CONV_W ** -0.5),
        'w_o': nrm(ks[24], (DEPTH, D_MODEL, D_MODEL), D_MODEL ** -0.5),
        'ln_ffn2': 1.0 + nrm(ks[25], (DEPTH, D_MODEL), 0.05),
        'w_ffn2_gate': nrm(ks[26], (DEPTH, D_MODEL, D_FF), D_MODEL ** -0.5),
        'w_ffn2_up': nrm(ks[27], (DEPTH, D_MODEL, D_FF), D_MODEL ** -0.5),
        'w_ffn2_down': nrm(ks[28], (DEPTH, D_FF, D_MODEL), D_FF ** -0.5),
    }


def reference(x_prompt, x_sample, cache_kv, state_win, state_pool, state_conv, page_table,
              rel_bias, ln_ffn1, w_ffn1_gate, w_ffn1_up, w_ffn1_down, ln_mix, w_in, q_norm, k_norm,
              cmp_pe, cmp_w1, cmp_w2, w_nsa_out, w_pool, pool_scale, conv_w, w_conv_out, w_o,
              ln_ffn2, w_ffn2_gate, w_ffn2_up, w_ffn2_down):
    B, T = x_prompt.shape[:2]
    w_keep_p = min(WINDOW, T)
    x = x_prompt
    pk_rows, pk_win, pk_pool, pk_conv = [], [], [], []
    for l in range(DEPTH):
        x = x + 0.5 * swiglu(rms_norm(x, ln_ffn1[l]), w_ffn1_gate[l], w_ffn1_up[l], w_ffn1_down[l])
        h = rms_norm(x, ln_mix[l])
        q, kcr, vcr, ksl, vsl, kwn, vwn, ng, u, cb, z, mg = project(h, w_in[l], q_norm[l], k_norm[l])
        kc, vc = compress_kv(kcr, vcr, cmp_pe[l], cmp_w1[l], cmp_w2[l], k_norm[l])
        attn = nsa_prompt(q, kc, vc, ksl, vsl, kwn, vwn, ng, rel_bias)
        pool_o = pool_mix(u, T, w_pool[l], pool_scale[l])
        z_ext = jnp.pad(z, ((0, 0), (CONV_K - 1, 0), (0, 0)))
        conv_o = conv_mix(z_ext, cb, conv_w[l], w_conv_out[l])
        x = x + merge_branches(attn, pool_o, conv_o, mg, w_nsa_out[l], w_o[l])
        x = x + 0.5 * swiglu(rms_norm(x, ln_ffn2[l]), w_ffn2_gate[l], w_ffn2_up[l], w_ffn2_down[l])
        pk_rows.append(jnp.stack([kcr, vcr, ksl, vsl], axis=2))
        pk_win.append(jnp.stack([kwn, vwn], axis=2)[:, T - w_keep_p:])
        pk_pool.append(u[:, T - POOL_STATE:])
        pk_conv.append(z[:, T - (CONV_K - 1):])
    y_prompt = x

    DB, S = x_sample.shape[:2]
    n_pages = page_table.shape[1]
    past = n_pages * cache_kv.shape[2]
    w_keep_s = state_win.shape[2]
    q_pos = past + jnp.arange(S)
    w_pos = past - w_keep_s + jnp.arange(w_keep_s + S)
    x = x_sample
    sk_rows, sk_win, sk_pool, sk_conv = [], [], [], []
    for l in range(DEPTH):
        x = x + 0.5 * swiglu(rms_norm(x, ln_ffn1[l]), w_ffn1_gate[l], w_ffn1_up[l], w_ffn1_down[l])
        h = rms_norm(x, ln_mix[l])
        q, kcr, vcr, ksl, vsl, kwn, vwn, ng, u, cb, z, mg = project(h, w_in[l], q_norm[l], k_norm[l])
        past_kv = cache_kv[l, page_table].reshape(DB, past, N_KV_SLOTS, N_KV_HEADS, HEAD_DIM)
        kc, vc = compress_kv(jnp.concatenate([past_kv[:, :, 0], kcr], axis=1),
                             jnp.concatenate([past_kv[:, :, 1], vcr], axis=1),
                             cmp_pe[l], cmp_w1[l], cmp_w2[l], k_norm[l])
        ks_blk = to_sel_blocks(jnp.concatenate([past_kv[:, :, 2], ksl], axis=1))
        vs_blk = to_sel_blocks(jnp.concatenate([past_kv[:, :, 3], vsl], axis=1))
        new_win = jnp.stack([kwn, vwn], axis=2)
        win_f = jnp.concatenate([state_win[l], new_win], axis=1)
        attn = nsa_attend(q, q_pos, kc, vc, ks_blk, vs_blk, win_f[:, :, 0], win_f[:, :, 1], w_pos,
                          ng, rel_bias).reshape(DB, S, NSA_W)
        u_ext = jnp.concatenate([state_pool[l], u], axis=1)
        z_ext = jnp.concatenate([state_conv[l], z], axis=1)
        pool_o = pool_mix(u_ext, S, w_pool[l], pool_scale[l])
        conv_o = conv_mix(z_ext, cb, conv_w[l], w_conv_out[l])
        x = x + merge_branches(attn, pool_o, conv_o, mg, w_nsa_out[l], w_o[l])
        x = x + 0.5 * swiglu(rms_norm(x, ln_ffn2[l]), w_ffn2_gate[l], w_ffn2_up[l], w_ffn2_down[l])
        sk_rows.append(jnp.stack([kcr, vcr, ksl, vsl], axis=2))
        sk_win.append(win_f[:, S:])
        sk_pool.append(u_ext[:, S:])
        sk_conv.append(z_ext[:, S:])
    y_sample = x

    p_kv_rows = jnp.stack(pk_rows)
    p_win = jnp.stack(pk_win)
    p_pool = jnp.stack(pk_pool)
    p_conv = jnp.stack(pk_conv)
    s_kv_rows = jnp.stack(sk_rows)
    s_win = jnp.stack(sk_win)
    s_pool = jnp.stack(sk_pool)
    s_conv = jnp.stack(sk_conv)
    return (y_prompt, y_sample, p_kv_rows, p_win, p_pool, p_conv, s_kv_rows, s_win, s_pool, s_conv)
```

```python
import functools
import math

import numpy as np
import jax
import jax.numpy as jnp
from jax import lax
from jax.experimental import pallas as pl
from jax.experimental.pallas import tpu as pltpu

F32 = jnp.float32
BF16 = jnp.bfloat16
NEG_INF = float("-inf")

HEAD_DIM = 64
N_KV_HEADS = 2
HPG = 4
N_Q_HEADS = N_KV_HEADS * HPG
NSA_W = N_Q_HEADS * HEAD_DIM
KV_W = N_KV_HEADS * HEAD_DIM
CMP_BLOCK = 32
CMP_STRIDE = 16
SEL_BLOCK = 64
TOP_K = 8
WINDOW = 512
Q_BLOCK = 128
N_BUCKETS = 32
MAX_EXACT = 16
MAX_DISTANCE = 128
POOL_WINDOWS = (2, 4, 8, 16)
POOL_STATE = 15
CONV_K = 3
RMS_EPS = 1e-6
LANES = 128
HALO = 16
CMP_PAD = 112
VMEM_LIMIT = 56 * 1024 * 1024


def _dot(a, b):
    return jnp.dot(a, b, preferred_element_type=F32)


def _dot_nt(a, b):
    return lax.dot_general(a, b, (((1,), (1,)), ((), ())), preferred_element_type=F32)


def _dot_split(a, b):
    hi = a.astype(BF16)
    lo = (a - hi.astype(F32)).astype(BF16)
    return _dot(hi, b) + _dot(lo, b)


def _group_mean_sq(x):
    r = lax.broadcasted_iota(jnp.int32, (LANES, LANES), 0) // HEAD_DIM
    c = lax.broadcasted_iota(jnp.int32, (LANES, LANES), 1) // HEAD_DIM
    ones_bd = (r == c).astype(BF16)
    return _dot_split(x * x, ones_bd) * (1.0 / HEAD_DIM)


def _head_rms(x, gain):
    parts = []
    for k in range(x.shape[1] // LANES):
        xs = x[:, k * LANES:(k + 1) * LANES]
        parts.append(xs * lax.rsqrt(_group_mean_sq(xs) + RMS_EPS))
    y = parts[0] if len(parts) == 1 else jnp.concatenate(parts, axis=1)
    return y * gain


def _row_rms(x, gain):
    return x * lax.rsqrt(jnp.mean(x * x, axis=-1, keepdims=True) + RMS_EPS) * gain


def _params(*sem):
    return pltpu.CompilerParams(dimension_semantics=sem, vmem_limit_bytes=VMEM_LIMIT)


def _ffn_kernel(x_ref, g_ref, wg_ref, wu_ref, wd_ref, o_ref, h_scr, acc_scr):
    f = pl.program_id(1)

    @pl.when(f == 0)
    def _():
        h_scr[...] = _row_rms(x_ref[...], g_ref[...]).astype(BF16)
        acc_scr[...] = jnp.zeros_like(acc_scr)

    h = h_scr[...]
    a = _dot(h, wg_ref[...])
    b = _dot(h, wu_ref[...])
    act = (a * jax.nn.sigmoid(a) * b).astype(BF16)
    acc_scr[...] += _dot(act, wd_ref[...])

    @pl.when(f == pl.num_programs(1) - 1)
    def _():
        o_ref[...] = x_ref[...] + 0.5 * acc_scr[...]


def _ffn(x, gain, wg, wu, wd, tm):
    n, d = x.shape
    dff = wg.shape[1]
    tf = dff // 2
    return pl.pallas_call(
        _ffn_kernel,
        grid=(n // tm, dff // tf),
        in_specs=[
            pl.BlockSpec((tm, d), lambda i, f: (i, 0)),
            pl.BlockSpec((1, d), lambda i, f: (0, 0)),
            pl.BlockSpec((d, tf), lambda i, f: (0, f)),
            pl.BlockSpec((d, tf), lambda i, f: (0, f)),
            pl.BlockSpec((tf, d), lambda i, f: (f, 0)),
        ],
        out_specs=pl.BlockSpec((tm, d), lambda i, f: (i, 0)),
        out_shape=jax.ShapeDtypeStruct((n, d), F32),
        scratch_shapes=[pltpu.VMEM((tm, d), BF16), pltpu.VMEM((tm, d), F32)],
        compiler_params=_params("parallel", "arbitrary"),
        name="ffn",
    )(x, gain.reshape(1, d), wg, wu, wd)


C_Q = 0
C_KSEL = 512
C_KWIN = 640
C_KCMP = 768
C_VCMP = 896
C_VSEL = 1024
C_VWIN = 1152
C_U = 1280
C_CB = 1536
C_CC = 1792
C_CH = 2048
C_MG = 2304
C_NG = 5376
C_END = 5504
N_NORMED = 768


def _proj_kernel(x_ref, g_ref, w_ref, hg_ref, q_ref, kv_ref, win_ref, ksv_ref, u_ref, cb_ref, z_ref, mg_ref, ng_ref):
    h = _row_rms(x_ref[...], g_ref[...]).astype(BF16)

    def sec(lo, hi):
        return _dot(h, w_ref[:, lo:hi])

    nrm = _head_rms(sec(0, N_NORMED), hg_ref[...])
    ksel = nrm[:, C_KSEL:C_KSEL + KV_W]
    kwin = nrm[:, C_KWIN:C_KWIN + KV_W]
    q_ref[...] = (nrm[:, :NSA_W] * (HEAD_DIM ** -0.5)).astype(BF16)
    rest = sec(C_KCMP, C_U)
    vsel = rest[:, 2 * KV_W:3 * KV_W]
    vwin = rest[:, 3 * KV_W:4 * KV_W]
    kv_ref[:, 0:2 * KV_W] = rest[:, 0:2 * KV_W]
    kv_ref[:, 2 * KV_W:3 * KV_W] = ksel
    kv_ref[:, 3 * KV_W:4 * KV_W] = vsel
    win_ref[:, 0:KV_W] = kwin
    win_ref[:, KV_W:2 * KV_W] = vwin
    ksv_ref[:, 0:KV_W] = ksel.astype(BF16)
    ksv_ref[:, KV_W:2 * KV_W] = vsel.astype(BF16)
    ksv_ref[:, 2 * KV_W:3 * KV_W] = kwin.astype(BF16)
    ksv_ref[:, 3 * KV_W:4 * KV_W] = vwin.astype(BF16)
    u_ref[...] = sec(C_U, C_CB)
    cv = sec(C_CB, C_MG)
    cw = C_CC - C_CB
    cb_ref[...] = cv[:, 0:cw]
    z_ref[...] = cv[:, cw:2 * cw] * cv[:, 2 * cw:3 * cw]
    mg_ref[...] = sec(C_MG, C_NG)
    ng_ref[...] = sec(C_NG, C_END)


def _proj(x, gain, w, head_gain, tm):
    n, d = x.shape
    row = lambda w_: pl.BlockSpec((tm, w_), lambda i: (i, 0))
    full = lambda a: pl.BlockSpec(a.shape, lambda i: (0, 0))
    gain = gain.reshape(1, d)
    widths = ((NSA_W, BF16), (4 * KV_W, F32), (2 * KV_W, F32), (4 * KV_W, BF16), (C_CB - C_U, F32),
              (C_CC - C_CB, F32), (C_CC - C_CB, F32), (C_NG - C_MG, F32), (C_END - C_NG, F32))
    return pl.pallas_call(
        _proj_kernel,
        grid=(n // tm,),
        in_specs=[row(d), full(gain), full(w), full(head_gain)],
        out_specs=[row(w_) for w_, _ in widths],
        out_shape=[jax.ShapeDtypeStruct((n, w_), dt) for w_, dt in widths],
        compiler_params=_params("parallel"),
        name="proj",
    )(x, gain, w, head_gain)


def _compress_kernel(k_ref, v_ref, pe_ref, w1_ref, w2_ref, kg_ref, kc_ref, vc_ref, *, n_ch):
    outs = []
    for s, src_ref in enumerate((k_ref, v_ref)):
        first = jnp.zeros((n_ch, KV_W), F32)
        second = jnp.zeros((n_ch, KV_W), F32)
        for l in range(CMP_STRIDE):
            xl = src_ref[pl.ds(l, n_ch, stride=CMP_STRIDE), :]
            first += _dot((xl + pe_ref[s, l:l + 1, :]).astype(BF16), w1_ref[s, l])
            second += _dot((xl + pe_ref[s, CMP_STRIDE + l:CMP_STRIDE + l + 1, :]).astype(BF16),
                           w1_ref[s, CMP_STRIDE + l])
        hid = first + pltpu.roll(second, n_ch - 1, 0)
        act = hid * jax.nn.sigmoid(hid)
        outs.append(_dot(act.astype(BF16), w2_ref[s]))
    kc = _head_rms(outs[0], kg_ref[...])
    rows = kc_ref.shape[0]
    for ref, val in ((kc_ref, kc), (vc_ref, outs[1])):
        ref[0:CMP_PAD, :] = jnp.zeros((CMP_PAD, KV_W), F32)
        ref[CMP_PAD:CMP_PAD + n_ch, :] = val
        ref[CMP_PAD + n_ch:rows, :] = jnp.zeros((rows - CMP_PAD - n_ch, KV_W), F32)


def _cmp_rows(t):
    return -(-(CMP_PAD + t // CMP_STRIDE) // LANES) * LANES


def _compress_prompt(kv4, pe2, w1bd, w2bd, kgain, b, t):
    n_ch = t // CMP_STRIDE
    rows = _cmp_rows(t)
    full = lambda a: pl.BlockSpec(a.shape, lambda i: (0,) * a.ndim)
    out = jax.ShapeDtypeStruct((b, rows, KV_W), F32)
    return pl.pallas_call(
        functools.partial(_compress_kernel, n_ch=n_ch),
        grid=(b,),
        in_specs=[pl.BlockSpec((t, KV_W), lambda i: (i, 0)), pl.BlockSpec((t, KV_W), lambda i: (i, 1)),
                  full(pe2), full(w1bd), full(w2bd), full(kgain)],
        out_specs=[pl.BlockSpec((None, rows, KV_W), lambda i: (i, 0, 0))] * 2,
        out_shape=[out, out],
        compiler_params=_params("parallel"),
        name="compress_prompt",
    )(kv4, kv4, pe2, w1bd, w2bd, kgain)


def _softmax_tile(s, v, m_scr, l_scr, acc_scr):
    m_old = m_scr[...]
    m_new = jnp.maximum(m_old, jnp.max(s, axis=1, keepdims=True))
    m_safe = jnp.where(m_new == NEG_INF, 0.0, m_new)
    alpha = jnp.exp(m_old - m_safe)
    e = jnp.exp(s - m_safe)
    l_scr[...] = alpha * l_scr[...] + jnp.sum(e, axis=1, keepdims=True)
    acc_scr[...] = alpha * acc_scr[...] + _dot(e.astype(BF16), v)
    m_scr[...] = m_new


def _softmax_reset(m_scr, l_scr, acc_scr):
    m_scr[...] = jnp.full_like(m_scr, NEG_INF)
    l_scr[...] = jnp.zeros_like(l_scr)
    acc_scr[...] = jnp.zeros_like(acc_scr)


def _softmax_result(l_scr, acc_scr):
    l = l_scr[...]
    return acc_scr[...] / jnp.where(l > 0, l, 1.0)


def _top_k_mask(score, k):
    col = lax.broadcasted_iota(jnp.int32, score.shape, 1)
    sel = jnp.zeros(score.shape, F32)
    work = score
    for _ in range(k):
        m = jnp.max(work, axis=1, keepdims=True)
        first = jnp.min(jnp.where(work == m, col, score.shape[1]), axis=1, keepdims=True)
        pick = col == first
        sel = jnp.maximum(sel, pick.astype(F32))
        work = jnp.where(pick, NEG_INF, work)
    return sel


def _attn_prompt_kernel(q_ref, ng_ref, kc_ref, vc_ref, ksv_ref, smap_ref, d0_ref, d1_ref, bc_ref, o_ref,
                        m_scr, l_scr, acc_scr, ob_scr):
    i = pl.program_id(1)
    rows = HPG * Q_BLOCK
    lane = lax.broadcasted_iota(jnp.int32, (Q_BLOCK, LANES), 1)
    lane4 = lax.broadcasted_iota(jnp.int32, (rows, LANES), 1)
    trow4 = lax.broadcasted_iota(jnp.int32, (rows, LANES), 0) % Q_BLOCK
    near0 = pl.multiple_of(i * (Q_BLOCK // CMP_STRIDE), 8)
    state = (m_scr, l_scr, acc_scr)

    for g in range(N_KV_HEADS):
        in_group = (lane >= HEAD_DIM) == (g == 1)
        lhs = jnp.concatenate(
            [jnp.where(in_group, q_ref[:, h * LANES:(h + 1) * LANES], jnp.zeros((), BF16)) for h in range(HPG)], axis=0)

        s_far = _dot_nt(lhs, kc_ref[...].astype(BF16))
        pcol = lax.broadcasted_iota(jnp.int32, s_far.shape, 1)
        s_far = jnp.where((pcol >= CMP_PAD) & (pcol < near0), s_far, NEG_INF)
        s_near = _dot_nt(lhs, kc_ref[pl.ds(near0, LANES), :].astype(BF16)) + bc_ref[g]
        ncol = lax.broadcasted_iota(jnp.int32, s_near.shape, 1)
        s_near = jnp.where(near0 + ncol >= CMP_PAD, s_near, NEG_INF)
        m = jnp.maximum(jnp.max(s_far, axis=1, keepdims=True), jnp.max(s_near, axis=1, keepdims=True))
        m = jnp.where(m == NEG_INF, 0.0, m)
        e_far = jnp.exp(s_far - m)
        e_near = jnp.exp(s_near - m)
        den = jnp.sum(e_far, axis=1, keepdims=True) + jnp.sum(e_near, axis=1, keepdims=True)
        inv = 1.0 / jnp.where(den > 0, den, 1.0)
        p_far = e_far * inv
        p_near = e_near * inv
        ob_scr[g, 0] = (_dot(p_far.astype(BF16), vc_ref[...].astype(BF16))
                        + _dot(p_near.astype(BF16), vc_ref[pl.ds(near0, LANES), :].astype(BF16)))

        imp_far = sum(p_far[h * Q_BLOCK:(h + 1) * Q_BLOCK] for h in range(HPG))
        imp_near = sum(p_near[h * Q_BLOCK:(h + 1) * Q_BLOCK] for h in range(HPG))
        score = (_dot_split(imp_far, smap_ref[...].astype(BF16))
                 + _dot_split(imp_near, smap_ref[pl.ds(near0, LANES), :].astype(BF16)))
        jcol = lax.broadcasted_iota(jnp.int32, score.shape, 1)
        cur = 2 * i + (lax.broadcasted_iota(jnp.int32, score.shape, 0) >= SEL_BLOCK).astype(jnp.int32)
        forced = (jcol == 0) | (jcol == cur) | (jcol == cur - 1)
        score = jnp.where(forced, jnp.inf, score)
        score = jnp.where(jcol <= cur, score, NEG_INF)
        sel = _top_k_mask(score, TOP_K).astype(BF16)

        def sel_tile(j2, bias_ref):
            base = pl.multiple_of(j2 * Q_BLOCK, Q_BLOCK)
            s = _dot_nt(lhs, ksv_ref[pl.ds(base, Q_BLOCK), 0:KV_W])
            if bias_ref is not None:
                s = s + bias_ref[g]
            blk = 2 * j2 + (lane >= SEL_BLOCK).astype(jnp.int32)
            expand = (lax.broadcasted_iota(jnp.int32, (sel.shape[1], LANES), 0) == blk[:1]).astype(BF16)
            chosen = _dot(sel, expand)
            s = jnp.where(jnp.concatenate([chosen] * HPG, axis=0) > 0.5, s, NEG_INF)
            _softmax_tile(s, ksv_ref[pl.ds(base, Q_BLOCK), KV_W:2 * KV_W], *state)

        _softmax_reset(*state)
        sel_tile(i, d0_ref)

        @pl.when(i >= 1)
        def _():
            sel_tile(i - 1, d1_ref)

        def far_body(j2, carry):
            sel_tile(j2, None)
            return carry

        lax.fori_loop(0, jnp.maximum(i - 1, 0), far_body, 0)
        ob_scr[g, 1] = _softmax_result(l_scr, acc_scr)

        def win_tile(delta, bias_ref, extra_mask):
            base = pl.multiple_of((i - delta) * Q_BLOCK, Q_BLOCK)
            s = _dot_nt(lhs, ksv_ref[pl.ds(base, Q_BLOCK), 2 * KV_W:3 * KV_W])
            if bias_ref is not None:
                s = s + bias_ref[g]
            if extra_mask is not None:
                s = jnp.where(extra_mask, s, NEG_INF)
            _softmax_tile(s, ksv_ref[pl.ds(base, Q_BLOCK), 3 * KV_W:4 * KV_W], *state)

        _softmax_reset(*state)
        win_tile(0, d0_ref, None)
        n_win_tiles = WINDOW // Q_BLOCK
        for delta in range(1, n_win_tiles + 1):
            bias = d1_ref if delta == 1 else None
            mask = (lane4 > trow4) if delta == n_win_tiles else None
            pl.when(i >= delta)(functools.partial(win_tile, delta, bias, mask))
        ob_scr[g, 2] = _softmax_result(l_scr, acc_scr)

    sig = jax.nn.sigmoid(ng_ref[...])
    for h in range(HPG):
        slab = None
        for j in range(3):
            part = []
            for g in range(N_KV_HEADS):
                c = j * N_Q_HEADS + g * HPG + h
                part.append(sig[:, c:c + 1] * ob_scr[g, j, h * Q_BLOCK:(h + 1) * Q_BLOCK, :])
            term = jnp.where(lane < HEAD_DIM, part[0], part[1])
            slab = term if slab is None else slab + term
        o_ref[:, h * LANES:(h + 1) * LANES] = slab.astype(BF16)


def _attn_prompt(q, ng, kc, vc, ksv, smap, d0, d1, bc, b, t):
    nq = t // Q_BLOCK
    rows = HPG * Q_BLOCK
    full = lambda a: pl.BlockSpec(a.shape, lambda bi, i: (0,) * a.ndim)
    cmp_spec = pl.BlockSpec((None,) + kc.shape[1:], lambda bi, i: (bi, 0, 0))
    return pl.pallas_call(
        _attn_prompt_kernel,
        grid=(b, nq),
        in_specs=[
            pl.BlockSpec((Q_BLOCK, NSA_W), lambda bi, i: (bi * nq + i, 0)),
            pl.BlockSpec((Q_BLOCK, LANES), lambda bi, i: (bi * nq + i, 0)),
            cmp_spec, cmp_spec,
            pl.BlockSpec((t, 4 * KV_W), lambda bi, i: (bi, 0)),
            full(smap), full(d0), full(d1), full(bc),
        ],
        out_specs=pl.BlockSpec((Q_BLOCK, NSA_W), lambda bi, i: (bi * nq + i, 0)),
        out_shape=jax.ShapeDtypeStruct((b * t, NSA_W), BF16),
        scratch_shapes=[pltpu.VMEM((rows, 1), F32), pltpu.VMEM((rows, 1), F32), pltpu.VMEM((rows, LANES), F32),
                        pltpu.VMEM((N_KV_HEADS, 3, rows, LANES), F32)],
        compiler_params=_params("parallel", "arbitrary"),
        name="attn_prompt",
    )(q, ng, kc, vc, ksv, smap, d0, d1, bc)


def _mix_prompt_kernel(u_ref, uh_ref, z_ref, zh_ref, cb_ref, cw_ref, d_ref, c_ref, ext_scr, *, tm):
    i = pl.program_id(1)
    has_hist = (i > 0).astype(F32)
    width = u_ref.shape[1]
    lane = lax.broadcasted_iota(jnp.int32, (tm, width), 1)
    pos = i * tm + lax.broadcasted_iota(jnp.int32, (tm, width), 0)

    ext_scr[0:HALO, :] = uh_ref[...] * has_hist
    ext_scr[HALO:HALO + tm, :] = u_ref[...]
    run = u_ref[...]
    pooled = None
    gc = width // len(POOL_WINDOWS)
    for k in range(1, max(POOL_WINDOWS)):
        run = run + ext_scr[HALO - k:HALO - k + tm, :]
        w = k + 1
        if w in POOL_WINDOWS:
            grp = POOL_WINDOWS.index(w)
            mean = run / jnp.minimum(pos + 1, w).astype(F32)
            pooled = mean if pooled is None else jnp.where(lane >= grp * gc, mean, pooled)
    d_ref[...] = (pooled - u_ref[...]).astype(BF16)

    ext_scr[0:HALO, :] = zh_ref[...] * has_hist
    ext_scr[HALO:HALO + tm, :] = z_ref[...]
    y = None
    for j in range(CONV_K):
        off = HALO - (CONV_K - 1) + j
        term = cw_ref[j:j + 1, :] * ext_scr[off:off + tm, :]
        y = term if y is None else y + term
    c_ref[...] = (cb_ref[...] * y).astype(BF16)


def _mix_prompt(u, z, cb, conv_w, b, t, tm):
    nt = t // tm
    width = u.shape[1]
    row = pl.BlockSpec((tm, width), lambda bi, i: (bi * nt + i, 0))
    halo = pl.BlockSpec((HALO, width), lambda bi, i: (jnp.maximum((bi * nt + i) * (tm // HALO) - 1, 0), 0))
    out = jax.ShapeDtypeStruct((b * t, width), BF16)
    return pl.pallas_call(
        functools.partial(_mix_prompt_kernel, tm=tm),
        grid=(b, nt),
        in_specs=[row, halo, row, halo, row, pl.BlockSpec(conv_w.shape, lambda bi, i: (0, 0))],
        out_specs=[row, row],
        out_shape=[out, out],
        scratch_shapes=[pltpu.VMEM((HALO + tm, width), F32)],
        compiler_params=_params("parallel", "parallel"),
        name="mix_prompt",
    )(u, u, z, z, cb, conv_w)


def _merge_kernel(x_ref, a_ref, d_ref, c_ref, mg_ref, wa_ref, wp_ref, ps_ref, wc_ref, wo_ref, o_ref):
    dm = x_ref.shape[1]
    gates = jax.nn.sigmoid(mg_ref[...])
    m = gates[:, 0:dm] * _dot(a_ref[...], wa_ref[...])
    m = m + gates[:, dm:2 * dm] * (_dot(d_ref[...], wp_ref[...]) * ps_ref[...])
    m = m + gates[:, 2 * dm:3 * dm] * _dot(c_ref[...], wc_ref[...])
    o_ref[...] = x_ref[...] + _dot(m.astype(BF16), wo_ref[...])


def _merge(x, attn, d, c, mg, wa, wp, ps, wc, wo, tm):
    n, dm = x.shape
    row = lambda a: pl.BlockSpec((tm, a.shape[1]), lambda i: (i, 0))
    full = lambda a: pl.BlockSpec(a.shape, lambda i: (0, 0))
    ps = ps.reshape(1, dm)
    return pl.pallas_call(
        _merge_kernel,
        grid=(n // tm,),
        in_specs=[row(x), row(attn), row(d), row(c), row(mg), full(wa), full(wp), full(ps), full(wc), full(wo)],
        out_specs=row(x),
        out_shape=jax.ShapeDtypeStruct((n, dm), F32),
        compiler_params=_params("parallel"),
        name="merge",
    )(x, attn, d, c, mg, wa, wp, ps, wc, wo)


def _bucket_table(n):
    d = np.arange(n)
    nf = np.maximum(d, 1).astype(np.float32)
    large = MAX_EXACT + (np.log(nf / np.float32(MAX_EXACT)) / np.float32(math.log(MAX_DISTANCE / MAX_EXACT))
                         * np.float32(N_BUCKETS - MAX_EXACT)).astype(np.int32)
    return np.where(d < MAX_EXACT, d, np.minimum(large, N_BUCKETS - 1)).astype(np.int32)


def _bias_table(rel_bias, dist):
    r, c = dist.shape
    bkt = _bucket_table(max(int(dist.max()) + 1, MAX_DISTANCE))[np.clip(dist, 0, None)]
    tab = rel_bias - rel_bias[N_BUCKETS - 1:N_BUCKETS]
    vals = jnp.where(jnp.asarray(dist >= 0)[..., None], tab[jnp.asarray(bkt)], NEG_INF)
    return vals.transpose(2, 0, 1).reshape(N_KV_HEADS, HPG * r, c)


def _sel_map(n_cmp, n_rows, n_cols):
    r_sel = SEL_BLOCK // CMP_STRIDE
    r_cmp = CMP_BLOCK // CMP_STRIDE
    out = np.zeros((n_rows, n_cols), np.float32)
    for j in range(n_cols):
        for m in range(r_sel):
            for n in range(r_cmp):
                c = r_sel * j + m - n
                if 0 <= c < n_cmp:
                    out[CMP_PAD + c, j] += 1.0
    return out


def _rel_bucket(dist):
    n = jnp.maximum(dist, 0)
    nf = jnp.maximum(n, 1).astype(F32)
    large = MAX_EXACT + (jnp.log(nf / MAX_EXACT) / math.log(MAX_DISTANCE / MAX_EXACT)
                         * (N_BUCKETS - MAX_EXACT)).astype(jnp.int32)
    large = jnp.minimum(large, N_BUCKETS - 1)
    return jnp.where(n < MAX_EXACT, n, large)


def _masked_softmax(s, mask):
    s = jnp.where(mask, s.astype(F32), -jnp.inf)
    m = jnp.max(s, axis=-1, keepdims=True)
    m = jnp.where(jnp.isfinite(m), m, 0.0)
    e = jnp.where(mask, jnp.exp(s - m), 0.0)
    den = jnp.sum(e, axis=-1, keepdims=True)
    return e / jnp.where(den > 0, den, 1.0)


def _rms(x, g):
    return x * lax.rsqrt(jnp.mean(x * x, axis=-1, keepdims=True) + RMS_EPS) * g


def _compress_x(x_raw, pe, w1, w2):
    b, length = x_raw.shape[:2]
    r_cmp = CMP_BLOCK // CMP_STRIDE
    n_ch = length // CMP_STRIDE
    ch = x_raw[:, :n_ch * CMP_STRIDE].reshape(b, n_ch, CMP_STRIDE, N_KV_HEADS, HEAD_DIM)
    n_c = n_ch - r_cmp + 1
    blocks = jnp.concatenate([ch[:, r:r + n_c] for r in range(r_cmp)], axis=2)
    hid = jax.nn.silu(jnp.einsum('bnlgd,lde->bnge', blocks + pe[None, None, :, None, :], w1))
    return jnp.einsum('bnge,ed->bngd', hid, w2)


def _to_sel_blocks(x):
    b, length = x.shape[:2]
    n_s = -(-length // SEL_BLOCK)
    x = jnp.pad(x, ((0, 0), (0, n_s * SEL_BLOCK - length), (0, 0), (0, 0)))
    return x.reshape(b, n_s, SEL_BLOCK, N_KV_HEADS, HEAD_DIM).transpose(0, 3, 1, 2, 4)


def _slc_map(n_sel, n_cmp):
    r_sel = SEL_BLOCK // CMP_STRIDE
    r_cmp = CMP_BLOCK // CMP_STRIDE
    offs = np.array([m - n for m in range(r_sel) for n in range(r_cmp)])
    idx = r_sel * np.arange(n_sel)[:, None] + offs[None, :]
    ok = (idx >= 0) & (idx < n_cmp)
    return np.clip(idx, 0, n_cmp - 1), ok


def _nsa_attend(qf, q_pos, kc, vc, ks_blk, vs_blk, kw, vw, w_pos, gates, rel_bias):
    b, tq = qf.shape[:2]
    tab = rel_bias.reshape(N_BUCKETS, N_KV_HEADS, HPG)
    gi = jnp.arange(N_KV_HEADS)
    n_c = kc.shape[1]
    c_end = jnp.arange(n_c) * CMP_STRIDE + CMP_BLOCK - 1
    dist_c = q_pos[:, None] - c_end[None, :]
    bias_c = tab[_rel_bucket(dist_c)].transpose(0, 2, 3, 1)[None]
    s_c = jnp.einsum('btghd,bngd->btghn', qf, kc).astype(F32) + bias_c
    p_c = _masked_softmax(s_c, (dist_c >= 0)[None, :, None, None, :])
    o_c = jnp.einsum('btghn,bngd->btghd', p_c, vc)
    imp = jnp.sum(p_c, axis=3)
    n_s = ks_blk.shape[2]
    sidx, sok = _slc_map(n_s, n_c)
    score = jnp.sum(jnp.where(sok, imp[..., sidx], 0.0), axis=-1)
    jn = jnp.arange(n_s)
    cur = q_pos // SEL_BLOCK
    forced = (jn[None, :] == 0) | (jn[None, :] == cur[:, None]) | (jn[None, :] == cur[:, None] - 1)
    allowed = (jn[None, :] * SEL_BLOCK) <= q_pos[:, None]
    score = jnp.where(forced[None, :, None, :], jnp.inf, score)
    score = jnp.where(allowed[None, :, None, :], score, -jnp.inf)
    k_sel = min(TOP_K, n_s)
    _, idx = lax.top_k(score, k_sel)
    bi = jnp.arange(b)[:, None, None, None]
    kb = ks_blk[bi, gi[None, None, :, None], idx]
    vb = vs_blk[bi, gi[None, None, :, None], idx]
    pos_s = idx[..., None] * SEL_BLOCK + jnp.arange(SEL_BLOCK)
    dist_s = q_pos[None, :, None, None, None] - pos_s
    bias_s = tab[_rel_bucket(dist_s), gi[None, None, :, None, None]].transpose(0, 1, 2, 5, 3, 4)
    s_s = jnp.einsum('btghd,btgksd->btghks', qf, kb).astype(F32) + bias_s
    n_sk = k_sel * SEL_BLOCK
    mask_s = (dist_s >= 0).reshape(b, tq, N_KV_HEADS, 1, n_sk)
    p_s = _masked_softmax(s_s.reshape(b, tq, N_KV_HEADS, HPG, n_sk), mask_s)
    o_s = jnp.einsum('btghn,btgnd->btghd', p_s, vb.reshape(b, tq, N_KV_HEADS, n_sk, HEAD_DIM))
    dist_w = q_pos[:, None] - w_pos[None, :]
    mask_w = (dist_w >= 0) & (dist_w < WINDOW) & (w_pos[None, :] >= 0)
    bias_w = tab[_rel_bucket(dist_w)].transpose(0, 2, 3, 1)[None]
    s_w = jnp.einsum('btghd,bsgd->btghs', qf, kw).astype(F32) + bias_w
    p_w = _masked_softmax(s_w, mask_w[None, :, None, None, :])
    o_w = jnp.einsum('btghs,bsgd->btghd', p_w, vw)
    gt = jax.nn.sigmoid(gates.astype(F32))
    return gt[..., 0:1] * o_c + gt[..., 1:2] * o_s + gt[..., 2:3] * o_w


def _sample_mix(q_s, kv4_s, win_s, ng_s, u_s, cb_s, z_s, cache_l, state_win_l, state_pool_l, state_conv_l,
                page_table, rel_bias, k_gain, pe, w1, w2, conv_w, db, s):
    n_pages = page_table.shape[1]
    past = n_pages * cache_l.shape[1]
    w_keep = state_win_l.shape[1]
    q_pos = past + jnp.arange(s)
    w_pos = past - w_keep + jnp.arange(w_keep + s)
    qf = q_s.astype(F32).reshape(db, s, HPG, N_KV_HEADS, HEAD_DIM).transpose(0, 1, 3, 2, 4)
    kv = kv4_s.reshape(db, s, 4, N_KV_HEADS, HEAD_DIM)
    new_win = win_s.reshape(db, s, 2, N_KV_HEADS, HEAD_DIM)
    gates = ng_s[:, :3 * N_Q_HEADS].reshape(db, s, 3, N_KV_HEADS, HPG).transpose(0, 1, 3, 4, 2)
    past_kv = cache_l[page_table].reshape(db, past, 4, N_KV_HEADS, HEAD_DIM)
    kc = _rms(_compress_x(jnp.concatenate([past_kv[:, :, 0], kv[:, :, 0]], axis=1), pe[0], w1[0], w2[0]), k_gain)
    vc = _compress_x(jnp.concatenate([past_kv[:, :, 1], kv[:, :, 1]], axis=1), pe[1], w1[1], w2[1])
    ks_blk = _to_sel_blocks(jnp.concatenate([past_kv[:, :, 2], kv[:, :, 2]], axis=1))
    vs_blk = _to_sel_blocks(jnp.concatenate([past_kv[:, :, 3], kv[:, :, 3]], axis=1))
    win_f = jnp.concatenate([state_win_l, new_win], axis=1)
    attn = _nsa_attend(qf, q_pos, kc, vc, ks_blk, vs_blk, win_f[:, :, 0], win_f[:, :, 1], w_pos, gates, rel_bias)
    attn = attn.transpose(0, 1, 3, 2, 4).reshape(db * s, NSA_W).astype(BF16)

    u = u_s.reshape(db, s, -1)
    z = z_s.reshape(db, s, -1)
    u_ext = jnp.concatenate([state_pool_l, u], axis=1)
    z_ext = jnp.concatenate([state_conv_l, z], axis=1)
    hist = u_ext.shape[1] - s
    n_grp = len(POOL_WINDOWS)
    cs = jnp.cumsum(u_ext.astype(F32), axis=1)
    cs0 = jnp.concatenate([jnp.zeros_like(cs[:, :1]), cs], axis=1).reshape(db, hist + s + 1, n_grp, -1)
    r = hist + jnp.arange(s)
    win = jnp.array(POOL_WINDOWS, jnp.int32)
    lo = jnp.maximum(r[:, None] + 1 - win[None, :], 0)
    cnt = (r[:, None] + 1 - lo).astype(F32)
    hi_v = cs0[:, r + 1]
    lo_v = cs0[:, lo, jnp.arange(n_grp)[None, :]]
    u_t = u.reshape(db, s, n_grp, -1)
    d = ((hi_v - lo_v) / cnt[None, :, :, None] - u_t).reshape(db * s, -1).astype(BF16)
    y = sum(conv_w[j] * z_ext[:, j:j + s] for j in range(CONV_K))
    c = (cb_s * y.reshape(db * s, -1)).astype(BF16)
    return attn, d, c, win_f[:, s:], u_ext[:, s:], z_ext[:, s:]


def _proj_columns():
    q0, kv0, ng0 = 0, NSA_W, NSA_W + 6 * KV_W
    u0 = ng0 + 3 * N_Q_HEADS
    idx = np.full((C_END,), -1, np.int64)
    for k in range(HPG):
        for g in range(N_KV_HEADS):
            dst = C_Q + k * LANES + g * HEAD_DIM
            idx[dst:dst + HEAD_DIM] = q0 + (g * HPG + k) * HEAD_DIM + np.arange(HEAD_DIM)
    for dst, slot in ((C_KCMP, 0), (C_VCMP, 1), (C_KSEL, 2), (C_VSEL, 3), (C_KWIN, 4), (C_VWIN, 5)):
        idx[dst:dst + KV_W] = kv0 + slot * KV_W + np.arange(KV_W)
    idx[C_U:C_NG] = u0 + np.arange(C_NG - C_U)
    for j in range(3):
        for g in range(N_KV_HEADS):
            for h in range(HPG):
                idx[C_NG + j * N_Q_HEADS + g * HPG + h] = ng0 + (g * HPG + h) * 3 + j
    return idx


def _slab_rows():
    idx = np.zeros((NSA_W,), np.int64)
    for k in range(HPG):
        for g in range(N_KV_HEADS):
            dst = k * LANES + g * HEAD_DIM
            idx[dst:dst + HEAD_DIM] = (g * HPG + k) * HEAD_DIM + np.arange(HEAD_DIM)
    return idx


def _block_diag2(w):
    z = jnp.zeros_like(w)
    return jnp.concatenate([jnp.concatenate([w, z], axis=-1), jnp.concatenate([z, w], axis=-1)], axis=-2)


def _token_tile(n):
    for tm in (512, 256, 128):
        if n % tm == 0:
            return tm
    raise ValueError(f"token count {n} is not a multiple of 128")


def kernel(x_prompt, x_sample, cache_kv, state_win, state_pool, state_conv, page_table, rel_bias, ln_ffn1, w_ffn1_gate, w_ffn1_up, w_ffn1_down, ln_mix, w_in, q_norm, k_norm, cmp_pe, cmp_w1, cmp_w2, w_nsa_out, w_pool, pool_scale, conv_w, w_conv_out, w_o, ln_ffn2, w_ffn2_gate, w_ffn2_up, w_ffn2_down):
    b, t, dm = x_prompt.shape
    db, s, _ = x_sample.shape
    depth = w_in.shape[0]
    n_p, n_s = b * t, db * s
    assert t % Q_BLOCK == 0 and t >= WINDOW
    tm = _token_tile(n_p + n_s)
    tm_proj = min(tm, 256)
    tm_mix = _token_tile(t)

    cols = _proj_columns()
    w_proj = (jnp.take(w_in, jnp.asarray(np.clip(cols, 0, None)), axis=2)
              * jnp.asarray(cols >= 0, F32)).astype(BF16)
    head_gain = jnp.concatenate([jnp.tile(q_norm, (1, N_Q_HEADS)), jnp.tile(k_norm[:, 1], (1, N_KV_HEADS)),
                                 jnp.tile(k_norm[:, 2], (1, N_KV_HEADS))], axis=1).reshape(depth, 1, N_NORMED)
    cmp_gain = jnp.tile(k_norm[:, 0], (1, N_KV_HEADS)).reshape(depth, 1, KV_W)
    pe2 = jnp.tile(cmp_pe, (1, 1, 1, N_KV_HEADS))
    w1bd = _block_diag2(cmp_w1).astype(BF16)
    w2bd = _block_diag2(cmp_w2).astype(BF16)
    wa = jnp.take(w_nsa_out, jnp.asarray(_slab_rows()), axis=1).astype(BF16)
    n_grp, gc, ge = w_pool.shape[1:]
    wp = jnp.zeros((depth, n_grp * gc, n_grp * ge), F32)
    for gi in range(n_grp):
        wp = wp.at[:, gi * gc:(gi + 1) * gc, gi * ge:(gi + 1) * ge].set(w_pool[:, gi])
    wp = wp.astype(BF16)
    wc = w_conv_out.astype(BF16)
    wo = w_o.astype(BF16)
    ffn1 = [w.astype(BF16) for w in (w_ffn1_gate, w_ffn1_up, w_ffn1_down)]
    ffn2 = [w.astype(BF16) for w in (w_ffn2_gate, w_ffn2_up, w_ffn2_down)]

    ti = np.arange(Q_BLOCK)
    d0 = _bias_table(rel_bias, ti[:, None] - ti[None, :])
    d1 = _bias_table(rel_bias, Q_BLOCK + ti[:, None] - ti[None, :])
    near_c = np.arange(LANES)
    bc = _bias_table(rel_bias, ti[:, None] - CMP_STRIDE * (near_c[None, :] - CMP_PAD) - (CMP_BLOCK - 1))
    n_cmp = t // CMP_STRIDE - CMP_BLOCK // CMP_STRIDE + 1
    smap = jnp.asarray(_sel_map(n_cmp, _cmp_rows(t), -(-(t // SEL_BLOCK) // LANES) * LANES))

    x = jnp.concatenate([x_prompt.reshape(n_p, dm), x_sample.reshape(n_s, dm)], axis=0)
    outs = [[] for _ in range(8)]
    for l in range(depth):
        x = _ffn(x, ln_ffn1[l], ffn1[0][l], ffn1[1][l], ffn1[2][l], tm)
        q, kv4, win, ksv, u, cb, z, mg, ng = _proj(x, ln_mix[l], w_proj[l], head_gain[l], tm_proj)

        kc, vc = _compress_prompt(kv4, pe2[l], w1bd[l], w2bd[l], cmp_gain[l], b, t)
        attn_p = _attn_prompt(q, ng, kc, vc, ksv, smap, d0, d1, bc, b, t)
        d_p, c_p = _mix_prompt(u, z, cb, conv_w[l], b, t, tm_mix)

        attn_s, d_s, c_s, s_win, s_pool, s_conv = _sample_mix(
            q[n_p:], kv4[n_p:], win[n_p:], ng[n_p:], u[n_p:], cb[n_p:], z[n_p:], cache_kv[l], state_win[l],
            state_pool[l], state_conv[l], page_table, rel_bias, k_norm[l, 0], cmp_pe[l], cmp_w1[l], cmp_w2[l],
            conv_w[l], db, s)

        x = _merge(x, jnp.concatenate([attn_p, attn_s]), jnp.concatenate([d_p, d_s]), jnp.concatenate([c_p, c_s]),
                   mg, wa[l], wp[l], pool_scale[l], wc[l], wo[l], tm)
        x = _ffn(x, ln_ffn2[l], ffn2[0][l], ffn2[1][l], ffn2[2][l], tm)

        w_keep = min(WINDOW, t)
        outs[0].append(kv4[:n_p].reshape(b, t, 4, N_KV_HEADS, HEAD_DIM))
        outs[1].append(win[:n_p].reshape(b, t, 2, N_KV_HEADS, HEAD_DIM)[:, t - w_keep:])
        outs[2].append(u[:n_p].reshape(b, t, -1)[:, t - POOL_STATE:])
        outs[3].append(z[:n_p].reshape(b, t, -1)[:, t - (CONV_K - 1):])
        outs[4].append(kv4[n_p:].reshape(db, s, 4, N_KV_HEADS, HEAD_DIM))
        outs[5].append(s_win)
        outs[6].append(s_pool)
        outs[7].append(s_conv)

    y_prompt = x[:n_p].reshape(b, t, dm)
    y_sample = x[n_p:].reshape(db, s, dm)
    return (y_prompt, y_sample) + tuple(jnp.stack(o) for o in outs)
```

```python
import functools
import math

import numpy as np
import jax
import jax.numpy as jnp
from jax import lax
from jax.experimental import pallas as pl
from jax.experimental.pallas import tpu as pltpu

F32 = jnp.float32
BF16 = jnp.bfloat16
NEG_INF = float("-inf")

HEAD_DIM = 64
N_KV_HEADS = 2
HPG = 4
N_Q_HEADS = N_KV_HEADS * HPG
NSA_W = N_Q_HEADS * HEAD_DIM
KV_W = N_KV_HEADS * HEAD_DIM
CMP_BLOCK = 32
CMP_STRIDE = 16
SEL_BLOCK = 64
TOP_K = 8
WINDOW = 512
Q_BLOCK = 128
N_BUCKETS = 32
MAX_EXACT = 16
MAX_DISTANCE = 128
POOL_WINDOWS = (2, 4, 8, 16)
POOL_STATE = 15
CONV_K = 3
RMS_EPS = 1e-6
LANES = 128
HALO = 16
CMP_PAD = 112
VMEM_LIMIT = 56 * 1024 * 1024


def _dot(a, b):
    return jnp.dot(a, b, preferred_element_type=F32)


def _dot_nt(a, b):
    return lax.dot_general(a, b, (((1,), (1,)), ((), ())), preferred_element_type=F32)


def _dot_split(a, b):
    hi = a.astype(BF16)
    lo = (a - hi.astype(F32)).astype(BF16)
    return _dot(hi, b) + _dot(lo, b)


def _group_mean_sq(x):
    r = lax.broadcasted_iota(jnp.int32, (LANES, LANES), 0) // HEAD_DIM
    c = lax.broadcasted_iota(jnp.int32, (LANES, LANES), 1) // HEAD_DIM
    ones_bd = (r == c).astype(BF16)
    return _dot_split(x * x, ones_bd) * (1.0 / HEAD_DIM)


def _head_rms(x, gain):
    parts = []
    for k in range(x.shape[1] // LANES):
        xs = x[:, k * LANES:(k + 1) * LANES]
        parts.append(xs * lax.rsqrt(_group_mean_sq(xs) + RMS_EPS))
    y = parts[0] if len(parts) == 1 else jnp.concatenate(parts, axis=1)
    return y * gain


def _row_rms(x, gain):
    return x * lax.rsqrt(jnp.mean(x * x, axis=-1, keepdims=True) + RMS_EPS) * gain


def _params(*sem):
    return pltpu.CompilerParams(dimension_semantics=sem, vmem_limit_bytes=VMEM_LIMIT)


def _ffn_kernel(x_ref, g_ref, wg_ref, wu_ref, wd_ref, o_ref, h_scr, acc_scr):
    f = pl.program_id(1)

    @pl.when(f == 0)
    def _():
        h_scr[...] = _row_rms(x_ref[...], g_ref[...]).astype(BF16)
        acc_scr[...] = jnp.zeros_like(acc_scr)

    h = h_scr[...]
    a = _dot(h, wg_ref[...])
    b = _dot(h, wu_ref[...])
    act = (a * jax.nn.sigmoid(a) * b).astype(BF16)
    acc_scr[...] += _dot(act, wd_ref[...])

    @pl.when(f == pl.num_programs(1) - 1)
    def _():
        o_ref[...] = x_ref[...] + 0.5 * acc_scr[...]


def _ffn(x, gain, wg, wu, wd, tm):
    n, d = x.shape
    dff = wg.shape[1]
    tf = dff // 2
    return pl.pallas_call(
        _ffn_kernel,
        grid=(n // tm, dff // tf),
        in_specs=[
            pl.BlockSpec((tm, d), lambda i, f: (i, 0)),
            pl.BlockSpec((1, d), lambda i, f: (0, 0)),
            pl.BlockSpec((d, tf), lambda i, f: (0, f)),
            pl.BlockSpec((d, tf), lambda i, f: (0, f)),
            pl.BlockSpec((tf, d), lambda i, f: (f, 0)),
        ],
        out_specs=pl.BlockSpec((tm, d), lambda i, f: (i, 0)),
        out_shape=jax.ShapeDtypeStruct((n, d), F32),
        scratch_shapes=[pltpu.VMEM((tm, d), BF16), pltpu.VMEM((tm, d), F32)],
        compiler_params=_params("parallel", "arbitrary"),
        name="ffn",
    )(x, gain.reshape(1, d), wg, wu, wd)


C_Q = 0
C_KSEL = 512
C_KWIN = 640
C_KCMP = 768
C_VCMP = 896
C_VSEL = 1024
C_VWIN = 1152
C_U = 1280
C_CB = 1536
C_CC = 1792
C_CH = 2048
C_MG = 2304
C_NG = 5376
C_END = 5504
N_NORMED = 768


def _proj_kernel(x_ref, g_ref, w_ref, hg_ref, q_ref, kv_ref, win_ref, ksv_ref, u_ref, cb_ref, z_ref, mg_ref, ng_ref):
    h = _row_rms(x_ref[...], g_ref[...]).astype(BF16)

    def sec(lo, hi):
        return _dot(h, w_ref[:, lo:hi])

    nrm = _head_rms(sec(0, N_NORMED), hg_ref[...])
    ksel = nrm[:, C_KSEL:C_KSEL + KV_W]
    kwin = nrm[:, C_KWIN:C_KWIN + KV_W]
    q_ref[...] = (nrm[:, :NSA_W] * (HEAD_DIM ** -0.5)).astype(BF16)
    rest = sec(C_KCMP, C_U)
    vsel = rest[:, 2 * KV_W:3 * KV_W]
    vwin = rest[:, 3 * KV_W:4 * KV_W]
    kv_ref[:, 0:2 * KV_W] = rest[:, 0:2 * KV_W]
    kv_ref[:, 2 * KV_W:3 * KV_W] = ksel
    kv_ref[:, 3 * KV_W:4 * KV_W] = vsel
    win_ref[:, 0:KV_W] = kwin
    win_ref[:, KV_W:2 * KV_W] = vwin
    ksv_ref[:, 0:KV_W] = ksel.astype(BF16)
    ksv_ref[:, KV_W:2 * KV_W] = vsel.astype(BF16)
    ksv_ref[:, 2 * KV_W:3 * KV_W] = kwin.astype(BF16)
    ksv_ref[:, 3 * KV_W:4 * KV_W] = vwin.astype(BF16)
    u_ref[...] = sec(C_U, C_CB)
    cv = sec(C_CB, C_MG)
    cw = C_CC - C_CB
    cb_ref[...] = cv[:, 0:cw]
    z_ref[...] = cv[:, cw:2 * cw] * cv[:, 2 * cw:3 * cw]
    mg_ref[...] = sec(C_MG, C_NG)
    ng_ref[...] = sec(C_NG, C_END)


def _proj(x, gain, w, head_gain, tm):
    n, d = x.shape
    row = lambda w_: pl.BlockSpec((tm, w_), lambda i: (i, 0))
    full = lambda a: pl.BlockSpec(a.shape, lambda i: (0, 0))
    gain = gain.reshape(1, d)
    widths = ((NSA_W, BF16), (4 * KV_W, F32), (2 * KV_W, F32), (4 * KV_W, BF16), (C_CB - C_U, F32),
              (C_CC - C_CB, F32), (C_CC - C_CB, F32), (C_NG - C_MG, F32), (C_END - C_NG, F32))
    return pl.pallas_call(
        _proj_kernel,
        grid=(n // tm,),
        in_specs=[row(d), full(gain), full(w), full(head_gain)],
        out_specs=[row(w_) for w_, _ in widths],
        out_shape=[jax.ShapeDtypeStruct((n, w_), dt) for w_, dt in widths],
        compiler_params=_params("parallel"),
        name="proj",
    )(x, gain, w, head_gain)


def _compress_kernel(k_ref, v_ref, pe_ref, w1_ref, w2_ref, kg_ref, kc_ref, vc_ref, *, n_ch):
    outs = []
    for s, src_ref in enumerate((k_ref, v_ref)):
        first = jnp.zeros((n_ch, KV_W), F32)
        second = jnp.zeros((n_ch, KV_W), F32)
        for l in range(CMP_STRIDE):
            xl = src_ref[pl.ds(l, n_ch, stride=CMP_STRIDE), :]
            first += _dot((xl + pe_ref[s, l:l + 1, :]).astype(BF16), w1_ref[s, l])
            second += _dot((xl + pe_ref[s, CMP_STRIDE + l:CMP_STRIDE + l + 1, :]).astype(BF16),
                           w1_ref[s, CMP_STRIDE + l])
        hid = first + pltpu.roll(second, n_ch - 1, 0)
        act = hid * jax.nn.sigmoid(hid)
        outs.append(_dot(act.astype(BF16), w2_ref[s]))
    kc = _head_rms(outs[0], kg_ref[...])
    rows = kc_ref.shape[0]
    for ref, val in ((kc_ref, kc), (vc_ref, outs[1])):
        ref[0:CMP_PAD, :] = jnp.zeros((CMP_PAD, KV_W), F32)
        ref[CMP_PAD:CMP_PAD + n_ch, :] = val
        ref[CMP_PAD + n_ch:rows, :] = jnp.zeros((rows - CMP_PAD - n_ch, KV_W), F32)


def _cmp_rows(t):
    return -(-(CMP_PAD + t // CMP_STRIDE) // LANES) * LANES


def _compress_prompt(kv4, pe2, w1bd, w2bd, kgain, b, t):
    n_ch = t // CMP_STRIDE
    rows = _cmp_rows(t)
    full = lambda a: pl.BlockSpec(a.shape, lambda i: (0,) * a.ndim)
    out = jax.ShapeDtypeStruct((b, rows, KV_W), F32)
    return pl.pallas_call(
        functools.partial(_compress_kernel, n_ch=n_ch),
        grid=(b,),
        in_specs=[pl.BlockSpec((t, KV_W), lambda i: (i, 0)), pl.BlockSpec((t, KV_W), lambda i: (i, 1)),
                  full(pe2), full(w1bd), full(w2bd), full(kgain)],
        out_specs=[pl.BlockSpec((None, rows, KV_W), lambda i: (i, 0, 0))] * 2,
        out_shape=[out, out],
        compiler_params=_params("parallel"),
        name="compress_prompt",
    )(kv4, kv4, pe2, w1bd, w2bd, kgain)


def _softmax_tile(s, v, m_scr, l_scr, acc_scr):
    m_old = m_scr[...]
    m_new = jnp.maximum(m_old, jnp.max(s, axis=1, keepdims=True))
    m_safe = jnp.where(m_new == NEG_INF, 0.0, m_new)
    alpha = jnp.exp(m_old - m_safe)
    e = jnp.exp(s - m_safe)
    l_scr[...] = alpha * l_scr[...] + jnp.sum(e, axis=1, keepdims=True)
    acc_scr[...] = alpha * acc_scr[...] + _dot(e.astype(BF16), v)
    m_scr[...] = m_new


def _softmax_reset(m_scr, l_scr, acc_scr):
    m_scr[...] = jnp.full_like(m_scr, NEG_INF)
    l_scr[...] = jnp.zeros_like(l_scr)
    acc_scr[...] = jnp.zeros_like(acc_scr)


def _softmax_result(l_scr, acc_scr):
    l = l_scr[...]
    return acc_scr[...] / jnp.where(l > 0, l, 1.0)


def _top_k_mask(score, k):
    col = lax.broadcasted_iota(jnp.int32, score.shape, 1)
    sel = jnp.zeros(score.shape, F32)
    work = score
    for _ in range(k):
        m = jnp.max(work, axis=1, keepdims=True)
        first = jnp.min(jnp.where(work == m, col, score.shape[1]), axis=1, keepdims=True)
        pick = col == first
        sel = jnp.maximum(sel, pick.astype(F32))
        work = jnp.where(pick, NEG_INF, work)
    return sel


def _attn_prompt_kernel(q_ref, ng_ref, kc_ref, vc_ref, ksv_ref, smap_ref, d0_ref, d1_ref, bc_ref, o_ref,
                        m_scr, l_scr, acc_scr, ob_scr):
    i = pl.program_id(1)
    rows = HPG * Q_BLOCK
    lane = lax.broadcasted_iota(jnp.int32, (Q_BLOCK, LANES), 1)
    lane4 = lax.broadcasted_iota(jnp.int32, (rows, LANES), 1)
    trow4 = lax.broadcasted_iota(jnp.int32, (rows, LANES), 0) % Q_BLOCK
    near0 = pl.multiple_of(i * (Q_BLOCK // CMP_STRIDE), 8)
    state = (m_scr, l_scr, acc_scr)

    for g in range(N_KV_HEADS):
        in_group = (lane >= HEAD_DIM) == (g == 1)
        lhs = jnp.concatenate(
            [jnp.where(in_group, q_ref[:, h * LANES:(h + 1) * LANES], jnp.zeros((), BF16)) for h in range(HPG)], axis=0)

        s_far = _dot_nt(lhs, kc_ref[...].astype(BF16))
        pcol = lax.broadcasted_iota(jnp.int32, s_far.shape, 1)
        s_far = jnp.where((pcol >= CMP_PAD) & (pcol < near0), s_far, NEG_INF)
        s_near = _dot_nt(lhs, kc_ref[pl.ds(near0, LANES), :].astype(BF16)) + bc_ref[g]
        ncol = lax.broadcasted_iota(jnp.int32, s_near.shape, 1)
        s_near = jnp.where(near0 + ncol >= CMP_PAD, s_near, NEG_INF)
        m = jnp.maximum(jnp.max(s_far, axis=1, keepdims=True), jnp.max(s_near, axis=1, keepdims=True))
        m = jnp.where(m == NEG_INF, 0.0, m)
        e_far = jnp.exp(s_far - m)
        e_near = jnp.exp(s_near - m)
        den = jnp.sum(e_far, axis=1, keepdims=True) + jnp.sum(e_near, axis=1, keepdims=True)
        inv = 1.0 / jnp.where(den > 0, den, 1.0)
        p_far = e_far * inv
        p_near = e_near * inv
        ob_scr[g, 0] = (_dot(p_far.astype(BF16), vc_ref[...].astype(BF16))
                        + _dot(p_near.astype(BF16), vc_ref[pl.ds(near0, LANES), :].astype(BF16)))

        imp_far = sum(p_far[h * Q_BLOCK:(h + 1) * Q_BLOCK] for h in range(HPG))
        imp_near = sum(p_near[h * Q_BLOCK:(h + 1) * Q_BLOCK] for h in range(HPG))
        score = (_dot_split(imp_far, smap_ref[...].astype(BF16))
                 + _dot_split(imp_near, smap_ref[pl.ds(near0, LANES), :].astype(BF16)))
        jcol = lax.broadcasted_iota(jnp.int32, score.shape, 1)
        cur = 2 * i + (lax.broadcasted_iota(jnp.int32, score.shape, 0) >= SEL_BLOCK).astype(jnp.int32)
        forced = (jcol == 0) | (jcol == cur) | (jcol == cur - 1)
        score = jnp.where(forced, jnp.inf, score)
        score = jnp.where(jcol <= cur, score, NEG_INF)
        sel = _top_k_mask(score, TOP_K).astype(BF16)

        def sel_tile(j2, bias_ref):
            base = pl.multiple_of(j2 * Q_BLOCK, Q_BLOCK)
            s = _dot_nt(lhs, ksv_ref[pl.ds(base, Q_BLOCK), 0:KV_W])
            if bias_ref is not None:
                s = s + bias_ref[g]
            blk = 2 * j2 + (lane >= SEL_BLOCK).astype(jnp.int32)
            expand = (lax.broadcasted_iota(jnp.int32, (sel.shape[1], LANES), 0) == blk[:1]).astype(BF16)
            chosen = _dot(sel, expand)
            s = jnp.where(jnp.concatenate([chosen] * HPG, axis=0) > 0.5, s, NEG_INF)
            _softmax_tile(s, ksv_ref[pl.ds(base, Q_BLOCK), KV_W:2 * KV_W], *state)

        _softmax_reset(*state)
        sel_tile(i, d0_ref)

        @pl.when(i >= 1)
        def _():
            sel_tile(i - 1, d1_ref)

        def far_body(j2, carry):
            sel_tile(j2, None)
            return carry

        lax.fori_loop(0, jnp.maximum(i - 1, 0), far_body, 0)
        ob_scr[g, 1] = _softmax_result(l_scr, acc_scr)

        def win_tile(delta, bias_ref, extra_mask):
            base = pl.multiple_of((i - delta) * Q_BLOCK, Q_BLOCK)
            s = _dot_nt(lhs, ksv_ref[pl.ds(base, Q_BLOCK), 2 * KV_W:3 * KV_W])
            if bias_ref is not None:
                s = s + bias_ref[g]
            if extra_mask is not None:
                s = jnp.where(extra_mask, s, NEG_INF)
            _softmax_tile(s, ksv_ref[pl.ds(base, Q_BLOCK), 3 * KV_W:4 * KV_W], *state)

        _softmax_reset(*state)
        win_tile(0, d0_ref, None)
        n_win_tiles = WINDOW // Q_BLOCK
        for delta in range(1, n_win_tiles + 1):
            bias = d1_ref if delta == 1 else None
            mask = (lane4 > trow4) if delta == n_win_tiles else None
            pl.when(i >= delta)(functools.partial(win_tile, delta, bias, mask))
        ob_scr[g, 2] = _softmax_result(l_scr, acc_scr)

    sig = jax.nn.sigmoid(ng_ref[...])
    for h in range(HPG):
        slab = None
        for j in range(3):
            part = []
            for g in range(N_KV_HEADS):
                c = j * N_Q_HEADS + g * HPG + h
                part.append(sig[:, c:c + 1] * ob_scr[g, j, h * Q_BLOCK:(h + 1) * Q_BLOCK, :])
            term = jnp.where(lane < HEAD_DIM, part[0], part[1])
            slab = term if slab is None else slab + term
        o_ref[:, h * LANES:(h + 1) * LANES] = slab.astype(BF16)


def _attn_prompt(q, ng, kc, vc, ksv, smap, d0, d1, bc, b, t):
    nq = t // Q_BLOCK
    rows = HPG * Q_BLOCK
    full = lambda a: pl.BlockSpec(a.shape, lambda bi, i: (0,) * a.ndim)
    cmp_spec = pl.BlockSpec((None,) + kc.shape[1:], lambda bi, i: (bi, 0, 0))
    return pl.pallas_call(
        _attn_prompt_kernel,
        grid=(b, nq),
        in_specs=[
            pl.BlockSpec((Q_BLOCK, NSA_W), lambda bi, i: (bi * nq + i, 0)),
            pl.BlockSpec((Q_BLOCK, LANES), lambda bi, i: (bi * nq + i, 0)),
            cmp_spec, cmp_spec,
            pl.BlockSpec((t, 4 * KV_W), lambda bi, i: (bi, 0)),
            full(smap), full(d0), full(d1), full(bc),
        ],
        out_specs=pl.BlockSpec((Q_BLOCK, NSA_W), lambda bi, i: (bi * nq + i, 0)),
        out_shape=jax.ShapeDtypeStruct((b * t, NSA_W), BF16),
        scratch_shapes=[pltpu.VMEM((rows, 1), F32), pltpu.VMEM((rows, 1), F32), pltpu.VMEM((rows, LANES), F32),
                        pltpu.VMEM((N_KV_HEADS, 3, rows, LANES), F32)],
        compiler_params=_params("parallel", "arbitrary"),
        name="attn_prompt",
    )(q, ng, kc, vc, ksv, smap, d0, d1, bc)


def _pool_rows(ext_scr, base, u, pos):
    m, width = u.shape
    lane = lax.broadcasted_iota(jnp.int32, (m, width), 1)
    gc = width // len(POOL_WINDOWS)
    run = u
    pooled = None
    for k in range(1, max(POOL_WINDOWS)):
        run = run + ext_scr[base - k:base - k + m, :]
        w = k + 1
        if w in POOL_WINDOWS:
            mean = run / jnp.minimum(pos + 1, w).astype(F32)
            pooled = mean if pooled is None else jnp.where(lane >= POOL_WINDOWS.index(w) * gc, mean, pooled)
    return pooled - u


def _conv_rows(ext_scr, base, m, cw_ref):
    y = None
    for j in range(CONV_K):
        off = base - (CONV_K - 1) + j
        term = cw_ref[j:j + 1, :] * ext_scr[off:off + m, :]
        y = term if y is None else y + term
    return y


def _mix_prompt_kernel(u_ref, uh_ref, z_ref, zh_ref, cb_ref, cw_ref, d_ref, c_ref, ext_scr, *, tm):
    i = pl.program_id(1)
    has_hist = (i > 0).astype(F32)
    pos = i * tm + lax.broadcasted_iota(jnp.int32, u_ref.shape, 0)
    ext_scr[0:HALO, :] = uh_ref[...] * has_hist
    ext_scr[HALO:HALO + tm, :] = u_ref[...]
    d_ref[...] = _pool_rows(ext_scr, HALO, u_ref[...], pos).astype(BF16)
    ext_scr[0:HALO, :] = zh_ref[...] * has_hist
    ext_scr[HALO:HALO + tm, :] = z_ref[...]
    c_ref[...] = (cb_ref[...] * _conv_rows(ext_scr, HALO, tm, cw_ref)).astype(BF16)


def _mix_prompt(u, z, cb, conv_w, b, t, tm):
    nt = t // tm
    width = u.shape[1]
    row = pl.BlockSpec((tm, width), lambda bi, i: (bi * nt + i, 0))
    halo = pl.BlockSpec((HALO, width), lambda bi, i: (jnp.maximum((bi * nt + i) * (tm // HALO) - 1, 0), 0))
    out = jax.ShapeDtypeStruct((b * t, width), BF16)
    return pl.pallas_call(
        functools.partial(_mix_prompt_kernel, tm=tm),
        grid=(b, nt),
        in_specs=[row, halo, row, halo, row, pl.BlockSpec(conv_w.shape, lambda bi, i: (0, 0))],
        out_specs=[row, row],
        out_shape=[out, out],
        scratch_shapes=[pltpu.VMEM((HALO + tm, width), F32)],
        compiler_params=_params("parallel", "parallel"),
        name="mix_prompt",
    )(u, u, z, z, cb, conv_w)


def _merge_kernel(x_ref, a_ref, d_ref, c_ref, mg_ref, wa_ref, wp_ref, ps_ref, wc_ref, wo_ref, o_ref):
    dm = x_ref.shape[1]
    gates = jax.nn.sigmoid(mg_ref[...])
    m = gates[:, 0:dm] * _dot(a_ref[...], wa_ref[...])
    m = m + gates[:, dm:2 * dm] * (_dot(d_ref[...], wp_ref[...]) * ps_ref[...])
    m = m + gates[:, 2 * dm:3 * dm] * _dot(c_ref[...], wc_ref[...])
    o_ref[...] = x_ref[...] + _dot(m.astype(BF16), wo_ref[...])


def _merge(x, attn, d, c, mg, wa, wp, ps, wc, wo, tm):
    n, dm = x.shape
    row = lambda a: pl.BlockSpec((tm, a.shape[1]), lambda i: (i, 0))
    full = lambda a: pl.BlockSpec(a.shape, lambda i: (0, 0))
    ps = ps.reshape(1, dm)
    return pl.pallas_call(
        _merge_kernel,
        grid=(n // tm,),
        in_specs=[row(x), row(attn), row(d), row(c), row(mg), full(wa), full(wp), full(ps), full(wc), full(wo)],
        out_specs=row(x),
        out_shape=jax.ShapeDtypeStruct((n, dm), F32),
        compiler_params=_params("parallel"),
        name="merge",
    )(x, attn, d, c, mg, wa, wp, ps, wc, wo)


SAMPLE_ROWS = 8
SAMPLE_NB = 2
MASK_BIG = 1e30


def _cmp_partial_kernel(x_ref, pe_ref, w1_ref, o_ref, *, n_ch):
    first = jnp.zeros((n_ch, KV_W), F32)
    second = jnp.zeros((n_ch, KV_W), F32)
    for l in range(CMP_STRIDE):
        xl = x_ref[pl.ds(l, n_ch, stride=CMP_STRIDE), :]
        first += _dot((xl + pe_ref[l:l + 1, :]).astype(BF16), w1_ref[l])
        second += _dot((xl + pe_ref[CMP_STRIDE + l:CMP_STRIDE + l + 1, :]).astype(BF16), w1_ref[CMP_STRIDE + l])
    o_ref[:, 0:KV_W] = first
    o_ref[:, KV_W:2 * KV_W] = second


def _cmp_partial(cache2d, pe2, w1bd, layer, n_pool, page):
    ch = page // CMP_STRIDE
    pp = next(p for p in (64, 32, 16, 8, 4, 2, 1) if n_pool % p == 0)
    nb = n_pool // pp
    return pl.pallas_call(
        functools.partial(_cmp_partial_kernel, n_ch=pp * ch),
        grid=(nb, 2),
        in_specs=[pl.BlockSpec((pp * page, KV_W), lambda i, s: (layer * nb + i, s)),
                  pl.BlockSpec((None,) + pe2.shape[1:], lambda i, s: (s, 0, 0)),
                  pl.BlockSpec((None,) + w1bd.shape[1:], lambda i, s: (s, 0, 0, 0))],
        out_specs=pl.BlockSpec((pp * ch, 2 * KV_W), lambda i, s: (i, s)),
        out_shape=jax.ShapeDtypeStruct((n_pool * ch, 4 * KV_W), F32),
        compiler_params=_params("parallel", "arbitrary"),
        name="cmp_partial",
    )(cache2d, pe2, w1bd)


def _softmax_pair(s_a, v_a, s_b, v_b):
    m = jnp.maximum(jnp.max(s_a, axis=1, keepdims=True), jnp.max(s_b, axis=1, keepdims=True))
    e_a = jnp.exp(s_a - m)
    e_b = jnp.exp(s_b - m)
    den = jnp.sum(e_a, axis=1, keepdims=True) + jnp.sum(e_b, axis=1, keepdims=True)
    return (_dot(e_a.astype(BF16), v_a) + _dot(e_b.astype(BF16), v_b)) / den


def _sample_kernel(pt_ref, q_ref, ng_ref, kv_ref, win_ref, u_ref, z_ref, cb_ref, swin_ref, spool_ref, sconv_ref,
                   cache_ref, fs_ref, w2_ref, kg_ref, cw_ref, smap_ref, emat_ref, bc_ref, dsel_ref, dwin_ref, dnew_ref,
                   attn_ref, d_ref, c_ref, owin_ref, opool_ref, oconv_ref,
                   kvbuf, fsbuf, sems, ext_scr, *, layer_base, n_pages, page, s_len, past):
    i = pl.program_id(0)
    slot = i % 2
    ch = page // CMP_STRIDE
    sr = SAMPLE_ROWS

    def copies(step, slot_):
        out = []
        for sb in range(SAMPLE_NB):
            for pg in range(n_pages):
                pid = pt_ref[step * SAMPLE_NB + sb, pg]
                out.append(pltpu.make_async_copy(
                    cache_ref.at[layer_base + pid, :, pl.ds(2 * KV_W, 2 * KV_W)],
                    kvbuf.at[slot_, sb, pl.ds(pg * page, page), :], sems.at[slot_, 0]))
                out.append(pltpu.make_async_copy(
                    fs_ref.at[pid], fsbuf.at[slot_, sb, pl.ds(pg * ch, ch), :], sems.at[slot_, 1]))
        return out

    @pl.when(i == 0)
    def _():
        for c in copies(0, 0):
            c.start()

    @pl.when(i + 1 < pl.num_programs(0))
    def _():
        for c in copies(i + 1, 1 - slot):
            c.start()

    for c in copies(i, slot):
        c.wait()

    lane = lax.broadcasted_iota(jnp.int32, (sr, LANES), 1)
    grp_rows = N_KV_HEADS * sr
    q_all = q_ref[...].astype(F32)
    sig_all = jax.nn.sigmoid(ng_ref[...])
    kvn = kv_ref[...]
    winn = win_ref[...]
    pad_rows = jnp.zeros((LANES - sr, KV_W), F32)
    w_keep = swin_ref.shape[1]
    hist = spool_ref.shape[1]
    ck = sconv_ref.shape[1]
    attn_rows, d_rows, c_rows = [], [], []

    for sb in range(SAMPLE_NB):
        r0 = sb * sr
        q8 = q_all[r0:r0 + sr]
        pieces = []
        for h in range(HPG):
            for g in range(N_KV_HEADS):
                pieces.append(jnp.where((lane >= HEAD_DIM) == (g == 1), q8[:, h * LANES:(h + 1) * LANES], 0.0))
        lhs = jnp.concatenate(pieces, axis=0).astype(BF16)

        def new_rows(x8):
            return jnp.concatenate([x8, pad_rows], axis=0).astype(BF16)

        ks_new = new_rows(kvn[r0:r0 + sr, 2 * KV_W:3 * KV_W])
        vs_new = new_rows(kvn[r0:r0 + sr, 3 * KV_W:4 * KV_W])
        kw_new = new_rows(winn[r0:r0 + sr, 0:KV_W])
        vw_new = new_rows(winn[r0:r0 + sr, KV_W:2 * KV_W])

        fs = fsbuf[slot, sb]
        n_ch = fs.shape[0]
        cmp = []
        for s_ in range(2):
            hid = (fs[:, 2 * s_ * KV_W:(2 * s_ + 1) * KV_W]
                   + pltpu.roll(fs[:, (2 * s_ + 1) * KV_W:(2 * s_ + 2) * KV_W], n_ch - 1, 0))
            cmp.append(_dot((hid * jax.nn.sigmoid(hid)).astype(BF16), w2_ref[s_]))
        kc = _head_rms(cmp[0], kg_ref[...]).astype(BF16)
        vc = cmp[1].astype(BF16)
        s_c = _dot_nt(lhs, kc) + bc_ref[...]
        e_c = jnp.exp(s_c - jnp.max(s_c, axis=1, keepdims=True))
        p_c = e_c / jnp.sum(e_c, axis=1, keepdims=True)
        o_c = _dot(p_c.astype(BF16), vc)

        imp = sum(p_c[h * grp_rows:(h + 1) * grp_rows] for h in range(HPG))
        score = _dot_split(imp, smap_ref[...])
        jcol = lax.broadcasted_iota(jnp.int32, score.shape, 1)
        cur = (past + lax.broadcasted_iota(jnp.int32, score.shape, 0) % sr) // SEL_BLOCK
        forced = (jcol == 0) | (jcol == cur) | (jcol == cur - 1)
        score = jnp.where(forced, jnp.inf, score)
        score = jnp.where(jcol <= cur, score, NEG_INF)
        sel = _top_k_mask(score, TOP_K)
        not_sel = jnp.concatenate([(sel - 1.0) * MASK_BIG] * HPG, axis=0).astype(BF16)
        lhs_sel = jnp.concatenate([lhs, not_sel], axis=1)

        k_past = jnp.concatenate([kvbuf[slot, sb, :, 0:KV_W].astype(BF16), emat_ref[...]], axis=1)
        o_s = _softmax_pair(_dot_nt(lhs_sel, k_past) + dsel_ref[...], kvbuf[slot, sb, :, KV_W:2 * KV_W].astype(BF16),
                            _dot_nt(lhs, ks_new) + dnew_ref[...], vs_new)
        o_w = _softmax_pair(_dot_nt(lhs, swin_ref[sb, :, 0:KV_W].astype(BF16)) + dwin_ref[...],
                            swin_ref[sb, :, KV_W:2 * KV_W].astype(BF16),
                            _dot_nt(lhs, kw_new) + dnew_ref[...], vw_new)

        branch = (o_c, o_s, o_w)
        sig8 = sig_all[r0:r0 + sr]
        slabs = []
        for h in range(HPG):
            slab = None
            for j in range(3):
                part = []
                for g in range(N_KV_HEADS):
                    c = j * N_Q_HEADS + g * HPG + h
                    lo = h * grp_rows + g * sr
                    part.append(sig8[:, c:c + 1] * branch[j][lo:lo + sr])
                term = jnp.where(lane < HEAD_DIM, part[0], part[1])
                slab = term if slab is None else slab + term
            slabs.append(slab)
        attn_rows.append(jnp.concatenate(slabs, axis=1))

        owin_ref[sb, 0:w_keep - s_len, :] = swin_ref[sb, s_len:w_keep, :]
        owin_ref[sb, w_keep - s_len:w_keep, :] = winn[r0:r0 + s_len, :]

        u8 = u_ref[r0:r0 + sr, :]
        ext_scr[0:hist, :] = spool_ref[sb]
        ext_scr[hist:hist + sr, :] = u8
        d_rows.append(_pool_rows(ext_scr, hist, u8, hist + lax.broadcasted_iota(jnp.int32, u8.shape, 0)))
        opool_ref[sb] = ext_scr[s_len:s_len + hist, :]
        ext_scr[0:ck, :] = sconv_ref[sb]
        ext_scr[ck:ck + sr, :] = z_ref[r0:r0 + sr, :]
        c_rows.append(cb_ref[r0:r0 + sr, :] * _conv_rows(ext_scr, ck, sr, cw_ref))
        oconv_ref[sb] = ext_scr[s_len:s_len + ck, :]

    attn_ref[...] = jnp.concatenate(attn_rows, axis=0).astype(BF16)
    d_ref[...] = jnp.concatenate(d_rows, axis=0).astype(BF16)
    c_ref[...] = jnp.concatenate(c_rows, axis=0).astype(BF16)


def _sample_layer(layer, page_table, q, ng, kv4, win, u, z, cb, swin3, spool3, sconv3, cache3, fs3, w2bd, kgain,
                  conv_w, smap_s, emat, bc_s, dsel, dwin, dnew, n_p, db, s_len, n_pool):
    nsteps = db // SAMPLE_NB
    rb = SAMPLE_NB * SAMPLE_ROWS
    r0 = n_p // rb
    page = cache3.shape[1]
    n_pages = page_table.shape[1]
    past = n_pages * page
    tok = lambda a: pl.BlockSpec((rb, a.shape[1]), lambda i, pt: (r0 + i, 0))
    state = lambda a: pl.BlockSpec((SAMPLE_NB,) + a.shape[1:], lambda i, pt: (layer * nsteps + i, 0, 0))
    full = lambda a: pl.BlockSpec(a.shape, lambda i, pt: (0,) * a.ndim)
    hbm = pl.BlockSpec(memory_space=pl.ANY)
    out_tok = lambda w_: pl.BlockSpec((rb, w_), lambda i, pt: (i, 0))
    out_state = lambda a: pl.BlockSpec((SAMPLE_NB,) + a.shape[1:], lambda i, pt: (i, 0, 0))
    width = u.shape[1]
    n_tok = db * SAMPLE_ROWS
    return pl.pallas_call(
        functools.partial(_sample_kernel, layer_base=layer * n_pool, n_pages=n_pages, page=page, s_len=s_len, past=past),
        grid_spec=pltpu.PrefetchScalarGridSpec(
            num_scalar_prefetch=1,
            grid=(nsteps,),
            in_specs=[tok(q), tok(ng), tok(kv4), tok(win), tok(u), tok(z), tok(cb), state(swin3), state(spool3),
                      state(sconv3), hbm, hbm, full(w2bd), full(kgain), full(conv_w), full(smap_s), full(emat),
                      full(bc_s), full(dsel), full(dwin), full(dnew)],
            out_specs=[out_tok(NSA_W), out_tok(width), out_tok(width), out_state(swin3), out_state(spool3),
                       out_state(sconv3)],
            scratch_shapes=[pltpu.VMEM((2, SAMPLE_NB, past, 2 * KV_W), F32),
                            pltpu.VMEM((2, SAMPLE_NB, past // CMP_STRIDE, 4 * KV_W), F32),
                            pltpu.SemaphoreType.DMA((2, 2)),
                            pltpu.VMEM((4 * SAMPLE_ROWS, width), F32)]),
        out_shape=[jax.ShapeDtypeStruct((n_tok, NSA_W), BF16), jax.ShapeDtypeStruct((n_tok, width), BF16),
                   jax.ShapeDtypeStruct((n_tok, width), BF16), jax.ShapeDtypeStruct((db,) + swin3.shape[1:], F32),
                   jax.ShapeDtypeStruct((db,) + spool3.shape[1:], F32),
                   jax.ShapeDtypeStruct((db,) + sconv3.shape[1:], F32)],
        compiler_params=_params("arbitrary"),
        name="sample_mix",
    )(page_table, q, ng, kv4, win, u, z, cb, swin3, spool3, sconv3, cache3, fs3, w2bd, kgain, conv_w, smap_s, emat,
      bc_s, dsel, dwin, dnew)


def _bucket_table(n):
    d = np.arange(n)
    nf = np.maximum(d, 1).astype(np.float32)
    large = MAX_EXACT + (np.log(nf / np.float32(MAX_EXACT)) / np.float32(math.log(MAX_DISTANCE / MAX_EXACT))
                         * np.float32(N_BUCKETS - MAX_EXACT)).astype(np.int32)
    return np.where(d < MAX_EXACT, d, np.minimum(large, N_BUCKETS - 1)).astype(np.int32)


def _bias_values(rel_bias, dist, visible=None):
    visible = (dist >= 0) if visible is None else (visible & (dist >= 0))
    bkt = _bucket_table(max(int(dist.max()) + 1, MAX_DISTANCE))[np.clip(dist, 0, None)]
    tab = rel_bias - rel_bias[N_BUCKETS - 1:N_BUCKETS]
    vals = jnp.where(jnp.asarray(visible)[..., None], tab[jnp.asarray(bkt)], NEG_INF)
    return vals.transpose(2, 0, 1).reshape((N_KV_HEADS, HPG) + dist.shape)


def _bias_table(rel_bias, dist):
    r, c = dist.shape
    return _bias_values(rel_bias, dist).reshape(N_KV_HEADS, HPG * r, c)


def _bias_table_sample(rel_bias, dist, visible=None):
    r, c = dist.shape
    return _bias_values(rel_bias, dist, visible).transpose(1, 0, 2, 3).reshape(HPG * N_KV_HEADS * r, c)


def _sel_map(n_cmp, n_rows, n_cols, pad):
    r_sel = SEL_BLOCK // CMP_STRIDE
    r_cmp = CMP_BLOCK // CMP_STRIDE
    out = np.zeros((n_rows, n_cols), np.float32)
    for j in range(n_cols):
        for m in range(r_sel):
            for n in range(r_cmp):
                c = r_sel * j + m - n
                if 0 <= c < n_cmp:
                    out[pad + c, j] += 1.0
    return out


def _proj_columns():
    q0, kv0, ng0 = 0, NSA_W, NSA_W + 6 * KV_W
    u0 = ng0 + 3 * N_Q_HEADS
    idx = np.full((C_END,), -1, np.int64)
    for k in range(HPG):
        for g in range(N_KV_HEADS):
            dst = C_Q + k * LANES + g * HEAD_DIM
            idx[dst:dst + HEAD_DIM] = q0 + (g * HPG + k) * HEAD_DIM + np.arange(HEAD_DIM)
    for dst, slot in ((C_KCMP, 0), (C_VCMP, 1), (C_KSEL, 2), (C_VSEL, 3), (C_KWIN, 4), (C_VWIN, 5)):
        idx[dst:dst + KV_W] = kv0 + slot * KV_W + np.arange(KV_W)
    idx[C_U:C_NG] = u0 + np.arange(C_NG - C_U)
    for j in range(3):
        for g in range(N_KV_HEADS):
            for h in range(HPG):
                idx[C_NG + j * N_Q_HEADS + g * HPG + h] = ng0 + (g * HPG + h) * 3 + j
    return idx


def _slab_rows():
    idx = np.zeros((NSA_W,), np.int64)
    for k in range(HPG):
        for g in range(N_KV_HEADS):
            dst = k * LANES + g * HEAD_DIM
            idx[dst:dst + HEAD_DIM] = (g * HPG + k) * HEAD_DIM + np.arange(HEAD_DIM)
    return idx


def _block_diag2(w):
    z = jnp.zeros_like(w)
    return jnp.concatenate([jnp.concatenate([w, z], axis=-1), jnp.concatenate([z, w], axis=-1)], axis=-2)


def _token_tile(n):
    for tm in (512, 256, 128):
        if n % tm == 0:
            return tm
    raise ValueError(f"token count {n} is not a multiple of 128")


def kernel(x_prompt, x_sample, cache_kv, state_win, state_pool, state_conv, page_table, rel_bias, ln_ffn1, w_ffn1_gate, w_ffn1_up, w_ffn1_down, ln_mix, w_in, q_norm, k_norm, cmp_pe, cmp_w1, cmp_w2, w_nsa_out, w_pool, pool_scale, conv_w, w_conv_out, w_o, ln_ffn2, w_ffn2_gate, w_ffn2_up, w_ffn2_down):
    b, t, dm = x_prompt.shape
    db, s, _ = x_sample.shape
    depth, n_pool, page = cache_kv.shape[:3]
    n_pages = page_table.shape[1]
    past = n_pages * page
    w_keep = state_win.shape[2]
    sr = SAMPLE_ROWS
    n_p, n_s = b * t, db * sr
    assert t % Q_BLOCK == 0 and t >= WINDOW
    assert s <= sr and db % SAMPLE_NB == 0 and n_p % (SAMPLE_NB * sr) == 0
    assert (past + s) // CMP_STRIDE == past // CMP_STRIDE == LANES and w_keep <= past and s <= w_keep
    assert -(-(past + s) // SEL_BLOCK) <= LANES
    tm = _token_tile(n_p + n_s)
    tm_proj = min(tm, 256)
    tm_mix = _token_tile(t)

    cols = _proj_columns()
    w_proj = (jnp.take(w_in, jnp.asarray(np.clip(cols, 0, None)), axis=2)
              * jnp.asarray(cols >= 0, F32)).astype(BF16)
    head_gain = jnp.concatenate([jnp.tile(q_norm, (1, N_Q_HEADS)), jnp.tile(k_norm[:, 1], (1, N_KV_HEADS)),
                                 jnp.tile(k_norm[:, 2], (1, N_KV_HEADS))], axis=1).reshape(depth, 1, N_NORMED)
    cmp_gain = jnp.tile(k_norm[:, 0], (1, N_KV_HEADS)).reshape(depth, 1, KV_W)
    pe2 = jnp.tile(cmp_pe, (1, 1, 1, N_KV_HEADS))
    w1bd = _block_diag2(cmp_w1).astype(BF16)
    w2bd = _block_diag2(cmp_w2).astype(BF16)
    wa = jnp.take(w_nsa_out, jnp.asarray(_slab_rows()), axis=1).astype(BF16)
    n_grp, gc, ge = w_pool.shape[1:]
    wp = jnp.zeros((depth, n_grp * gc, n_grp * ge), F32)
    for gi in range(n_grp):
        wp = wp.at[:, gi * gc:(gi + 1) * gc, gi * ge:(gi + 1) * ge].set(w_pool[:, gi])
    wp = wp.astype(BF16)
    wc = w_conv_out.astype(BF16)
    wo = w_o.astype(BF16)
    ffn1 = [w.astype(BF16) for w in (w_ffn1_gate, w_ffn1_up, w_ffn1_down)]
    ffn2 = [w.astype(BF16) for w in (w_ffn2_gate, w_ffn2_up, w_ffn2_down)]

    ti = np.arange(Q_BLOCK)
    d0 = _bias_table(rel_bias, ti[:, None] - ti[None, :])
    d1 = _bias_table(rel_bias, Q_BLOCK + ti[:, None] - ti[None, :])
    near_c = np.arange(LANES)
    bc = _bias_table(rel_bias, ti[:, None] - CMP_STRIDE * (near_c[None, :] - CMP_PAD) - (CMP_BLOCK - 1))
    n_cmp = t // CMP_STRIDE - CMP_BLOCK // CMP_STRIDE + 1
    smap = jnp.asarray(_sel_map(n_cmp, _cmp_rows(t), -(-(t // SEL_BLOCK) // LANES) * LANES, CMP_PAD))

    tq = np.arange(sr)[:, None]
    n_chunks = past // CMP_STRIDE
    n_cmp_s = n_chunks - CMP_BLOCK // CMP_STRIDE + 1
    cn = np.arange(n_chunks)[None, :]
    bc_s = _bias_table_sample(rel_bias, past + tq - (CMP_STRIDE * cn + CMP_BLOCK - 1), cn < n_cmp_s)
    dsel = _bias_table_sample(rel_bias, past + tq - np.arange(past)[None, :])
    tn = np.arange(LANES)[None, :]
    dnew = _bias_table_sample(rel_bias, tq - tn, tn < s)
    dist_w = w_keep + tq - np.arange(w_keep)[None, :]
    dwin = _bias_table_sample(rel_bias, dist_w, dist_w < WINDOW)
    smap_s = jnp.asarray(_sel_map(n_cmp_s, n_chunks, LANES, 0)).astype(BF16)
    emat = jnp.asarray(np.arange(past)[:, None] // SEL_BLOCK == np.arange(LANES)[None, :]).astype(BF16)

    cache2d = cache_kv.reshape(depth * n_pool * page, -1)
    cache3 = cache_kv.reshape(depth * n_pool, page, -1)
    swin3 = state_win.reshape(depth * db, w_keep, -1)
    spool3 = state_pool.reshape((depth * db,) + state_pool.shape[2:])
    sconv3 = state_conv.reshape((depth * db,) + state_conv.shape[2:])

    xs = jnp.pad(x_sample, ((0, 0), (0, sr - s), (0, 0)))
    x = jnp.concatenate([x_prompt.reshape(n_p, dm), xs.reshape(n_s, dm)], axis=0)
    outs = [[] for _ in range(8)]
    for l in range(depth):
        x = _ffn(x, ln_ffn1[l], ffn1[0][l], ffn1[1][l], ffn1[2][l], tm)
        q, kv4, win, ksv, u, cb, z, mg, ng = _proj(x, ln_mix[l], w_proj[l], head_gain[l], tm_proj)

        kc, vc = _compress_prompt(kv4, pe2[l], w1bd[l], w2bd[l], cmp_gain[l], b, t)
        attn_p = _attn_prompt(q, ng, kc, vc, ksv, smap, d0, d1, bc, b, t)
        d_p, c_p = _mix_prompt(u, z, cb, conv_w[l], b, t, tm_mix)

        fs3 = _cmp_partial(cache2d, pe2[l], w1bd[l], l, n_pool, page).reshape(n_pool, page // CMP_STRIDE, -1)
        attn_s, d_s, c_s, s_win, s_pool, s_conv = _sample_layer(
            l, page_table, q, ng, kv4, win, u, z, cb, swin3, spool3, sconv3, cache3, fs3, w2bd[l], cmp_gain[l],
            conv_w[l], smap_s, emat, bc_s, dsel, dwin, dnew, n_p, db, s, n_pool)

        x = _merge(x, jnp.concatenate([attn_p, attn_s]), jnp.concatenate([d_p, d_s]), jnp.concatenate([c_p, c_s]),
                   mg, wa[l], wp[l], pool_scale[l], wc[l], wo[l], tm)
        x = _ffn(x, ln_ffn2[l], ffn2[0][l], ffn2[1][l], ffn2[2][l], tm)

        p_keep = min(WINDOW, t)
        outs[0].append(kv4[:n_p].reshape(b, t, 4, N_KV_HEADS, HEAD_DIM))
        outs[1].append(win[:n_p].reshape(b, t, 2, N_KV_HEADS, HEAD_DIM)[:, t - p_keep:])
        outs[2].append(u[:n_p].reshape(b, t, -1)[:, t - POOL_STATE:])
        outs[3].append(z[:n_p].reshape(b, t, -1)[:, t - (CONV_K - 1):])
        outs[4].append(kv4[n_p:].reshape(db, sr, 4, N_KV_HEADS, HEAD_DIM)[:, :s])
        outs[5].append(s_win.reshape(db, w_keep, 2, N_KV_HEADS, HEAD_DIM))
        outs[6].append(s_pool)
        outs[7].append(s_conv)

    y_prompt = x[:n_p].reshape(b, t, dm)
    y_sample = x[n_p:].reshape(db, sr, dm)[:, :s]
    return (y_prompt, y_sample) + tuple(jnp.stack(o) for o in outs)
```

```python
import functools
import math

import numpy as np
import jax
import jax.numpy as jnp
from jax import lax
from jax.experimental import pallas as pl
from jax.experimental.pallas import tpu as pltpu

F32 = jnp.float32
BF16 = jnp.bfloat16
NEG_INF = float("-inf")

HEAD_DIM = 64
N_KV_HEADS = 2
HPG = 4
N_Q_HEADS = N_KV_HEADS * HPG
NSA_W = N_Q_HEADS * HEAD_DIM
KV_W = N_KV_HEADS * HEAD_DIM
CMP_BLOCK = 32
CMP_STRIDE = 16
SEL_BLOCK = 64
TOP_K = 8
WINDOW = 512
Q_BLOCK = 128
N_BUCKETS = 32
MAX_EXACT = 16
MAX_DISTANCE = 128
POOL_WINDOWS = (2, 4, 8, 16)
POOL_STATE = 15
CONV_K = 3
RMS_EPS = 1e-6
LANES = 128
HALO = 16
CMP_PAD = 112
SEL_CHUNK = 512
MASK_BIG = 1e30
VMEM_LIMIT = 56 * 1024 * 1024


def _dot(a, b):
    return jnp.dot(a, b, preferred_element_type=F32)


def _dot_nt(a, b):
    return lax.dot_general(a, b, (((1,), (1,)), ((), ())), preferred_element_type=F32)


def _dot_split(a, b):
    hi = a.astype(BF16)
    lo = (a - hi.astype(F32)).astype(BF16)
    return _dot(hi, b) + _dot(lo, b)


def _group_mean_sq(x):
    r = lax.broadcasted_iota(jnp.int32, (LANES, LANES), 0) // HEAD_DIM
    c = lax.broadcasted_iota(jnp.int32, (LANES, LANES), 1) // HEAD_DIM
    ones_bd = (r == c).astype(BF16)
    return _dot_split(x * x, ones_bd) * (1.0 / HEAD_DIM)


def _head_rms(x, gain):
    parts = []
    for k in range(x.shape[1] // LANES):
        xs = x[:, k * LANES:(k + 1) * LANES]
        parts.append(xs * lax.rsqrt(_group_mean_sq(xs) + RMS_EPS))
    y = parts[0] if len(parts) == 1 else jnp.concatenate(parts, axis=1)
    return y * gain


def _row_rms(x, gain):
    return x * lax.rsqrt(jnp.mean(x * x, axis=-1, keepdims=True) + RMS_EPS) * gain


def _params(*sem):
    return pltpu.CompilerParams(dimension_semantics=sem, vmem_limit_bytes=VMEM_LIMIT)


def _ffn_kernel(x_ref, g_ref, wg_ref, wu_ref, wd_ref, o_ref, h_scr, acc_scr):
    f = pl.program_id(1)

    @pl.when(f == 0)
    def _():
        h_scr[...] = _row_rms(x_ref[...], g_ref[...]).astype(BF16)
        acc_scr[...] = jnp.zeros_like(acc_scr)

    h = h_scr[...]
    a = _dot(h, wg_ref[...])
    b = _dot(h, wu_ref[...])
    act = (a * jax.nn.sigmoid(a) * b).astype(BF16)
    acc_scr[...] += _dot(act, wd_ref[...])

    @pl.when(f == pl.num_programs(1) - 1)
    def _():
        o_ref[...] = x_ref[...] + 0.5 * acc_scr[...]


def _ffn(x, gain, wg, wu, wd, tm):
    n, d = x.shape
    dff = wg.shape[1]
    tf = dff // 2
    return pl.pallas_call(
        _ffn_kernel,
        grid=(n // tm, dff // tf),
        in_specs=[
            pl.BlockSpec((tm, d), lambda i, f: (i, 0)),
            pl.BlockSpec((1, d), lambda i, f: (0, 0)),
            pl.BlockSpec((d, tf), lambda i, f: (0, f)),
            pl.BlockSpec((d, tf), lambda i, f: (0, f)),
            pl.BlockSpec((tf, d), lambda i, f: (f, 0)),
        ],
        out_specs=pl.BlockSpec((tm, d), lambda i, f: (i, 0)),
        out_shape=jax.ShapeDtypeStruct((n, d), F32),
        scratch_shapes=[pltpu.VMEM((tm, d), BF16), pltpu.VMEM((tm, d), F32)],
        compiler_params=_params("parallel", "arbitrary"),
        name="ffn",
    )(x, gain.reshape(1, d), wg, wu, wd)


C_Q = 0
C_KSEL = 512
C_KWIN = 640
C_KCMP = 768
C_VCMP = 896
C_VSEL = 1024
C_VWIN = 1152
C_U = 1280
C_CB = 1536
C_CC = 1792
C_CH = 2048
C_MG = 2304
C_NG = 5376
C_END = 5504
N_NORMED = 768


def _proj_kernel(x_ref, g_ref, w_ref, hg_ref, q_ref, kv_ref, win_ref, ksv_ref, u_ref, cb_ref, z_ref, mg_ref, ng_ref):
    h = _row_rms(x_ref[...], g_ref[...]).astype(BF16)

    def sec(lo, hi):
        return _dot(h, w_ref[:, lo:hi])

    nrm = _head_rms(sec(0, N_NORMED), hg_ref[...])
    ksel = nrm[:, C_KSEL:C_KSEL + KV_W]
    kwin = nrm[:, C_KWIN:C_KWIN + KV_W]
    q_ref[...] = (nrm[:, :NSA_W] * (HEAD_DIM ** -0.5)).astype(BF16)
    rest = sec(C_KCMP, C_U)
    vsel = rest[:, 2 * KV_W:3 * KV_W]
    vwin = rest[:, 3 * KV_W:4 * KV_W]
    kv_ref[:, 0:2 * KV_W] = rest[:, 0:2 * KV_W]
    kv_ref[:, 2 * KV_W:3 * KV_W] = ksel
    kv_ref[:, 3 * KV_W:4 * KV_W] = vsel
    win_ref[:, 0:KV_W] = kwin
    win_ref[:, KV_W:2 * KV_W] = vwin
    ksv_ref[:, 0:KV_W] = ksel.astype(BF16)
    ksv_ref[:, KV_W:2 * KV_W] = vsel.astype(BF16)
    ksv_ref[:, 2 * KV_W:3 * KV_W] = kwin.astype(BF16)
    ksv_ref[:, 3 * KV_W:4 * KV_W] = vwin.astype(BF16)
    u_ref[...] = sec(C_U, C_CB)
    cv = sec(C_CB, C_MG)
    cw = C_CC - C_CB
    cb_ref[...] = cv[:, 0:cw]
    z_ref[...] = cv[:, cw:2 * cw] * cv[:, 2 * cw:3 * cw]
    mg_ref[...] = sec(C_MG, C_NG)
    ng_ref[...] = sec(C_NG, C_END)


def _proj(x, gain, w, head_gain, tm):
    n, d = x.shape
    row = lambda w_: pl.BlockSpec((tm, w_), lambda i: (i, 0))
    full = lambda a: pl.BlockSpec(a.shape, lambda i: (0, 0))
    gain = gain.reshape(1, d)
    widths = ((NSA_W, BF16), (4 * KV_W, F32), (2 * KV_W, F32), (4 * KV_W, BF16), (C_CB - C_U, F32),
              (C_CC - C_CB, F32), (C_CC - C_CB, F32), (C_NG - C_MG, F32), (C_END - C_NG, F32))
    return pl.pallas_call(
        _proj_kernel,
        grid=(n // tm,),
        in_specs=[row(d), full(gain), full(w), full(head_gain)],
        out_specs=[row(w_) for w_, _ in widths],
        out_shape=[jax.ShapeDtypeStruct((n, w_), dt) for w_, dt in widths],
        compiler_params=_params("parallel"),
        name="proj",
    )(x, gain, w, head_gain)


def _compress_kernel(k_ref, v_ref, pe_ref, w1_ref, w2_ref, kg_ref, kc_ref, vc_ref, *, n_ch):
    outs = []
    for s, src_ref in enumerate((k_ref, v_ref)):
        first = jnp.zeros((n_ch, KV_W), F32)
        second = jnp.zeros((n_ch, KV_W), F32)
        for l in range(CMP_STRIDE):
            xl = src_ref[pl.ds(l, n_ch, stride=CMP_STRIDE), :]
            first += _dot((xl + pe_ref[s, l:l + 1, :]).astype(BF16), w1_ref[s, l])
            second += _dot((xl + pe_ref[s, CMP_STRIDE + l:CMP_STRIDE + l + 1, :]).astype(BF16),
                           w1_ref[s, CMP_STRIDE + l])
        hid = first + pltpu.roll(second, n_ch - 1, 0)
        act = hid * jax.nn.sigmoid(hid)
        outs.append(_dot(act.astype(BF16), w2_ref[s]))
    kc = _head_rms(outs[0], kg_ref[...])
    rows = kc_ref.shape[0]
    for ref, val in ((kc_ref, kc), (vc_ref, outs[1])):
        ref[0:CMP_PAD, :] = jnp.zeros((CMP_PAD, KV_W), F32)
        ref[CMP_PAD:CMP_PAD + n_ch, :] = val
        ref[CMP_PAD + n_ch:rows, :] = jnp.zeros((rows - CMP_PAD - n_ch, KV_W), F32)


def _cmp_rows(t):
    return -(-(CMP_PAD + t // CMP_STRIDE) // LANES) * LANES


def _compress_prompt(kv4, pe2, w1bd, w2bd, kgain, b, t):
    n_ch = t // CMP_STRIDE
    rows = _cmp_rows(t)
    full = lambda a: pl.BlockSpec(a.shape, lambda i: (0,) * a.ndim)
    out = jax.ShapeDtypeStruct((b, rows, KV_W), F32)
    return pl.pallas_call(
        functools.partial(_compress_kernel, n_ch=n_ch),
        grid=(b,),
        in_specs=[pl.BlockSpec((t, KV_W), lambda i: (i, 0)), pl.BlockSpec((t, KV_W), lambda i: (i, 1)),
                  full(pe2), full(w1bd), full(w2bd), full(kgain)],
        out_specs=[pl.BlockSpec((None, rows, KV_W), lambda i: (i, 0, 0))] * 2,
        out_shape=[out, out],
        compiler_params=_params("parallel"),
        name="compress_prompt",
    )(kv4, kv4, pe2, w1bd, w2bd, kgain)


def _softmax_first(s, v, m_scr, l_scr, acc_scr):
    m = jnp.max(s, axis=1, keepdims=True)
    e = jnp.exp(s - m)
    m_scr[...] = m
    l_scr[...] = jnp.sum(e, axis=1, keepdims=True)
    acc_scr[...] = _dot(e.astype(BF16), v)


def _softmax_tile(s, v, m_scr, l_scr, acc_scr):
    m_old = m_scr[...]
    m_new = jnp.maximum(m_old, jnp.max(s, axis=1, keepdims=True))
    alpha = jnp.exp(m_old - m_new)
    e = jnp.exp(s - m_new)
    l_scr[...] = alpha * l_scr[...] + jnp.sum(e, axis=1, keepdims=True)
    acc_scr[...] = alpha * acc_scr[...] + _dot(e.astype(BF16), v)
    m_scr[...] = m_new


def _top_k_mask(score, k):
    col = lax.broadcasted_iota(jnp.int32, score.shape, 1)
    sel = jnp.zeros(score.shape, F32)
    work = score
    for _ in range(k):
        m = jnp.max(work, axis=1, keepdims=True)
        first = jnp.min(jnp.where(work == m, col, score.shape[1]), axis=1, keepdims=True)
        pick = col == first
        sel = jnp.maximum(sel, pick.astype(F32))
        work = jnp.where(pick, NEG_INF, work)
    return sel


def _attn_prompt_kernel(q_ref, ng_ref, kc_ref, vc_ref, ksv_ref, kwv_ref, emat_ref, smap_ref, d0_ref, d1_ref, bc_ref,
                        wtab_ref, o_ref, m_scr, l_scr, acc_scr, ob_scr):
    i = pl.program_id(1)
    lane = lax.broadcasted_iota(jnp.int32, (Q_BLOCK, LANES), 1)
    near0 = pl.multiple_of(i * (Q_BLOCK // CMP_STRIDE), 8)
    state = (m_scr, l_scr, acc_scr)

    for g in range(N_KV_HEADS):
        in_group = (lane >= HEAD_DIM) == (g == 1)
        lhs = jnp.concatenate(
            [jnp.where(in_group, q_ref[:, h * LANES:(h + 1) * LANES], jnp.zeros((), BF16)) for h in range(HPG)], axis=0)

        s_far = _dot_nt(lhs, kc_ref[...].astype(BF16))
        pcol = lax.broadcasted_iota(jnp.int32, s_far.shape, 1)
        s_far = jnp.where((pcol >= CMP_PAD) & (pcol < near0), s_far, NEG_INF)
        s_near = _dot_nt(lhs, kc_ref[pl.ds(near0, LANES), :].astype(BF16)) + bc_ref[g]
        ncol = lax.broadcasted_iota(jnp.int32, s_near.shape, 1)
        s_near = jnp.where(near0 + ncol >= CMP_PAD, s_near, NEG_INF)
        m = jnp.maximum(jnp.max(s_far, axis=1, keepdims=True), jnp.max(s_near, axis=1, keepdims=True))
        m = jnp.where(m == NEG_INF, 0.0, m)
        e_far = jnp.exp(s_far - m)
        e_near = jnp.exp(s_near - m)
        den = jnp.sum(e_far, axis=1, keepdims=True) + jnp.sum(e_near, axis=1, keepdims=True)
        inv = 1.0 / jnp.where(den > 0, den, 1.0)
        p_far = e_far * inv
        p_near = e_near * inv
        ob_scr[g, 0] = (_dot(p_far.astype(BF16), vc_ref[...].astype(BF16))
                        + _dot(p_near.astype(BF16), vc_ref[pl.ds(near0, LANES), :].astype(BF16)))

        imp_far = sum(p_far[h * Q_BLOCK:(h + 1) * Q_BLOCK] for h in range(HPG))
        imp_near = sum(p_near[h * Q_BLOCK:(h + 1) * Q_BLOCK] for h in range(HPG))
        score = (_dot_split(imp_far, smap_ref[...].astype(BF16))
                 + _dot_split(imp_near, smap_ref[pl.ds(near0, LANES), :].astype(BF16)))
        jcol = lax.broadcasted_iota(jnp.int32, score.shape, 1)
        cur = 2 * i + (lax.broadcasted_iota(jnp.int32, score.shape, 0) >= SEL_BLOCK).astype(jnp.int32)
        forced = (jcol == 0) | (jcol == cur) | (jcol == cur - 1)
        score = jnp.where(forced, jnp.inf, score)
        score = jnp.where(jcol <= cur, score, NEG_INF)
        sel = _top_k_mask(score, TOP_K)

        not_sel = jnp.concatenate([(sel - 1.0) * MASK_BIG] * HPG, axis=0).astype(BF16)
        lhs_sel = jnp.concatenate([lhs, not_sel], axis=1)

        def sel_logits(base, width, bias_ref):
            keys = jnp.concatenate([ksv_ref[pl.ds(base, width), 0:KV_W], emat_ref[pl.ds(base, width), :]], axis=1)
            s = _dot_nt(lhs_sel, keys)
            return s if bias_ref is None else s + bias_ref[g]

        def sel_values(base, width):
            return ksv_ref[pl.ds(base, width), KV_W:2 * KV_W]

        own = pl.multiple_of(i * Q_BLOCK, Q_BLOCK)
        _softmax_first(sel_logits(own, Q_BLOCK, d0_ref), sel_values(own, Q_BLOCK), *state)

        @pl.when(i >= 1)
        def _():
            prev = pl.multiple_of((i - 1) * Q_BLOCK, Q_BLOCK)
            _softmax_tile(sel_logits(prev, Q_BLOCK, d1_ref), sel_values(prev, Q_BLOCK), *state)

        n_plain = jnp.maximum(i - 1, 0)
        n_wide = n_plain // (SEL_CHUNK // Q_BLOCK)

        def tile_body(j2, carry):
            base = pl.multiple_of(j2 * Q_BLOCK, Q_BLOCK)
            _softmax_tile(sel_logits(base, Q_BLOCK, None), sel_values(base, Q_BLOCK), *state)
            return carry

        def wide_body(c, carry):
            base = pl.multiple_of(c * SEL_CHUNK, SEL_CHUNK)
            _softmax_tile(sel_logits(base, SEL_CHUNK, None), sel_values(base, SEL_CHUNK), *state)
            return carry

        lax.fori_loop(n_wide * (SEL_CHUNK // Q_BLOCK), n_plain, tile_body, 0)
        lax.fori_loop(0, n_wide, wide_body, 0)
        ob_scr[g, 1] = acc_scr[...] / l_scr[...]

        s_w = _dot_nt(lhs, kwv_ref[pl.ds(own, WINDOW + Q_BLOCK), 0:KV_W]) + wtab_ref[g]
        wcol = lax.broadcasted_iota(jnp.int32, s_w.shape, 1)
        s_w = jnp.where(wcol >= WINDOW - own, s_w, NEG_INF)
        e_w = jnp.exp(s_w - jnp.max(s_w, axis=1, keepdims=True))
        ob_scr[g, 2] = (_dot(e_w.astype(BF16), kwv_ref[pl.ds(own, WINDOW + Q_BLOCK), KV_W:2 * KV_W])
                        / jnp.sum(e_w, axis=1, keepdims=True))

    sig = jax.nn.sigmoid(ng_ref[...])
    for h in range(HPG):
        slab = None
        for j in range(3):
            part = []
            for g in range(N_KV_HEADS):
                c = j * N_Q_HEADS + g * HPG + h
                part.append(sig[:, c:c + 1] * ob_scr[g, j, h * Q_BLOCK:(h + 1) * Q_BLOCK, :])
            term = jnp.where(lane < HEAD_DIM, part[0], part[1])
            slab = term if slab is None else slab + term
        o_ref[:, h * LANES:(h + 1) * LANES] = slab.astype(BF16)


def _attn_prompt(q, ng, kc, vc, ksv, kwv, emat, smap, d0, d1, bc, wtab, b, t):
    nq = t // Q_BLOCK
    rows = HPG * Q_BLOCK
    full = lambda a: pl.BlockSpec(a.shape, lambda bi, i: (0,) * a.ndim)
    per_batch = lambda a: pl.BlockSpec((None,) + a.shape[1:], lambda bi, i: (bi, 0, 0))
    return pl.pallas_call(
        _attn_prompt_kernel,
        grid=(b, nq),
        in_specs=[
            pl.BlockSpec((Q_BLOCK, NSA_W), lambda bi, i: (bi * nq + i, 0)),
            pl.BlockSpec((Q_BLOCK, LANES), lambda bi, i: (bi * nq + i, 0)),
            per_batch(kc), per_batch(vc),
            pl.BlockSpec((t, 2 * KV_W), lambda bi, i: (bi, 0)),
            per_batch(kwv),
            full(emat), full(smap), full(d0), full(d1), full(bc), full(wtab),
        ],
        out_specs=pl.BlockSpec((Q_BLOCK, NSA_W), lambda bi, i: (bi * nq + i, 0)),
        out_shape=jax.ShapeDtypeStruct((b * t, NSA_W), BF16),
        scratch_shapes=[pltpu.VMEM((rows, 1), F32), pltpu.VMEM((rows, 1), F32), pltpu.VMEM((rows, LANES), F32),
                        pltpu.VMEM((N_KV_HEADS, 3, rows, LANES), F32)],
        compiler_params=_params("parallel", "arbitrary"),
        name="attn_prompt",
    )(q, ng, kc, vc, ksv, kwv, emat, smap, d0, d1, bc, wtab)


def _pool_rows(ext_scr, base, u, pos):
    m, width = u.shape
    lane = lax.broadcasted_iota(jnp.int32, (m, width), 1)
    gc = width // len(POOL_WINDOWS)
    run = u
    pooled = None
    for k in range(1, max(POOL_WINDOWS)):
        run = run + ext_scr[base - k:base - k + m, :]
        w = k + 1
        if w in POOL_WINDOWS:
            mean = run / jnp.minimum(pos + 1, w).astype(F32)
            pooled = mean if pooled is None else jnp.where(lane >= POOL_WINDOWS.index(w) * gc, mean, pooled)
    return pooled - u


def _conv_rows(ext_scr, base, m, cw_ref):
    y = None
    for j in range(CONV_K):
        off = base - (CONV_K - 1) + j
        term = cw_ref[j:j + 1, :] * ext_scr[off:off + m, :]
        y = term if y is None else y + term
    return y


def _mix_prompt_kernel(u_ref, uh_ref, z_ref, zh_ref, cb_ref, cw_ref, d_ref, c_ref, ext_scr, *, tm):
    i = pl.program_id(1)
    has_hist = (i > 0).astype(F32)
    pos = i * tm + lax.broadcasted_iota(jnp.int32, u_ref.shape, 0)
    ext_scr[0:HALO, :] = uh_ref[...] * has_hist
    ext_scr[HALO:HALO + tm, :] = u_ref[...]
    d_ref[...] = _pool_rows(ext_scr, HALO, u_ref[...], pos).astype(BF16)
    ext_scr[0:HALO, :] = zh_ref[...] * has_hist
    ext_scr[HALO:HALO + tm, :] = z_ref[...]
    c_ref[...] = (cb_ref[...] * _conv_rows(ext_scr, HALO, tm, cw_ref)).astype(BF16)


def _mix_prompt(u, z, cb, conv_w, b, t, tm):
    nt = t // tm
    width = u.shape[1]
    row = pl.BlockSpec((tm, width), lambda bi, i: (bi * nt + i, 0))
    halo = pl.BlockSpec((HALO, width), lambda bi, i: (jnp.maximum((bi * nt + i) * (tm // HALO) - 1, 0), 0))
    out = jax.ShapeDtypeStruct((b * t, width), BF16)
    return pl.pallas_call(
        functools.partial(_mix_prompt_kernel, tm=tm),
        grid=(b, nt),
        in_specs=[row, halo, row, halo, row, pl.BlockSpec(conv_w.shape, lambda bi, i: (0, 0))],
        out_specs=[row, row],
        out_shape=[out, out],
        scratch_shapes=[pltpu.VMEM((HALO + tm, width), F32)],
        compiler_params=_params("parallel", "parallel"),
        name="mix_prompt",
    )(u, u, z, z, cb, conv_w)


def _merge_kernel(x_ref, a_ref, d_ref, c_ref, mg_ref, wa_ref, wp_ref, ps_ref, wc_ref, wo_ref, o_ref):
    dm = x_ref.shape[1]
    gates = jax.nn.sigmoid(mg_ref[...])
    m = gates[:, 0:dm] * _dot(a_ref[...], wa_ref[...])
    m = m + gates[:, dm:2 * dm] * (_dot(d_ref[...], wp_ref[...]) * ps_ref[...])
    m = m + gates[:, 2 * dm:3 * dm] * _dot(c_ref[...], wc_ref[...])
    o_ref[...] = x_ref[...] + _dot(m.astype(BF16), wo_ref[...])


def _merge(x, attn, d, c, mg, wa, wp, ps, wc, wo, tm):
    n, dm = x.shape
    row = lambda a: pl.BlockSpec((tm, a.shape[1]), lambda i: (i, 0))
    full = lambda a: pl.BlockSpec(a.shape, lambda i: (0, 0))
    ps = ps.reshape(1, dm)
    return pl.pallas_call(
        _merge_kernel,
        grid=(n // tm,),
        in_specs=[row(x), row(attn), row(d), row(c), row(mg), full(wa), full(wp), full(ps), full(wc), full(wo)],
        out_specs=row(x),
        out_shape=jax.ShapeDtypeStruct((n, dm), F32),
        compiler_params=_params("parallel"),
        name="merge",
    )(x, attn, d, c, mg, wa, wp, ps, wc, wo)


SAMPLE_ROWS = 8
SAMPLE_NB = 2


def _cmp_partial_kernel(xt_ref, pe_ref, w1_ref, o_ref, x_scr, *, n_ch):
    page = xt_ref.shape[2]
    for p in range(xt_ref.shape[0]):
        x_scr[p * page:(p + 1) * page, :] = xt_ref[p].T
    first = jnp.zeros((n_ch, KV_W), F32)
    second = jnp.zeros((n_ch, KV_W), F32)
    for l in range(CMP_STRIDE):
        xl = x_scr[pl.ds(l, n_ch, stride=CMP_STRIDE), :]
        first += _dot((xl + pe_ref[l:l + 1, :]).astype(BF16), w1_ref[l])
        second += _dot((xl + pe_ref[CMP_STRIDE + l:CMP_STRIDE + l + 1, :]).astype(BF16), w1_ref[CMP_STRIDE + l])
    o_ref[:, 0:KV_W] = first
    o_ref[:, KV_W:2 * KV_W] = second


def _cmp_partial(cache_t, pe2, w1bd, layer, n_pool):
    page = cache_t.shape[2]
    ch = page // CMP_STRIDE
    pp = next(p for p in (64, 32, 16, 8, 4, 2, 1) if n_pool % p == 0)
    nb = n_pool // pp
    return pl.pallas_call(
        functools.partial(_cmp_partial_kernel, n_ch=pp * ch),
        grid=(nb, 2),
        in_specs=[pl.BlockSpec((pp, KV_W, page), lambda i, s: (layer * nb + i, s, 0)),
                  pl.BlockSpec((None,) + pe2.shape[1:], lambda i, s: (s, 0, 0)),
                  pl.BlockSpec((None,) + w1bd.shape[1:], lambda i, s: (s, 0, 0, 0))],
        out_specs=pl.BlockSpec((pp * ch, 2 * KV_W), lambda i, s: (i, s)),
        out_shape=jax.ShapeDtypeStruct((n_pool * ch, 4 * KV_W), F32),
        scratch_shapes=[pltpu.VMEM((pp * page, KV_W), F32)],
        compiler_params=_params("parallel", "arbitrary"),
        name="cmp_partial",
    )(cache_t, pe2, w1bd)


def _softmax_pair(s_a, vt_a, s_b, v_b):
    m = jnp.maximum(jnp.max(s_a, axis=1, keepdims=True), jnp.max(s_b, axis=1, keepdims=True))
    e_a = jnp.exp(s_a - m)
    e_b = jnp.exp(s_b - m)
    den = jnp.sum(e_a, axis=1, keepdims=True) + jnp.sum(e_b, axis=1, keepdims=True)
    return (_dot_nt(e_a.astype(BF16), vt_a) + _dot(e_b.astype(BF16), v_b)) / den


def _sample_kernel(pt_ref, q_ref, ng_ref, kv_ref, win_ref, wnt_ref, u_ref, z_ref, cb_ref, swin_ref, spool_ref,
                   sconv_ref, cache_ref, fs_ref, w2_ref, kg_ref, cw_ref, smap_ref, emat_ref, bc_ref, dsel_ref, dwin_ref,
                   dnew_ref, attn_ref, d_ref, c_ref, owin_ref, opool_ref, oconv_ref,
                   kvbuf, fsbuf, sems, ext_scr, *, layer_base, n_pages, page, s_len, past):
    i = pl.program_id(0)
    slot = i % 2
    ch = page // CMP_STRIDE
    sr = SAMPLE_ROWS

    def copies(step, slot_):
        out = []
        for sb in range(SAMPLE_NB):
            for pg in range(n_pages):
                pid = pt_ref[step * SAMPLE_NB + sb, pg]
                out.append(pltpu.make_async_copy(
                    cache_ref.at[layer_base + pid, pl.ds(2 * KV_W, 2 * KV_W), :],
                    kvbuf.at[slot_, sb, :, pl.ds(pg * page, page)], sems.at[slot_, 0]))
                out.append(pltpu.make_async_copy(
                    fs_ref.at[pid], fsbuf.at[slot_, sb, pl.ds(pg * ch, ch), :], sems.at[slot_, 1]))
        return out

    @pl.when(i == 0)
    def _():
        for c in copies(0, 0):
            c.start()

    @pl.when(i + 1 < pl.num_programs(0))
    def _():
        for c in copies(i + 1, 1 - slot):
            c.start()

    for c in copies(i, slot):
        c.wait()

    lane = lax.broadcasted_iota(jnp.int32, (sr, LANES), 1)
    grp_rows = N_KV_HEADS * sr
    q_all = q_ref[...].astype(F32)
    sig_all = jax.nn.sigmoid(ng_ref[...])
    kvn = kv_ref[...]
    winn = win_ref[...]
    pad_rows = jnp.zeros((LANES - sr, KV_W), F32)
    w_keep = swin_ref.shape[2]
    wlane = lax.broadcasted_iota(jnp.int32, swin_ref.shape[1:], 1)
    hist = spool_ref.shape[1]
    ck = sconv_ref.shape[1]
    attn_rows, d_rows, c_rows = [], [], []

    for sb in range(SAMPLE_NB):
        r0 = sb * sr
        q8 = q_all[r0:r0 + sr]
        pieces = []
        for h in range(HPG):
            for g in range(N_KV_HEADS):
                pieces.append(jnp.where((lane >= HEAD_DIM) == (g == 1), q8[:, h * LANES:(h + 1) * LANES], 0.0))
        lhs = jnp.concatenate(pieces, axis=0).astype(BF16)

        def new_rows(x8):
            return jnp.concatenate([x8, pad_rows], axis=0).astype(BF16)

        ks_new = new_rows(kvn[r0:r0 + sr, 2 * KV_W:3 * KV_W])
        vs_new = new_rows(kvn[r0:r0 + sr, 3 * KV_W:4 * KV_W])
        kw_new = new_rows(winn[r0:r0 + sr, 0:KV_W])
        vw_new = new_rows(winn[r0:r0 + sr, KV_W:2 * KV_W])

        fs = fsbuf[slot, sb]
        n_ch = fs.shape[0]
        cmp = []
        for s_ in range(2):
            hid = (fs[:, 2 * s_ * KV_W:(2 * s_ + 1) * KV_W]
                   + pltpu.roll(fs[:, (2 * s_ + 1) * KV_W:(2 * s_ + 2) * KV_W], n_ch - 1, 0))
            cmp.append(_dot((hid * jax.nn.sigmoid(hid)).astype(BF16), w2_ref[s_]))
        kc = _head_rms(cmp[0], kg_ref[...]).astype(BF16)
        vc = cmp[1].astype(BF16)
        s_c = _dot_nt(lhs, kc) + bc_ref[...]
        e_c = jnp.exp(s_c - jnp.max(s_c, axis=1, keepdims=True))
        p_c = e_c / jnp.sum(e_c, axis=1, keepdims=True)
        o_c = _dot(p_c.astype(BF16), vc)

        imp = sum(p_c[h * grp_rows:(h + 1) * grp_rows] for h in range(HPG))
        score = _dot_split(imp, smap_ref[...])
        jcol = lax.broadcasted_iota(jnp.int32, score.shape, 1)
        cur = (past + lax.broadcasted_iota(jnp.int32, score.shape, 0) % sr) // SEL_BLOCK
        forced = (jcol == 0) | (jcol == cur) | (jcol == cur - 1)
        score = jnp.where(forced, jnp.inf, score)
        score = jnp.where(jcol <= cur, score, NEG_INF)
        sel = _top_k_mask(score, TOP_K)
        not_sel = jnp.concatenate([(sel - 1.0) * MASK_BIG] * HPG, axis=0).astype(BF16)
        lhs_sel = jnp.concatenate([lhs, not_sel], axis=1)

        k_past = jnp.concatenate([kvbuf[slot, sb, 0:KV_W, :].astype(BF16), emat_ref[...]], axis=0)
        o_s = _softmax_pair(_dot(lhs_sel, k_past) + dsel_ref[...], kvbuf[slot, sb, KV_W:2 * KV_W, :].astype(BF16),
                            _dot_nt(lhs, ks_new) + dnew_ref[...], vs_new)
        o_w = _softmax_pair(_dot(lhs, swin_ref[sb, 0:KV_W, :].astype(BF16)) + dwin_ref[...],
                            swin_ref[sb, KV_W:2 * KV_W, :].astype(BF16),
                            _dot_nt(lhs, kw_new) + dnew_ref[...], vw_new)

        branch = (o_c, o_s, o_w)
        sig8 = sig_all[r0:r0 + sr]
        slabs = []
        for h in range(HPG):
            slab = None
            for j in range(3):
                part = []
                for g in range(N_KV_HEADS):
                    c = j * N_Q_HEADS + g * HPG + h
                    lo = h * grp_rows + g * sr
                    part.append(sig8[:, c:c + 1] * branch[j][lo:lo + sr])
                term = jnp.where(lane < HEAD_DIM, part[0], part[1])
                slab = term if slab is None else slab + term
            slabs.append(slab)
        attn_rows.append(jnp.concatenate(slabs, axis=1))

        state = pltpu.roll(swin_ref[sb], w_keep - s_len, 1)
        for tt in range(s_len):
            state = jnp.where(wlane == w_keep - s_len + tt, wnt_ref[sb, :, tt:tt + 1], state)
        owin_ref[sb] = state

        u8 = u_ref[r0:r0 + sr, :]
        ext_scr[0:hist, :] = spool_ref[sb]
        ext_scr[hist:hist + sr, :] = u8
        d_rows.append(_pool_rows(ext_scr, hist, u8, hist + lax.broadcasted_iota(jnp.int32, u8.shape, 0)))
        opool_ref[sb] = ext_scr[s_len:s_len + hist, :]
        ext_scr[0:ck, :] = sconv_ref[sb]
        ext_scr[ck:ck + sr, :] = z_ref[r0:r0 + sr, :]
        c_rows.append(cb_ref[r0:r0 + sr, :] * _conv_rows(ext_scr, ck, sr, cw_ref))
        oconv_ref[sb] = ext_scr[s_len:s_len + ck, :]

    attn_ref[...] = jnp.concatenate(attn_rows, axis=0).astype(BF16)
    d_ref[...] = jnp.concatenate(d_rows, axis=0).astype(BF16)
    c_ref[...] = jnp.concatenate(c_rows, axis=0).astype(BF16)


def _sample_layer(layer, page_table, q, ng, kv4, win, wnt, u, z, cb, swin3, spool3, sconv3, cache3, fs3, w2bd, kgain,
                  conv_w, smap_s, emat, bc_s, dsel, dwin, dnew, n_p, db, s_len, n_pool):
    nsteps = db // SAMPLE_NB
    rb = SAMPLE_NB * SAMPLE_ROWS
    r0 = n_p // rb
    page = cache3.shape[2]
    n_pages = page_table.shape[1]
    past = n_pages * page
    tok = lambda a: pl.BlockSpec((rb, a.shape[1]), lambda i, pt: (r0 + i, 0))
    new_t = pl.BlockSpec((SAMPLE_NB,) + wnt.shape[1:], lambda i, pt: (i, 0, 0))
    state = lambda a: pl.BlockSpec((SAMPLE_NB,) + a.shape[1:], lambda i, pt: (layer * nsteps + i, 0, 0))
    full = lambda a: pl.BlockSpec(a.shape, lambda i, pt: (0,) * a.ndim)
    hbm = pl.BlockSpec(memory_space=pl.ANY)
    out_tok = lambda w_: pl.BlockSpec((rb, w_), lambda i, pt: (i, 0))
    out_state = lambda a: pl.BlockSpec((SAMPLE_NB,) + a.shape[1:], lambda i, pt: (i, 0, 0))
    width = u.shape[1]
    n_tok = db * SAMPLE_ROWS
    return pl.pallas_call(
        functools.partial(_sample_kernel, layer_base=layer * n_pool, n_pages=n_pages, page=page, s_len=s_len, past=past),
        grid_spec=pltpu.PrefetchScalarGridSpec(
            num_scalar_prefetch=1,
            grid=(nsteps,),
            in_specs=[tok(q), tok(ng), tok(kv4), tok(win), new_t, tok(u), tok(z), tok(cb), state(swin3), state(spool3),
                      state(sconv3), hbm, hbm, full(w2bd), full(kgain), full(conv_w), full(smap_s), full(emat),
                      full(bc_s), full(dsel), full(dwin), full(dnew)],
            out_specs=[out_tok(NSA_W), out_tok(width), out_tok(width), out_state(swin3), out_state(spool3),
                       out_state(sconv3)],
            scratch_shapes=[pltpu.VMEM((2, SAMPLE_NB, 2 * KV_W, past), F32),
                            pltpu.VMEM((2, SAMPLE_NB, past // CMP_STRIDE, 4 * KV_W), F32),
                            pltpu.SemaphoreType.DMA((2, 2)),
                            pltpu.VMEM((4 * SAMPLE_ROWS, width), F32)]),
        out_shape=[jax.ShapeDtypeStruct((n_tok, NSA_W), BF16), jax.ShapeDtypeStruct((n_tok, width), BF16),
                   jax.ShapeDtypeStruct((n_tok, width), BF16), jax.ShapeDtypeStruct((db,) + swin3.shape[1:], F32),
                   jax.ShapeDtypeStruct((db,) + spool3.shape[1:], F32),
                   jax.ShapeDtypeStruct((db,) + sconv3.shape[1:], F32)],
        compiler_params=_params("arbitrary"),
        name="sample_mix",
    )(page_table, q, ng, kv4, win, wnt, u, z, cb, swin3, spool3, sconv3, cache3, fs3, w2bd, kgain, conv_w, smap_s,
      emat, bc_s, dsel, dwin, dnew)


def _bucket_table(n):
    d = np.arange(n)
    nf = np.maximum(d, 1).astype(np.float32)
    large = MAX_EXACT + (np.log(nf / np.float32(MAX_EXACT)) / np.float32(math.log(MAX_DISTANCE / MAX_EXACT))
                         * np.float32(N_BUCKETS - MAX_EXACT)).astype(np.int32)
    return np.where(d < MAX_EXACT, d, np.minimum(large, N_BUCKETS - 1)).astype(np.int32)


def _bias_values(rel_bias, dist, visible=None):
    visible = (dist >= 0) if visible is None else (visible & (dist >= 0))
    bkt = _bucket_table(max(int(dist.max()) + 1, MAX_DISTANCE))[np.clip(dist, 0, None)]
    tab = rel_bias - rel_bias[N_BUCKETS - 1:N_BUCKETS]
    vals = jnp.where(jnp.asarray(visible)[..., None], tab[jnp.asarray(bkt)], NEG_INF)
    return vals.transpose(2, 0, 1).reshape((N_KV_HEADS, HPG) + dist.shape)


def _bias_table(rel_bias, dist):
    r, c = dist.shape
    return _bias_values(rel_bias, dist).reshape(N_KV_HEADS, HPG * r, c)


def _bias_table_sample(rel_bias, dist, visible=None):
    r, c = dist.shape
    return _bias_values(rel_bias, dist, visible).transpose(1, 0, 2, 3).reshape(HPG * N_KV_HEADS * r, c)


def _sel_map(n_cmp, n_rows, n_cols, pad):
    r_sel = SEL_BLOCK // CMP_STRIDE
    r_cmp = CMP_BLOCK // CMP_STRIDE
    out = np.zeros((n_rows, n_cols), np.float32)
    for j in range(n_cols):
        for m in range(r_sel):
            for n in range(r_cmp):
                c = r_sel * j + m - n
                if 0 <= c < n_cmp:
                    out[pad + c, j] += 1.0
    return out


def _proj_columns():
    q0, kv0, ng0 = 0, NSA_W, NSA_W + 6 * KV_W
    u0 = ng0 + 3 * N_Q_HEADS
    idx = np.full((C_END,), -1, np.int64)
    for k in range(HPG):
        for g in range(N_KV_HEADS):
            dst = C_Q + k * LANES + g * HEAD_DIM
            idx[dst:dst + HEAD_DIM] = q0 + (g * HPG + k) * HEAD_DIM + np.arange(HEAD_DIM)
    for dst, slot in ((C_KCMP, 0), (C_VCMP, 1), (C_KSEL, 2), (C_VSEL, 3), (C_KWIN, 4), (C_VWIN, 5)):
        idx[dst:dst + KV_W] = kv0 + slot * KV_W + np.arange(KV_W)
    idx[C_U:C_NG] = u0 + np.arange(C_NG - C_U)
    for j in range(3):
        for g in range(N_KV_HEADS):
            for h in range(HPG):
                idx[C_NG + j * N_Q_HEADS + g * HPG + h] = ng0 + (g * HPG + h) * 3 + j
    return idx


def _slab_rows():
    idx = np.zeros((NSA_W,), np.int64)
    for k in range(HPG):
        for g in range(N_KV_HEADS):
            dst = k * LANES + g * HEAD_DIM
            idx[dst:dst + HEAD_DIM] = (g * HPG + k) * HEAD_DIM + np.arange(HEAD_DIM)
    return idx


def _block_diag2(w):
    z = jnp.zeros_like(w)
    return jnp.concatenate([jnp.concatenate([w, z], axis=-1), jnp.concatenate([z, w], axis=-1)], axis=-2)


def _token_tile(n):
    for tm in (512, 256, 128):
        if n % tm == 0:
            return tm
    raise ValueError(f"token count {n} is not a multiple of 128")


def kernel(x_prompt, x_sample, cache_kv, state_win, state_pool, state_conv, page_table, rel_bias, ln_ffn1, w_ffn1_gate, w_ffn1_up, w_ffn1_down, ln_mix, w_in, q_norm, k_norm, cmp_pe, cmp_w1, cmp_w2, w_nsa_out, w_pool, pool_scale, conv_w, w_conv_out, w_o, ln_ffn2, w_ffn2_gate, w_ffn2_up, w_ffn2_down):
    b, t, dm = x_prompt.shape
    db, s, _ = x_sample.shape
    depth, n_pool, page = cache_kv.shape[:3]
    n_pages = page_table.shape[1]
    past = n_pages * page
    w_keep = state_win.shape[2]
    sr = SAMPLE_ROWS
    n_p, n_s = b * t, db * sr
    assert t % Q_BLOCK == 0 and t >= WINDOW and t // SEL_BLOCK <= LANES
    assert s <= sr and db % SAMPLE_NB == 0 and n_p % (SAMPLE_NB * sr) == 0
    assert (past + s) // CMP_STRIDE == past // CMP_STRIDE == LANES and w_keep <= past and s <= w_keep
    assert -(-(past + s) // SEL_BLOCK) <= LANES
    tm = _token_tile(n_p + n_s)
    tm_proj = min(tm, 256)
    tm_mix = _token_tile(t)

    cols = _proj_columns()
    w_proj = (jnp.take(w_in, jnp.asarray(np.clip(cols, 0, None)), axis=2)
              * jnp.asarray(cols >= 0, F32)).astype(BF16)
    head_gain = jnp.concatenate([jnp.tile(q_norm, (1, N_Q_HEADS)), jnp.tile(k_norm[:, 1], (1, N_KV_HEADS)),
                                 jnp.tile(k_norm[:, 2], (1, N_KV_HEADS))], axis=1).reshape(depth, 1, N_NORMED)
    cmp_gain = jnp.tile(k_norm[:, 0], (1, N_KV_HEADS)).reshape(depth, 1, KV_W)
    pe2 = jnp.tile(cmp_pe, (1, 1, 1, N_KV_HEADS))
    w1bd = _block_diag2(cmp_w1).astype(BF16)
    w2bd = _block_diag2(cmp_w2).astype(BF16)
    wa = jnp.take(w_nsa_out, jnp.asarray(_slab_rows()), axis=1).astype(BF16)
    n_grp, gc, ge = w_pool.shape[1:]
    wp = jnp.zeros((depth, n_grp * gc, n_grp * ge), F32)
    for gi in range(n_grp):
        wp = wp.at[:, gi * gc:(gi + 1) * gc, gi * ge:(gi + 1) * ge].set(w_pool[:, gi])
    wp = wp.astype(BF16)
    wc = w_conv_out.astype(BF16)
    wo = w_o.astype(BF16)
    ffn1 = [w.astype(BF16) for w in (w_ffn1_gate, w_ffn1_up, w_ffn1_down)]
    ffn2 = [w.astype(BF16) for w in (w_ffn2_gate, w_ffn2_up, w_ffn2_down)]

    ti = np.arange(Q_BLOCK)
    d0 = _bias_table(rel_bias, ti[:, None] - ti[None, :])
    d1 = _bias_table(rel_bias, Q_BLOCK + ti[:, None] - ti[None, :])
    near_c = np.arange(LANES)
    bc = _bias_table(rel_bias, ti[:, None] - CMP_STRIDE * (near_c[None, :] - CMP_PAD) - (CMP_BLOCK - 1))
    n_cmp = t // CMP_STRIDE - CMP_BLOCK // CMP_STRIDE + 1
    smap = jnp.asarray(_sel_map(n_cmp, _cmp_rows(t), LANES, CMP_PAD))
    wj = np.arange(WINDOW + Q_BLOCK)[None, :]
    dist_pw = WINDOW + ti[:, None] - wj
    wtab = _bias_values(rel_bias, dist_pw, dist_pw < WINDOW).reshape(N_KV_HEADS, HPG * Q_BLOCK, WINDOW + Q_BLOCK)
    emat_p = jnp.asarray(np.arange(t)[:, None] // SEL_BLOCK == np.arange(LANES)[None, :]).astype(BF16)

    tq = np.arange(sr)[:, None]
    n_chunks = past // CMP_STRIDE
    n_cmp_s = n_chunks - CMP_BLOCK // CMP_STRIDE + 1
    cn = np.arange(n_chunks)[None, :]
    bc_s = _bias_table_sample(rel_bias, past + tq - (CMP_STRIDE * cn + CMP_BLOCK - 1), cn < n_cmp_s)
    dsel = _bias_table_sample(rel_bias, past + tq - np.arange(past)[None, :])
    tn = np.arange(LANES)[None, :]
    dnew = _bias_table_sample(rel_bias, tq - tn, tn < s)
    dist_w = w_keep + tq - np.arange(w_keep)[None, :]
    dwin = _bias_table_sample(rel_bias, dist_w, dist_w < WINDOW)
    smap_s = jnp.asarray(_sel_map(n_cmp_s, n_chunks, LANES, 0)).astype(BF16)
    emat = jnp.asarray(np.arange(LANES)[:, None] == np.arange(past)[None, :] // SEL_BLOCK).astype(BF16)

    cache3 = cache_kv.transpose(0, 1, 3, 4, 5, 2).reshape(depth * n_pool, -1, page)
    swin3 = state_win.transpose(0, 1, 3, 4, 5, 2).reshape(depth * db, -1, w_keep)
    spool3 = state_pool.reshape((depth * db,) + state_pool.shape[2:])
    sconv3 = state_conv.reshape((depth * db,) + state_conv.shape[2:])

    xs = jnp.pad(x_sample, ((0, 0), (0, sr - s), (0, 0)))
    x = jnp.concatenate([x_prompt.reshape(n_p, dm), xs.reshape(n_s, dm)], axis=0)
    outs = [[] for _ in range(8)]
    for l in range(depth):
        x = _ffn(x, ln_ffn1[l], ffn1[0][l], ffn1[1][l], ffn1[2][l], tm)
        q, kv4, win, ksv, u, cb, z, mg, ng = _proj(x, ln_mix[l], w_proj[l], head_gain[l], tm_proj)

        kc, vc = _compress_prompt(kv4, pe2[l], w1bd[l], w2bd[l], cmp_gain[l], b, t)
        kwv = jnp.pad(ksv[:n_p, 2 * KV_W:].reshape(b, t, 2 * KV_W), ((0, 0), (WINDOW, 0), (0, 0)))
        attn_p = _attn_prompt(q, ng, kc, vc, ksv, kwv, emat_p, smap, d0, d1, bc, wtab, b, t)
        d_p, c_p = _mix_prompt(u, z, cb, conv_w[l], b, t, tm_mix)

        fs3 = _cmp_partial(cache3, pe2[l], w1bd[l], l, n_pool).reshape(n_pool, page // CMP_STRIDE, -1)
        wnt = win[n_p:].reshape(db, sr, -1).transpose(0, 2, 1)
        attn_s, d_s, c_s, s_win, s_pool, s_conv = _sample_layer(
            l, page_table, q, ng, kv4, win, wnt, u, z, cb, swin3, spool3, sconv3, cache3, fs3, w2bd[l], cmp_gain[l],
            conv_w[l], smap_s, emat, bc_s, dsel, dwin, dnew, n_p, db, s, n_pool)

        x = _merge(x, jnp.concatenate([attn_p, attn_s]), jnp.concatenate([d_p, d_s]), jnp.concatenate([c_p, c_s]),
                   mg, wa[l], wp[l], pool_scale[l], wc[l], wo[l], tm)
        x = _ffn(x, ln_ffn2[l], ffn2[0][l], ffn2[1][l], ffn2[2][l], tm)

        p_keep = min(WINDOW, t)
        outs[0].append(kv4[:n_p].reshape(b, t, 4, N_KV_HEADS, HEAD_DIM))
        outs[1].append(win[:n_p].reshape(b, t, 2, N_KV_HEADS, HEAD_DIM)[:, t - p_keep:])
        outs[2].append(u[:n_p].reshape(b, t, -1)[:, t - POOL_STATE:])
        outs[3].append(z[:n_p].reshape(b, t, -1)[:, t - (CONV_K - 1):])
        outs[4].append(kv4[n_p:].reshape(db, sr, 4, N_KV_HEADS, HEAD_DIM)[:, :s])
        outs[5].append(s_win.reshape(db, 2, N_KV_HEADS, HEAD_DIM, w_keep).transpose(0, 4, 1, 2, 3))
        outs[6].append(s_pool)
        outs[7].append(s_conv)

    y_prompt = x[:n_p].reshape(b, t, dm)
    y_sample = x[n_p:].reshape(db, sr, dm)[:, :s]
    return (y_prompt, y_sample) + tuple(jnp.stack(o) for o in outs)
```

```python
import functools
import math

import numpy as np
import jax
import jax.numpy as jnp
from jax import lax
from jax.experimental import pallas as pl
from jax.experimental.pallas import tpu as pltpu

F32 = jnp.float32
BF16 = jnp.bfloat16
NEG_INF = float("-inf")

HEAD_DIM = 64
N_KV_HEADS = 2
HPG = 4
N_Q_HEADS = N_KV_HEADS * HPG
NSA_W = N_Q_HEADS * HEAD_DIM
KV_W = N_KV_HEADS * HEAD_DIM
CMP_BLOCK = 32
CMP_STRIDE = 16
SEL_BLOCK = 64
TOP_K = 8
WINDOW = 512
Q_BLOCK = 128
N_BUCKETS = 32
MAX_EXACT = 16
MAX_DISTANCE = 128
POOL_WINDOWS = (2, 4, 8, 16)
POOL_STATE = 15
CONV_K = 3
RMS_EPS = 1e-6
LANES = 128
HALO = 16
CMP_PAD = 112
SEL_CHUNK = 512
MASK_BIG = 1e30
VMEM_LIMIT = 56 * 1024 * 1024


def _dot(a, b):
    return jnp.dot(a, b, preferred_element_type=F32)


def _dot_nt(a, b):
    return lax.dot_general(a, b, (((1,), (1,)), ((), ())), preferred_element_type=F32)


def _dot_split(a, b):
    hi = a.astype(BF16)
    lo = (a - hi.astype(F32)).astype(BF16)
    return _dot(hi, b) + _dot(lo, b)


def _group_mean_sq(x):
    r = lax.broadcasted_iota(jnp.int32, (LANES, LANES), 0) // HEAD_DIM
    c = lax.broadcasted_iota(jnp.int32, (LANES, LANES), 1) // HEAD_DIM
    ones_bd = (r == c).astype(BF16)
    return _dot_split(x * x, ones_bd) * (1.0 / HEAD_DIM)


def _head_rms(x, gain):
    parts = []
    for k in range(x.shape[1] // LANES):
        xs = x[:, k * LANES:(k + 1) * LANES]
        parts.append(xs * lax.rsqrt(_group_mean_sq(xs) + RMS_EPS))
    y = parts[0] if len(parts) == 1 else jnp.concatenate(parts, axis=1)
    return y * gain


def _row_rms(x, gain):
    return x * lax.rsqrt(jnp.mean(x * x, axis=-1, keepdims=True) + RMS_EPS) * gain


def _params(*sem):
    return pltpu.CompilerParams(dimension_semantics=sem, vmem_limit_bytes=VMEM_LIMIT)


def _ffn_kernel(x_ref, g_ref, wg_ref, wu_ref, wd_ref, o_ref, h_scr, acc_scr):
    f = pl.program_id(1)

    @pl.when(f == 0)
    def _():
        h_scr[...] = _row_rms(x_ref[...], g_ref[...]).astype(BF16)
        acc_scr[...] = jnp.zeros_like(acc_scr)

    h = h_scr[...]
    a = _dot(h, wg_ref[...])
    b = _dot(h, wu_ref[...])
    act = (a * jax.nn.sigmoid(a) * b).astype(BF16)
    acc_scr[...] += _dot(act, wd_ref[...])

    @pl.when(f == pl.num_programs(1) - 1)
    def _():
        o_ref[...] = x_ref[...] + 0.5 * acc_scr[...]


def _ffn(x, gain, wg, wu, wd, tm):
    n, d = x.shape
    dff = wg.shape[1]
    tf = dff // 2
    return pl.pallas_call(
        _ffn_kernel,
        grid=(n // tm, dff // tf),
        in_specs=[
            pl.BlockSpec((tm, d), lambda i, f: (i, 0)),
            pl.BlockSpec((1, d), lambda i, f: (0, 0)),
            pl.BlockSpec((d, tf), lambda i, f: (0, f)),
            pl.BlockSpec((d, tf), lambda i, f: (0, f)),
            pl.BlockSpec((tf, d), lambda i, f: (f, 0)),
        ],
        out_specs=pl.BlockSpec((tm, d), lambda i, f: (i, 0)),
        out_shape=jax.ShapeDtypeStruct((n, d), F32),
        scratch_shapes=[pltpu.VMEM((tm, d), BF16), pltpu.VMEM((tm, d), F32)],
        compiler_params=_params("parallel", "arbitrary"),
        name="ffn",
    )(x, gain.reshape(1, d), wg, wu, wd)


C_Q = 0
C_KSEL = 512
C_KWIN = 640
C_KCMP = 768
C_VCMP = 896
C_VSEL = 1024
C_VWIN = 1152
C_U = 1280
C_CB = 1536
C_CC = 1792
C_CH = 2048
C_MG = 2304
C_NG = 5376
C_END = 5504
N_NORMED = 768


def _proj_kernel(x_ref, g_ref, w_ref, hg_ref, q_ref, kv_ref, win_ref, ksv_ref, u_ref, cb_ref, z_ref, mg_ref, ng_ref):
    h = _row_rms(x_ref[...], g_ref[...]).astype(BF16)

    def sec(lo, hi):
        return _dot(h, w_ref[:, lo:hi])

    nrm = _head_rms(sec(0, N_NORMED), hg_ref[...])
    ksel = nrm[:, C_KSEL:C_KSEL + KV_W]
    kwin = nrm[:, C_KWIN:C_KWIN + KV_W]
    q_ref[...] = (nrm[:, :NSA_W] * (HEAD_DIM ** -0.5)).astype(BF16)
    rest = sec(C_KCMP, C_U)
    vsel = rest[:, 2 * KV_W:3 * KV_W]
    vwin = rest[:, 3 * KV_W:4 * KV_W]
    kv_ref[:, 0:2 * KV_W] = rest[:, 0:2 * KV_W]
    kv_ref[:, 2 * KV_W:3 * KV_W] = ksel
    kv_ref[:, 3 * KV_W:4 * KV_W] = vsel
    win_ref[:, 0:KV_W] = kwin
    win_ref[:, KV_W:2 * KV_W] = vwin
    ksv_ref[:, 0:KV_W] = ksel.astype(BF16)
    ksv_ref[:, KV_W:2 * KV_W] = vsel.astype(BF16)
    ksv_ref[:, 2 * KV_W:3 * KV_W] = kwin.astype(BF16)
    ksv_ref[:, 3 * KV_W:4 * KV_W] = vwin.astype(BF16)
    u_ref[...] = sec(C_U, C_CB)
    cv = sec(C_CB, C_MG)
    cw = C_CC - C_CB
    cb_ref[...] = cv[:, 0:cw]
    z_ref[...] = cv[:, cw:2 * cw] * cv[:, 2 * cw:3 * cw]
    mg_ref[...] = sec(C_MG, C_NG)
    ng_ref[...] = sec(C_NG, C_END)


def _proj(x, gain, w, head_gain, tm):
    n, d = x.shape
    row = lambda w_: pl.BlockSpec((tm, w_), lambda i: (i, 0))
    full = lambda a: pl.BlockSpec(a.shape, lambda i: (0, 0))
    gain = gain.reshape(1, d)
    widths = ((NSA_W, BF16), (4 * KV_W, F32), (2 * KV_W, F32), (4 * KV_W, BF16), (C_CB - C_U, F32),
              (C_CC - C_CB, F32), (C_CC - C_CB, F32), (C_NG - C_MG, F32), (C_END - C_NG, F32))
    return pl.pallas_call(
        _proj_kernel,
        grid=(n // tm,),
        in_specs=[row(d), full(gain), full(w), full(head_gain)],
        out_specs=[row(w_) for w_, _ in widths],
        out_shape=[jax.ShapeDtypeStruct((n, w_), dt) for w_, dt in widths],
        compiler_params=_params("parallel"),
        name="proj",
    )(x, gain, w, head_gain)


def _compress_kernel(k_ref, v_ref, pe_ref, w1_ref, w2_ref, kg_ref, kc_ref, vc_ref, *, n_ch):
    outs = []
    for s, src_ref in enumerate((k_ref, v_ref)):
        first = jnp.zeros((n_ch, KV_W), F32)
        second = jnp.zeros((n_ch, KV_W), F32)
        for l in range(CMP_STRIDE):
            xl = src_ref[pl.ds(l, n_ch, stride=CMP_STRIDE), :]
            first += _dot((xl + pe_ref[s, l:l + 1, :]).astype(BF16), w1_ref[s, l])
            second += _dot((xl + pe_ref[s, CMP_STRIDE + l:CMP_STRIDE + l + 1, :]).astype(BF16),
                           w1_ref[s, CMP_STRIDE + l])
        hid = first + pltpu.roll(second, n_ch - 1, 0)
        act = hid * jax.nn.sigmoid(hid)
        outs.append(_dot(act.astype(BF16), w2_ref[s]))
    kc = _head_rms(outs[0], kg_ref[...])
    rows = kc_ref.shape[0]
    for ref, val in ((kc_ref, kc), (vc_ref, outs[1])):
        ref[0:CMP_PAD, :] = jnp.zeros((CMP_PAD, KV_W), F32)
        ref[CMP_PAD:CMP_PAD + n_ch, :] = val
        ref[CMP_PAD + n_ch:rows, :] = jnp.zeros((rows - CMP_PAD - n_ch, KV_W), F32)


def _cmp_rows(t):
    return -(-(CMP_PAD + t // CMP_STRIDE) // LANES) * LANES


def _compress_prompt(kv4, pe2, w1bd, w2bd, kgain, b, t):
    n_ch = t // CMP_STRIDE
    rows = _cmp_rows(t)
    full = lambda a: pl.BlockSpec(a.shape, lambda i: (0,) * a.ndim)
    out = jax.ShapeDtypeStruct((b, rows, KV_W), F32)
    return pl.pallas_call(
        functools.partial(_compress_kernel, n_ch=n_ch),
        grid=(b,),
        in_specs=[pl.BlockSpec((t, KV_W), lambda i: (i, 0)), pl.BlockSpec((t, KV_W), lambda i: (i, 1)),
                  full(pe2), full(w1bd), full(w2bd), full(kgain)],
        out_specs=[pl.BlockSpec((None, rows, KV_W), lambda i: (i, 0, 0))] * 2,
        out_shape=[out, out],
        compiler_params=_params("parallel"),
        name="compress_prompt",
    )(kv4, kv4, pe2, w1bd, w2bd, kgain)


def _top_k_mask(score, k):
    col = lax.broadcasted_iota(jnp.int32, score.shape, 1)
    sel = jnp.zeros(score.shape, F32)
    work = score
    for _ in range(k):
        m = jnp.max(work, axis=1, keepdims=True)
        first = jnp.min(jnp.where(work == m, col, score.shape[1]), axis=1, keepdims=True)
        pick = col == first
        sel = jnp.maximum(sel, pick.astype(F32))
        work = jnp.where(pick, NEG_INF, work)
    return sel


def _attn_prompt_kernel(q_ref, ng_ref, kc_ref, vc_ref, kall_ref, emat_ref, smap_ref, d01_ref, bc_ref, wtab_ref, o_ref,
                        m_scr, acc_scr, ob_scr):
    i = pl.program_id(1)
    lane = lax.broadcasted_iota(jnp.int32, (Q_BLOCK, LANES), 1)
    near0 = pl.multiple_of(i * (Q_BLOCK // CMP_STRIDE), 8)
    own = pl.multiple_of(i * Q_BLOCK, Q_BLOCK)
    scores, lhs_g = [], []

    for g in range(N_KV_HEADS):
        in_group = (lane >= HEAD_DIM) == (g == 1)
        lhs = jnp.concatenate(
            [jnp.where(in_group, q_ref[:, h * LANES:(h + 1) * LANES], jnp.zeros((), BF16)) for h in range(HPG)], axis=0)

        s_far = _dot_nt(lhs, kc_ref[...].astype(BF16))
        pcol = lax.broadcasted_iota(jnp.int32, s_far.shape, 1)
        s_far = jnp.where((pcol >= CMP_PAD) & (pcol < near0), s_far, NEG_INF)
        s_near = _dot_nt(lhs, kc_ref[pl.ds(near0, LANES), :].astype(BF16)) + bc_ref[g]
        ncol = lax.broadcasted_iota(jnp.int32, s_near.shape, 1)
        s_near = jnp.where(near0 + ncol >= CMP_PAD, s_near, NEG_INF)
        m = jnp.maximum(jnp.max(s_far, axis=1, keepdims=True), jnp.max(s_near, axis=1, keepdims=True))
        m = jnp.where(m == NEG_INF, 0.0, m)
        e_far = jnp.exp(s_far - m)
        e_near = jnp.exp(s_near - m)
        den = jnp.sum(e_far, axis=1, keepdims=True) + jnp.sum(e_near, axis=1, keepdims=True)
        inv = 1.0 / jnp.where(den > 0, den, 1.0)
        p_far = e_far * inv
        p_near = e_near * inv
        ob_scr[g, 0] = (_dot(p_far.astype(BF16), vc_ref[...].astype(BF16))
                        + _dot(p_near.astype(BF16), vc_ref[pl.ds(near0, LANES), :].astype(BF16)))

        imp_far = sum(p_far[h * Q_BLOCK:(h + 1) * Q_BLOCK] for h in range(HPG))
        imp_near = sum(p_near[h * Q_BLOCK:(h + 1) * Q_BLOCK] for h in range(HPG))
        score = (_dot_split(imp_far, smap_ref[...].astype(BF16))
                 + _dot_split(imp_near, smap_ref[pl.ds(near0, LANES), :].astype(BF16)))
        jcol = lax.broadcasted_iota(jnp.int32, score.shape, 1)
        cur = 2 * i + (lax.broadcasted_iota(jnp.int32, score.shape, 0) >= SEL_BLOCK).astype(jnp.int32)
        forced = (jcol == 0) | (jcol == cur) | (jcol == cur - 1)
        score = jnp.where(forced, jnp.inf, score)
        scores.append(jnp.where(jcol <= cur, score, NEG_INF))
        lhs_g.append(lhs)

        s_w = _dot_nt(lhs, kall_ref[pl.ds(own, WINDOW + Q_BLOCK), 2 * KV_W:3 * KV_W]) + wtab_ref[g]
        wcol = lax.broadcasted_iota(jnp.int32, s_w.shape, 1)
        s_w = jnp.where(wcol >= WINDOW - own, s_w, NEG_INF)
        e_w = jnp.exp(s_w - jnp.max(s_w, axis=1, keepdims=True))
        ob_scr[g, 2] = (_dot(e_w.astype(BF16), kall_ref[pl.ds(own, WINDOW + Q_BLOCK), 3 * KV_W:4 * KV_W])
                        / jnp.sum(e_w, axis=1, keepdims=True))

    sel_both = _top_k_mask(jnp.concatenate(scores, axis=0), TOP_K)
    n_plain = jnp.maximum(i - 1, 0)
    per_chunk = SEL_CHUNK // Q_BLOCK
    n_full = n_plain // per_chunk
    n_rest = n_plain % per_chunk
    ones_v = jnp.ones((2 * SEL_CHUNK, LANES), BF16)

    for g in range(N_KV_HEADS):
        lhs = lhs_g[g]
        sel = sel_both[g * Q_BLOCK:(g + 1) * Q_BLOCK]
        not_sel = jnp.concatenate([(sel - 1.0) * MASK_BIG] * HPG, axis=0).astype(BF16)
        lhs_sel = jnp.concatenate([lhs, not_sel], axis=1)

        def sel_logits(pos, width):
            keys = jnp.concatenate([kall_ref[pl.ds(WINDOW + pos, width), 0:KV_W],
                                    emat_ref[pl.ds(Q_BLOCK + pos, width), :]], axis=1)
            return _dot_nt(lhs_sel, keys)

        def sel_values(pos, width):
            return jnp.concatenate([kall_ref[pl.ds(WINDOW + pos, width), KV_W:2 * KV_W], ones_v[:width]], axis=1)

        def lane_tile_max(s):
            out = s[:, 0:LANES]
            for k in range(1, s.shape[1] // LANES):
                out = jnp.maximum(out, s[:, k * LANES:(k + 1) * LANES])
            return out

        def weights(s):
            mb = m_scr[...]
            return jnp.concatenate([jnp.exp(s[:, k * LANES:(k + 1) * LANES] - mb)
                                    for k in range(s.shape[1] // LANES)], axis=1).astype(BF16)

        def rest_logits():
            s = sel_logits(pl.multiple_of(n_full * SEL_CHUNK, SEL_CHUNK), SEL_CHUNK)
            col = lax.broadcasted_iota(jnp.int32, s.shape, 1)
            return jnp.where(col < n_rest * Q_BLOCK, s, NEG_INF)

        near = pl.multiple_of(own - Q_BLOCK, Q_BLOCK)
        s_near2 = sel_logits(near, 2 * Q_BLOCK) + d01_ref[g]
        ncol2 = lax.broadcasted_iota(jnp.int32, s_near2.shape, 1)
        s_near2 = jnp.where((i == 0) & (ncol2 < Q_BLOCK), NEG_INF, s_near2)

        def over_plain_keys(fn):
            def pair_body(c, carry):
                pos = pl.multiple_of(c * 2 * SEL_CHUNK, 2 * SEL_CHUNK)
                fn(sel_logits(pos, 2 * SEL_CHUNK), pos, 2 * SEL_CHUNK)
                return carry

            lax.fori_loop(0, n_full // 2, pair_body, 0)
            last = pl.multiple_of(n_full * SEL_CHUNK, SEL_CHUNK)

            @pl.when(n_full % 2 == 1)
            def _():
                odd = pl.multiple_of((n_full - 1) * SEL_CHUNK, SEL_CHUNK)
                fn(sel_logits(odd, SEL_CHUNK), odd, SEL_CHUNK)

            @pl.when(n_rest > 0)
            def _():
                fn(rest_logits(), last, SEL_CHUNK)

        def take_max(s, pos, width):
            m_scr[...] = jnp.maximum(m_scr[...], lane_tile_max(s))

        def accumulate(s, pos, width):
            acc_scr[...] += _dot(weights(s), sel_values(pos, width))

        m_scr[...] = lane_tile_max(s_near2)
        over_plain_keys(take_max)
        m_scr[...] = jnp.broadcast_to(jnp.max(m_scr[...], axis=1, keepdims=True), m_scr.shape)
        acc_scr[...] = _dot(weights(s_near2), sel_values(near, 2 * Q_BLOCK))
        over_plain_keys(accumulate)
        ob_scr[g, 1] = acc_scr[:, 0:LANES] / acc_scr[:, LANES:2 * LANES]

    sig = jax.nn.sigmoid(ng_ref[...])
    for h in range(HPG):
        slab = None
        for j in range(3):
            part = []
            for g in range(N_KV_HEADS):
                c = j * N_Q_HEADS + g * HPG + h
                part.append(sig[:, c:c + 1] * ob_scr[g, j, h * Q_BLOCK:(h + 1) * Q_BLOCK, :])
            term = jnp.where(lane < HEAD_DIM, part[0], part[1])
            slab = term if slab is None else slab + term
        o_ref[:, h * LANES:(h + 1) * LANES] = slab.astype(BF16)


def _attn_prompt(q, ng, kc, vc, kall, emat, smap, d01, bc, wtab, b, t):
    nq = t // Q_BLOCK
    rows = HPG * Q_BLOCK
    full = lambda a: pl.BlockSpec(a.shape, lambda bi, i: (0,) * a.ndim)
    per_batch = lambda a: pl.BlockSpec((None,) + a.shape[1:], lambda bi, i: (bi, 0, 0))
    return pl.pallas_call(
        _attn_prompt_kernel,
        grid=(b, nq),
        in_specs=[
            pl.BlockSpec((Q_BLOCK, NSA_W), lambda bi, i: (bi * nq + i, 0)),
            pl.BlockSpec((Q_BLOCK, LANES), lambda bi, i: (bi * nq + i, 0)),
            per_batch(kc), per_batch(vc), per_batch(kall),
            full(emat), full(smap), full(d01), full(bc), full(wtab),
        ],
        out_specs=pl.BlockSpec((Q_BLOCK, NSA_W), lambda bi, i: (bi * nq + i, 0)),
        out_shape=jax.ShapeDtypeStruct((b * t, NSA_W), BF16),
        scratch_shapes=[pltpu.VMEM((rows, LANES), F32), pltpu.VMEM((rows, 2 * LANES), F32),
                        pltpu.VMEM((N_KV_HEADS, 3, rows, LANES), F32)],
        compiler_params=_params("parallel", "arbitrary"),
        name="attn_prompt",
    )(q, ng, kc, vc, kall, emat, smap, d01, bc, wtab)


def _pool_rows(ext_scr, base, u, pos):
    m, width = u.shape
    lane = lax.broadcasted_iota(jnp.int32, (m, width), 1)
    gc = width // len(POOL_WINDOWS)
    run = u
    pooled = None
    for k in range(1, max(POOL_WINDOWS)):
        run = run + ext_scr[base - k:base - k + m, :]
        w = k + 1
        if w in POOL_WINDOWS:
            mean = run / jnp.minimum(pos + 1, w).astype(F32)
            pooled = mean if pooled is None else jnp.where(lane >= POOL_WINDOWS.index(w) * gc, mean, pooled)
    return pooled - u


def _conv_rows(ext_scr, base, m, cw_ref):
    y = None
    for j in range(CONV_K):
        off = base - (CONV_K - 1) + j
        term = cw_ref[j:j + 1, :] * ext_scr[off:off + m, :]
        y = term if y is None else y + term
    return y


def _mix_prompt_kernel(u_ref, uh_ref, z_ref, zh_ref, cb_ref, cw_ref, d_ref, c_ref, ext_scr, *, tm):
    i = pl.program_id(1)
    has_hist = (i > 0).astype(F32)
    pos = i * tm + lax.broadcasted_iota(jnp.int32, u_ref.shape, 0)
    ext_scr[0:HALO, :] = uh_ref[...] * has_hist
    ext_scr[HALO:HALO + tm, :] = u_ref[...]
    d_ref[...] = _pool_rows(ext_scr, HALO, u_ref[...], pos).astype(BF16)
    ext_scr[0:HALO, :] = zh_ref[...] * has_hist
    ext_scr[HALO:HALO + tm, :] = z_ref[...]
    c_ref[...] = (cb_ref[...] * _conv_rows(ext_scr, HALO, tm, cw_ref)).astype(BF16)


def _mix_prompt(u, z, cb, conv_w, b, t, tm):
    nt = t // tm
    width = u.shape[1]
    row = pl.BlockSpec((tm, width), lambda bi, i: (bi * nt + i, 0))
    halo = pl.BlockSpec((HALO, width), lambda bi, i: (jnp.maximum((bi * nt + i) * (tm // HALO) - 1, 0), 0))
    out = jax.ShapeDtypeStruct((b * t, width), BF16)
    return pl.pallas_call(
        functools.partial(_mix_prompt_kernel, tm=tm),
        grid=(b, nt),
        in_specs=[row, halo, row, halo, row, pl.BlockSpec(conv_w.shape, lambda bi, i: (0, 0))],
        out_specs=[row, row],
        out_shape=[out, out],
        scratch_shapes=[pltpu.VMEM((HALO + tm, width), F32)],
        compiler_params=_params("parallel", "parallel"),
        name="mix_prompt",
    )(u, u, z, z, cb, conv_w)


def _merge_kernel(x_ref, a_ref, d_ref, c_ref, mg_ref, wa_ref, wp_ref, ps_ref, wc_ref, wo_ref, o_ref):
    dm = x_ref.shape[1]
    gates = jax.nn.sigmoid(mg_ref[...])
    m = gates[:, 0:dm] * _dot(a_ref[...], wa_ref[...])
    m = m + gates[:, dm:2 * dm] * (_dot(d_ref[...], wp_ref[...]) * ps_ref[...])
    m = m + gates[:, 2 * dm:3 * dm] * _dot(c_ref[...], wc_ref[...])
    o_ref[...] = x_ref[...] + _dot(m.astype(BF16), wo_ref[...])


def _merge(x, attn, d, c, mg, wa, wp, ps, wc, wo, tm):
    n, dm = x.shape
    row = lambda a: pl.BlockSpec((tm, a.shape[1]), lambda i: (i, 0))
    full = lambda a: pl.BlockSpec(a.shape, lambda i: (0, 0))
    ps = ps.reshape(1, dm)
    return pl.pallas_call(
        _merge_kernel,
        grid=(n // tm,),
        in_specs=[row(x), row(attn), row(d), row(c), row(mg), full(wa), full(wp), full(ps), full(wc), full(wo)],
        out_specs=row(x),
        out_shape=jax.ShapeDtypeStruct((n, dm), F32),
        compiler_params=_params("parallel"),
        name="merge",
    )(x, attn, d, c, mg, wa, wp, ps, wc, wo)


SAMPLE_ROWS = 8
SAMPLE_NB = 2


def _cmp_partial_kernel(xt_ref, pe_ref, w1_ref, o_ref, x_scr, *, n_ch):
    page = xt_ref.shape[2]
    for p in range(xt_ref.shape[0]):
        x_scr[p * page:(p + 1) * page, :] = xt_ref[p].T
    first = jnp.zeros((n_ch, KV_W), F32)
    second = jnp.zeros((n_ch, KV_W), F32)
    for l in range(CMP_STRIDE):
        xl = x_scr[pl.ds(l, n_ch, stride=CMP_STRIDE), :]
        first += _dot((xl + pe_ref[l:l + 1, :]).astype(BF16), w1_ref[l])
        second += _dot((xl + pe_ref[CMP_STRIDE + l:CMP_STRIDE + l + 1, :]).astype(BF16), w1_ref[CMP_STRIDE + l])
    o_ref[:, 0:KV_W] = first
    o_ref[:, KV_W:2 * KV_W] = second


def _cmp_partial(cache_t, pe2, w1bd, layer, n_pool):
    page = cache_t.shape[2]
    ch = page // CMP_STRIDE
    pp = next(p for p in (64, 32, 16, 8, 4, 2, 1) if n_pool % p == 0)
    nb = n_pool // pp
    return pl.pallas_call(
        functools.partial(_cmp_partial_kernel, n_ch=pp * ch),
        grid=(nb, 2),
        in_specs=[pl.BlockSpec((pp, KV_W, page), lambda i, s: (layer * nb + i, s, 0)),
                  pl.BlockSpec((None,) + pe2.shape[1:], lambda i, s: (s, 0, 0)),
                  pl.BlockSpec((None,) + w1bd.shape[1:], lambda i, s: (s, 0, 0, 0))],
        out_specs=pl.BlockSpec((pp * ch, 2 * KV_W), lambda i, s: (i, s)),
        out_shape=jax.ShapeDtypeStruct((n_pool * ch, 4 * KV_W), F32),
        scratch_shapes=[pltpu.VMEM((pp * page, KV_W), F32)],
        compiler_params=_params("parallel", "arbitrary"),
        name="cmp_partial",
    )(cache_t, pe2, w1bd)


def _softmax_pair(s_a, vt_a, s_b, v_b):
    m = jnp.maximum(jnp.max(s_a, axis=1, keepdims=True), jnp.max(s_b, axis=1, keepdims=True))
    e_a = jnp.exp(s_a - m)
    e_b = jnp.exp(s_b - m)
    den = jnp.sum(e_a, axis=1, keepdims=True) + jnp.sum(e_b, axis=1, keepdims=True)
    return (_dot_nt(e_a.astype(BF16), vt_a) + _dot(e_b.astype(BF16), v_b)) / den


def _sample_kernel(pt_ref, q_ref, ng_ref, kv_ref, win_ref, wnt_ref, u_ref, z_ref, cb_ref, swin_ref, spool_ref,
                   sconv_ref, cache_ref, fs_ref, w2_ref, kg_ref, cw_ref, smap_ref, emat_ref, bc_ref, dsel_ref, dwin_ref,
                   dnew_ref, attn_ref, d_ref, c_ref, owin_ref, opool_ref, oconv_ref,
                   kvbuf, fsbuf, sems, ext_scr, *, layer_base, n_pages, page, s_len, past):
    i = pl.program_id(0)
    slot = i % 2
    ch = page // CMP_STRIDE
    sr = SAMPLE_ROWS

    def copies(step, slot_):
        out = []
        for sb in range(SAMPLE_NB):
            for pg in range(n_pages):
                pid = pt_ref[step * SAMPLE_NB + sb, pg]
                out.append(pltpu.make_async_copy(
                    cache_ref.at[layer_base + pid, pl.ds(2 * KV_W, 2 * KV_W), :],
                    kvbuf.at[slot_, sb, :, pl.ds(pg * page, page)], sems.at[slot_, 0]))
                out.append(pltpu.make_async_copy(
                    fs_ref.at[pid], fsbuf.at[slot_, sb, pl.ds(pg * ch, ch), :], sems.at[slot_, 1]))
        return out

    @pl.when(i == 0)
    def _():
        for c in copies(0, 0):
            c.start()

    @pl.when(i + 1 < pl.num_programs(0))
    def _():
        for c in copies(i + 1, 1 - slot):
            c.start()

    for c in copies(i, slot):
        c.wait()

    lane = lax.broadcasted_iota(jnp.int32, (sr, LANES), 1)
    grp_rows = N_KV_HEADS * sr
    q_all = q_ref[...].astype(F32)
    sig_all = jax.nn.sigmoid(ng_ref[...])
    kvn = kv_ref[...]
    winn = win_ref[...]
    pad_rows = jnp.zeros((LANES - sr, KV_W), F32)
    w_keep = swin_ref.shape[2]
    wlane = lax.broadcasted_iota(jnp.int32, swin_ref.shape[1:], 1)
    hist = spool_ref.shape[1]
    ck = sconv_ref.shape[1]
    attn_rows, d_rows, c_rows = [], [], []

    for sb in range(SAMPLE_NB):
        r0 = sb * sr
        q8 = q_all[r0:r0 + sr]
        pieces = []
        for h in range(HPG):
            for g in range(N_KV_HEADS):
                pieces.append(jnp.where((lane >= HEAD_DIM) == (g == 1), q8[:, h * LANES:(h + 1) * LANES], 0.0))
        lhs = jnp.concatenate(pieces, axis=0).astype(BF16)

        def new_rows(x8):
            return jnp.concatenate([x8, pad_rows], axis=0).astype(BF16)

        ks_new = new_rows(kvn[r0:r0 + sr, 2 * KV_W:3 * KV_W])
        vs_new = new_rows(kvn[r0:r0 + sr, 3 * KV_W:4 * KV_W])
        kw_new = new_rows(winn[r0:r0 + sr, 0:KV_W])
        vw_new = new_rows(winn[r0:r0 + sr, KV_W:2 * KV_W])

        fs = fsbuf[slot, sb]
        n_ch = fs.shape[0]
        cmp = []
        for s_ in range(2):
            hid = (fs[:, 2 * s_ * KV_W:(2 * s_ + 1) * KV_W]
                   + pltpu.roll(fs[:, (2 * s_ + 1) * KV_W:(2 * s_ + 2) * KV_W], n_ch - 1, 0))
            cmp.append(_dot((hid * jax.nn.sigmoid(hid)).astype(BF16), w2_ref[s_]))
        kc = _head_rms(cmp[0], kg_ref[...]).astype(BF16)
        vc = cmp[1].astype(BF16)
        s_c = _dot_nt(lhs, kc) + bc_ref[...]
        e_c = jnp.exp(s_c - jnp.max(s_c, axis=1, keepdims=True))
        p_c = e_c / jnp.sum(e_c, axis=1, keepdims=True)
        o_c = _dot(p_c.astype(BF16), vc)

        imp = sum(p_c[h * grp_rows:(h + 1) * grp_rows] for h in range(HPG))
        score = _dot_split(imp, smap_ref[...])
        jcol = lax.broadcasted_iota(jnp.int32, score.shape, 1)
        cur = (past + lax.broadcasted_iota(jnp.int32, score.shape, 0) % sr) // SEL_BLOCK
        forced = (jcol == 0) | (jcol == cur) | (jcol == cur - 1)
        score = jnp.where(forced, jnp.inf, score)
        score = jnp.where(jcol <= cur, score, NEG_INF)
        sel = _top_k_mask(score, TOP_K)
        not_sel = jnp.concatenate([(sel - 1.0) * MASK_BIG] * HPG, axis=0).astype(BF16)
        lhs_sel = jnp.concatenate([lhs, not_sel], axis=1)

        k_past = jnp.concatenate([kvbuf[slot, sb, 0:KV_W, :].astype(BF16), emat_ref[...]], axis=0)
        o_s = _softmax_pair(_dot(lhs_sel, k_past) + dsel_ref[...], kvbuf[slot, sb, KV_W:2 * KV_W, :].astype(BF16),
                            _dot_nt(lhs, ks_new) + dnew_ref[...], vs_new)
        o_w = _softmax_pair(_dot(lhs, swin_ref[sb, 0:KV_W, :].astype(BF16)) + dwin_ref[...],
                            swin_ref[sb, KV_W:2 * KV_W, :].astype(BF16),
                            _dot_nt(lhs, kw_new) + dnew_ref[...], vw_new)

        branch = (o_c, o_s, o_w)
        sig8 = sig_all[r0:r0 + sr]
        slabs = []
        for h in range(HPG):
            slab = None
            for j in range(3):
                part = []
                for g in range(N_KV_HEADS):
                    c = j * N_Q_HEADS + g * HPG + h
                    lo = h * grp_rows + g * sr
                    part.append(sig8[:, c:c + 1] * branch[j][lo:lo + sr])
                term = jnp.where(lane < HEAD_DIM, part[0], part[1])
                slab = term if slab is None else slab + term
            slabs.append(slab)
        attn_rows.append(jnp.concatenate(slabs, axis=1))

        state = pltpu.roll(swin_ref[sb], w_keep - s_len, 1)
        for tt in range(s_len):
            state = jnp.where(wlane == w_keep - s_len + tt, wnt_ref[sb, :, tt:tt + 1], state)
        owin_ref[sb] = state

        u8 = u_ref[r0:r0 + sr, :]
        ext_scr[0:hist, :] = spool_ref[sb]
        ext_scr[hist:hist + sr, :] = u8
        d_rows.append(_pool_rows(ext_scr, hist, u8, hist + lax.broadcasted_iota(jnp.int32, u8.shape, 0)))
        opool_ref[sb] = ext_scr[s_len:s_len + hist, :]
        ext_scr[0:ck, :] = sconv_ref[sb]
        ext_scr[ck:ck + sr, :] = z_ref[r0:r0 + sr, :]
        c_rows.append(cb_ref[r0:r0 + sr, :] * _conv_rows(ext_scr, ck, sr, cw_ref))
        oconv_ref[sb] = ext_scr[s_len:s_len + ck, :]

    attn_ref[...] = jnp.concatenate(attn_rows, axis=0).astype(BF16)
    d_ref[...] = jnp.concatenate(d_rows, axis=0).astype(BF16)
    c_ref[...] = jnp.concatenate(c_rows, axis=0).astype(BF16)


def _sample_layer(layer, page_table, q, ng, kv4, win, wnt, u, z, cb, swin3, spool3, sconv3, cache3, fs3, w2bd, kgain,
                  conv_w, smap_s, emat, bc_s, dsel, dwin, dnew, n_p, db, s_len, n_pool):
    nsteps = db // SAMPLE_NB
    rb = SAMPLE_NB * SAMPLE_ROWS
    r0 = n_p // rb
    page = cache3.shape[2]
    n_pages = page_table.shape[1]
    past = n_pages * page
    tok = lambda a: pl.BlockSpec((rb, a.shape[1]), lambda i, pt: (r0 + i, 0))
    new_t = pl.BlockSpec((SAMPLE_NB,) + wnt.shape[1:], lambda i, pt: (i, 0, 0))
    state = lambda a: pl.BlockSpec((SAMPLE_NB,) + a.shape[1:], lambda i, pt: (layer * nsteps + i, 0, 0))
    full = lambda a: pl.BlockSpec(a.shape, lambda i, pt: (0,) * a.ndim)
    hbm = pl.BlockSpec(memory_space=pl.ANY)
    out_tok = lambda w_: pl.BlockSpec((rb, w_), lambda i, pt: (i, 0))
    out_state = lambda a: pl.BlockSpec((SAMPLE_NB,) + a.shape[1:], lambda i, pt: (i, 0, 0))
    width = u.shape[1]
    n_tok = db * SAMPLE_ROWS
    return pl.pallas_call(
        functools.partial(_sample_kernel, layer_base=layer * n_pool, n_pages=n_pages, page=page, s_len=s_len, past=past),
        grid_spec=pltpu.PrefetchScalarGridSpec(
            num_scalar_prefetch=1,
            grid=(nsteps,),
            in_specs=[tok(q), tok(ng), tok(kv4), tok(win), new_t, tok(u), tok(z), tok(cb), state(swin3), state(spool3),
                      state(sconv3), hbm, hbm, full(w2bd), full(kgain), full(conv_w), full(smap_s), full(emat),
                      full(bc_s), full(dsel), full(dwin), full(dnew)],
            out_specs=[out_tok(NSA_W), out_tok(width), out_tok(width), out_state(swin3), out_state(spool3),
                       out_state(sconv3)],
            scratch_shapes=[pltpu.VMEM((2, SAMPLE_NB, 2 * KV_W, past), F32),
                            pltpu.VMEM((2, SAMPLE_NB, past // CMP_STRIDE, 4 * KV_W), F32),
                            pltpu.SemaphoreType.DMA((2, 2)),
                            pltpu.VMEM((4 * SAMPLE_ROWS, width), F32)]),
        out_shape=[jax.ShapeDtypeStruct((n_tok, NSA_W), BF16), jax.ShapeDtypeStruct((n_tok, width), BF16),
                   jax.ShapeDtypeStruct((n_tok, width), BF16), jax.ShapeDtypeStruct((db,) + swin3.shape[1:], F32),
                   jax.ShapeDtypeStruct((db,) + spool3.shape[1:], F32),
                   jax.ShapeDtypeStruct((db,) + sconv3.shape[1:], F32)],
        compiler_params=_params("arbitrary"),
        name="sample_mix",
    )(page_table, q, ng, kv4, win, wnt, u, z, cb, swin3, spool3, sconv3, cache3, fs3, w2bd, kgain, conv_w, smap_s,
      emat, bc_s, dsel, dwin, dnew)


def _bucket_table(n):
    d = np.arange(n)
    nf = np.maximum(d, 1).astype(np.float32)
    large = MAX_EXACT + (np.log(nf / np.float32(MAX_EXACT)) / np.float32(math.log(MAX_DISTANCE / MAX_EXACT))
                         * np.float32(N_BUCKETS - MAX_EXACT)).astype(np.int32)
    return np.where(d < MAX_EXACT, d, np.minimum(large, N_BUCKETS - 1)).astype(np.int32)


def _bias_values(rel_bias, dist, visible=None):
    visible = (dist >= 0) if visible is None else (visible & (dist >= 0))
    bkt = _bucket_table(max(int(dist.max()) + 1, MAX_DISTANCE))[np.clip(dist, 0, None)]
    tab = rel_bias - rel_bias[N_BUCKETS - 1:N_BUCKETS]
    vals = jnp.where(jnp.asarray(visible)[..., None], tab[jnp.asarray(bkt)], NEG_INF)
    return vals.transpose(2, 0, 1).reshape((N_KV_HEADS, HPG) + dist.shape)


def _bias_table(rel_bias, dist):
    r, c = dist.shape
    return _bias_values(rel_bias, dist).reshape(N_KV_HEADS, HPG * r, c)


def _bias_table_sample(rel_bias, dist, visible=None):
    r, c = dist.shape
    return _bias_values(rel_bias, dist, visible).transpose(1, 0, 2, 3).reshape(HPG * N_KV_HEADS * r, c)


def _sel_map(n_cmp, n_rows, n_cols, pad):
    r_sel = SEL_BLOCK // CMP_STRIDE
    r_cmp = CMP_BLOCK // CMP_STRIDE
    out = np.zeros((n_rows, n_cols), np.float32)
    for j in range(n_cols):
        for m in range(r_sel):
            for n in range(r_cmp):
                c = r_sel * j + m - n
                if 0 <= c < n_cmp:
                    out[pad + c, j] += 1.0
    return out


def _proj_columns():
    q0, kv0, ng0 = 0, NSA_W, NSA_W + 6 * KV_W
    u0 = ng0 + 3 * N_Q_HEADS
    idx = np.full((C_END,), -1, np.int64)
    for k in range(HPG):
        for g in range(N_KV_HEADS):
            dst = C_Q + k * LANES + g * HEAD_DIM
            idx[dst:dst + HEAD_DIM] = q0 + (g * HPG + k) * HEAD_DIM + np.arange(HEAD_DIM)
    for dst, slot in ((C_KCMP, 0), (C_VCMP, 1), (C_KSEL, 2), (C_VSEL, 3), (C_KWIN, 4), (C_VWIN, 5)):
        idx[dst:dst + KV_W] = kv0 + slot * KV_W + np.arange(KV_W)
    idx[C_U:C_NG] = u0 + np.arange(C_NG - C_U)
    for j in range(3):
        for g in range(N_KV_HEADS):
            for h in range(HPG):
                idx[C_NG + j * N_Q_HEADS + g * HPG + h] = ng0 + (g * HPG + h) * 3 + j
    return idx


def _slab_rows():
    idx = np.zeros((NSA_W,), np.int64)
    for k in range(HPG):
        for g in range(N_KV_HEADS):
            dst = k * LANES + g * HEAD_DIM
            idx[dst:dst + HEAD_DIM] = (g * HPG + k) * HEAD_DIM + np.arange(HEAD_DIM)
    return idx


def _block_diag2(w):
    z = jnp.zeros_like(w)
    return jnp.concatenate([jnp.concatenate([w, z], axis=-1), jnp.concatenate([z, w], axis=-1)], axis=-2)


def _token_tile(n):
    for tm in (512, 256, 128):
        if n % tm == 0:
            return tm
    raise ValueError(f"token count {n} is not a multiple of 128")


def kernel(x_prompt, x_sample, cache_kv, state_win, state_pool, state_conv, page_table, rel_bias, ln_ffn1, w_ffn1_gate, w_ffn1_up, w_ffn1_down, ln_mix, w_in, q_norm, k_norm, cmp_pe, cmp_w1, cmp_w2, w_nsa_out, w_pool, pool_scale, conv_w, w_conv_out, w_o, ln_ffn2, w_ffn2_gate, w_ffn2_up, w_ffn2_down):
    b, t, dm = x_prompt.shape
    db, s, _ = x_sample.shape
    depth, n_pool, page = cache_kv.shape[:3]
    n_pages = page_table.shape[1]
    past = n_pages * page
    w_keep = state_win.shape[2]
    sr = SAMPLE_ROWS
    n_p, n_s = b * t, db * sr
    assert t % Q_BLOCK == 0 and t >= WINDOW and t // SEL_BLOCK <= LANES
    assert s <= sr and db % SAMPLE_NB == 0 and n_p % (SAMPLE_NB * sr) == 0
    assert (past + s) // CMP_STRIDE == past // CMP_STRIDE == LANES and w_keep <= past and s <= w_keep
    assert -(-(past + s) // SEL_BLOCK) <= LANES
    tm = _token_tile(n_p + n_s)
    tm_proj = min(tm, 256)
    tm_mix = _token_tile(t)

    cols = _proj_columns()
    w_proj = (jnp.take(w_in, jnp.asarray(np.clip(cols, 0, None)), axis=2)
              * jnp.asarray(cols >= 0, F32)).astype(BF16)
    head_gain = jnp.concatenate([jnp.tile(q_norm, (1, N_Q_HEADS)), jnp.tile(k_norm[:, 1], (1, N_KV_HEADS)),
                                 jnp.tile(k_norm[:, 2], (1, N_KV_HEADS))], axis=1).reshape(depth, 1, N_NORMED)
    cmp_gain = jnp.tile(k_norm[:, 0], (1, N_KV_HEADS)).reshape(depth, 1, KV_W)
    pe2 = jnp.tile(cmp_pe, (1, 1, 1, N_KV_HEADS))
    w1bd = _block_diag2(cmp_w1).astype(BF16)
    w2bd = _block_diag2(cmp_w2).astype(BF16)
    wa = jnp.take(w_nsa_out, jnp.asarray(_slab_rows()), axis=1).astype(BF16)
    n_grp, gc, ge = w_pool.shape[1:]
    wp = jnp.zeros((depth, n_grp * gc, n_grp * ge), F32)
    for gi in range(n_grp):
        wp = wp.at[:, gi * gc:(gi + 1) * gc, gi * ge:(gi + 1) * ge].set(w_pool[:, gi])
    wp = wp.astype(BF16)
    wc = w_conv_out.astype(BF16)
    wo = w_o.astype(BF16)
    ffn1 = [w.astype(BF16) for w in (w_ffn1_gate, w_ffn1_up, w_ffn1_down)]
    ffn2 = [w.astype(BF16) for w in (w_ffn2_gate, w_ffn2_up, w_ffn2_down)]

    ti = np.arange(Q_BLOCK)
    d01 = _bias_table(rel_bias, Q_BLOCK + ti[:, None] - np.arange(2 * Q_BLOCK)[None, :])
    near_c = np.arange(LANES)
    bc = _bias_table(rel_bias, ti[:, None] - CMP_STRIDE * (near_c[None, :] - CMP_PAD) - (CMP_BLOCK - 1))
    n_cmp = t // CMP_STRIDE - CMP_BLOCK // CMP_STRIDE + 1
    smap = jnp.asarray(_sel_map(n_cmp, _cmp_rows(t), LANES, CMP_PAD))
    wj = np.arange(WINDOW + Q_BLOCK)[None, :]
    dist_pw = WINDOW + ti[:, None] - wj
    wtab = _bias_values(rel_bias, dist_pw, dist_pw < WINDOW).reshape(N_KV_HEADS, HPG * Q_BLOCK, WINDOW + Q_BLOCK)
    key_pos = np.arange(-Q_BLOCK, t + SEL_CHUNK)[:, None]
    emat_p = jnp.asarray((key_pos >= 0) & (key_pos < t)
                         & (key_pos // SEL_BLOCK == np.arange(LANES)[None, :])).astype(BF16)

    tq = np.arange(sr)[:, None]
    n_chunks = past // CMP_STRIDE
    n_cmp_s = n_chunks - CMP_BLOCK // CMP_STRIDE + 1
    cn = np.arange(n_chunks)[None, :]
    bc_s = _bias_table_sample(rel_bias, past + tq - (CMP_STRIDE * cn + CMP_BLOCK - 1), cn < n_cmp_s)
    dsel = _bias_table_sample(rel_bias, past + tq - np.arange(past)[None, :])
    tn = np.arange(LANES)[None, :]
    dnew = _bias_table_sample(rel_bias, tq - tn, tn < s)
    dist_w = w_keep + tq - np.arange(w_keep)[None, :]
    dwin = _bias_table_sample(rel_bias, dist_w, dist_w < WINDOW)
    smap_s = jnp.asarray(_sel_map(n_cmp_s, n_chunks, LANES, 0)).astype(BF16)
    emat = jnp.asarray(np.arange(LANES)[:, None] == np.arange(past)[None, :] // SEL_BLOCK).astype(BF16)

    cache3 = cache_kv.transpose(0, 1, 3, 4, 5, 2).reshape(depth * n_pool, -1, page)
    swin3 = state_win.transpose(0, 1, 3, 4, 5, 2).reshape(depth * db, -1, w_keep)
    spool3 = state_pool.reshape((depth * db,) + state_pool.shape[2:])
    sconv3 = state_conv.reshape((depth * db,) + state_conv.shape[2:])

    xs = jnp.pad(x_sample, ((0, 0), (0, sr - s), (0, 0)))
    x = jnp.concatenate([x_prompt.reshape(n_p, dm), xs.reshape(n_s, dm)], axis=0)
    outs = [[] for _ in range(8)]
    for l in range(depth):
        x = _ffn(x, ln_ffn1[l], ffn1[0][l], ffn1[1][l], ffn1[2][l], tm)
        q, kv4, win, ksv, u, cb, z, mg, ng = _proj(x, ln_mix[l], w_proj[l], head_gain[l], tm_proj)

        kc, vc = _compress_prompt(kv4, pe2[l], w1bd[l], w2bd[l], cmp_gain[l], b, t)
        kall = jnp.pad(ksv[:n_p].reshape(b, t, 4 * KV_W), ((0, 0), (WINDOW, SEL_CHUNK), (0, 0)))
        attn_p = _attn_prompt(q, ng, kc, vc, kall, emat_p, smap, d01, bc, wtab, b, t)
        d_p, c_p = _mix_prompt(u, z, cb, conv_w[l], b, t, tm_mix)

        fs3 = _cmp_partial(cache3, pe2[l], w1bd[l], l, n_pool).reshape(n_pool, page // CMP_STRIDE, -1)
        wnt = win[n_p:].reshape(db, sr, -1).transpose(0, 2, 1)
        attn_s, d_s, c_s, s_win, s_pool, s_conv = _sample_layer(
            l, page_table, q, ng, kv4, win, wnt, u, z, cb, swin3, spool3, sconv3, cache3, fs3, w2bd[l], cmp_gain[l],
            conv_w[l], smap_s, emat, bc_s, dsel, dwin, dnew, n_p, db, s, n_pool)

        x = _merge(x, jnp.concatenate([attn_p, attn_s]), jnp.concatenate([d_p, d_s]), jnp.concatenate([c_p, c_s]),
                   mg, wa[l], wp[l], pool_scale[l], wc[l], wo[l], tm)
        x = _ffn(x, ln_ffn2[l], ffn2[0][l], ffn2[1][l], ffn2[2][l], tm)

        p_keep = min(WINDOW, t)
        outs[0].append(kv4[:n_p].reshape(b, t, 4, N_KV_HEADS, HEAD_DIM))
        outs[1].append(win[:n_p].reshape(b, t, 2, N_KV_HEADS, HEAD_DIM)[:, t - p_keep:])
        outs[2].append(u[:n_p].reshape(b, t, -1)[:, t - POOL_STATE:])
        outs[3].append(z[:n_p].reshape(b, t, -1)[:, t - (CONV_K - 1):])
        outs[4].append(kv4[n_p:].reshape(db, sr, 4, N_KV_HEADS, HEAD_DIM)[:, :s])
        outs[5].append(s_win.reshape(db, 2, N_KV_HEADS, HEAD_DIM, w_keep).transpose(0, 4, 1, 2, 3))
        outs[6].append(s_pool)
        outs[7].append(s_conv)

    y_prompt = x[:n_p].reshape(b, t, dm)
    y_sample = x[n_p:].reshape(db, sr, dm)[:, :s]
    return (y_prompt, y_sample) + tuple(jnp.stack(o) for o in outs)
```

```python
import functools
import math

import numpy as np
import jax
import jax.numpy as jnp
from jax import lax
from jax.experimental import pallas as pl
from jax.experimental.pallas import tpu as pltpu

F32 = jnp.float32
BF16 = jnp.bfloat16
NEG_INF = float("-inf")

HEAD_DIM = 64
N_KV_HEADS = 2
HPG = 4
N_Q_HEADS = N_KV_HEADS * HPG
NSA_W = N_Q_HEADS * HEAD_DIM
KV_W = N_KV_HEADS * HEAD_DIM
CMP_BLOCK = 32
CMP_STRIDE = 16
SEL_BLOCK = 64
TOP_K = 8
WINDOW = 512
Q_BLOCK = 128
N_BUCKETS = 32
MAX_EXACT = 16
MAX_DISTANCE = 128
POOL_WINDOWS = (2, 4, 8, 16)
POOL_STATE = 15
CONV_K = 3
RMS_EPS = 1e-6
LANES = 128
HALO = 16
CMP_PAD = 112
SEL_CHUNK = 512
MASK_BIG = 1e30
VMEM_LIMIT = 56 * 1024 * 1024


def _dot(a, b):
    return jnp.dot(a, b, preferred_element_type=F32)


def _dot_nt(a, b):
    return lax.dot_general(a, b, (((1,), (1,)), ((), ())), preferred_element_type=F32)


def _dot_split(a, b):
    hi = a.astype(BF16)
    lo = (a - hi.astype(F32)).astype(BF16)
    return _dot(hi, b) + _dot(lo, b)


def _group_mean_sq(x):
    r = lax.broadcasted_iota(jnp.int32, (LANES, LANES), 0) // HEAD_DIM
    c = lax.broadcasted_iota(jnp.int32, (LANES, LANES), 1) // HEAD_DIM
    ones_bd = (r == c).astype(BF16)
    return _dot_split(x * x, ones_bd) * (1.0 / HEAD_DIM)


def _head_rms(x, gain):
    parts = []
    for k in range(x.shape[1] // LANES):
        xs = x[:, k * LANES:(k + 1) * LANES]
        parts.append(xs * lax.rsqrt(_group_mean_sq(xs) + RMS_EPS))
    y = parts[0] if len(parts) == 1 else jnp.concatenate(parts, axis=1)
    return y * gain


def _row_rms(x, gain):
    return x * lax.rsqrt(jnp.mean(x * x, axis=-1, keepdims=True) + RMS_EPS) * gain


def _params(*sem):
    return pltpu.CompilerParams(dimension_semantics=sem, vmem_limit_bytes=VMEM_LIMIT)


def _ffn_kernel(x_ref, g_ref, wg_ref, wu_ref, wd_ref, o_ref, h_scr, acc_scr):
    f = pl.program_id(1)

    @pl.when(f == 0)
    def _():
        h_scr[...] = _row_rms(x_ref[...], g_ref[...]).astype(BF16)
        acc_scr[...] = jnp.zeros_like(acc_scr)

    h = h_scr[...]
    a = _dot(h, wg_ref[...])
    b = _dot(h, wu_ref[...])
    act = (a * jax.nn.sigmoid(a) * b).astype(BF16)
    acc_scr[...] += _dot(act, wd_ref[...])

    @pl.when(f == pl.num_programs(1) - 1)
    def _():
        o_ref[...] = x_ref[...] + 0.5 * acc_scr[...]


def _ffn(x, gain, wg, wu, wd, tm):
    n, d = x.shape
    dff = wg.shape[1]
    tf = dff // 2
    return pl.pallas_call(
        _ffn_kernel,
        grid=(n // tm, dff // tf),
        in_specs=[
            pl.BlockSpec((tm, d), lambda i, f: (i, 0)),
            pl.BlockSpec((1, d), lambda i, f: (0, 0)),
            pl.BlockSpec((d, tf), lambda i, f: (0, f)),
            pl.BlockSpec((d, tf), lambda i, f: (0, f)),
            pl.BlockSpec((tf, d), lambda i, f: (f, 0)),
        ],
        out_specs=pl.BlockSpec((tm, d), lambda i, f: (i, 0)),
        out_shape=jax.ShapeDtypeStruct((n, d), F32),
        scratch_shapes=[pltpu.VMEM((tm, d), BF16), pltpu.VMEM((tm, d), F32)],
        compiler_params=_params("parallel", "arbitrary"),
        name="ffn",
    )(x, gain.reshape(1, d), wg, wu, wd)


C_Q = 0
C_KSEL = 512
C_KWIN = 640
C_KCMP = 768
C_VCMP = 896
C_VSEL = 1024
C_VWIN = 1152
C_U = 1280
C_CB = 1536
C_CC = 1792
C_CH = 2048
C_MG = 2304
C_NG = 5376
C_END = 5504
N_NORMED = 768


def _proj_kernel(x_ref, g_ref, w_ref, hg_ref, q_ref, kv_ref, win_ref, ksv_ref, u_ref, cb_ref, z_ref, mg_ref, ng_ref):
    h = _row_rms(x_ref[...], g_ref[...]).astype(BF16)

    def sec(lo, hi):
        return _dot(h, w_ref[:, lo:hi])

    nrm = _head_rms(sec(0, N_NORMED), hg_ref[...])
    ksel = nrm[:, C_KSEL:C_KSEL + KV_W]
    kwin = nrm[:, C_KWIN:C_KWIN + KV_W]
    q_ref[...] = (nrm[:, :NSA_W] * (HEAD_DIM ** -0.5)).astype(BF16)
    rest = sec(C_KCMP, C_U)
    vsel = rest[:, 2 * KV_W:3 * KV_W]
    vwin = rest[:, 3 * KV_W:4 * KV_W]
    kv_ref[:, 0:2 * KV_W] = rest[:, 0:2 * KV_W]
    kv_ref[:, 2 * KV_W:3 * KV_W] = ksel
    kv_ref[:, 3 * KV_W:4 * KV_W] = vsel
    win_ref[:, 0:KV_W] = kwin
    win_ref[:, KV_W:2 * KV_W] = vwin
    ksv_ref[:, 0:KV_W] = ksel.astype(BF16)
    ksv_ref[:, KV_W:2 * KV_W] = vsel.astype(BF16)
    ksv_ref[:, 2 * KV_W:3 * KV_W] = kwin.astype(BF16)
    ksv_ref[:, 3 * KV_W:4 * KV_W] = vwin.astype(BF16)
    u_ref[...] = sec(C_U, C_CB)
    cv = sec(C_CB, C_MG)
    cw = C_CC - C_CB
    cb_ref[...] = cv[:, 0:cw]
    z_ref[...] = cv[:, cw:2 * cw] * cv[:, 2 * cw:3 * cw]
    mg_ref[...] = sec(C_MG, C_NG)
    ng_ref[...] = sec(C_NG, C_END)


def _proj(x, gain, w, head_gain, tm):
    n, d = x.shape
    row = lambda w_: pl.BlockSpec((tm, w_), lambda i: (i, 0))
    full = lambda a: pl.BlockSpec(a.shape, lambda i: (0, 0))
    gain = gain.reshape(1, d)
    widths = ((NSA_W, BF16), (4 * KV_W, F32), (2 * KV_W, F32), (4 * KV_W, BF16), (C_CB - C_U, F32),
              (C_CC - C_CB, F32), (C_CC - C_CB, F32), (C_NG - C_MG, F32), (C_END - C_NG, F32))
    return pl.pallas_call(
        _proj_kernel,
        grid=(n // tm,),
        in_specs=[row(d), full(gain), full(w), full(head_gain)],
        out_specs=[row(w_) for w_, _ in widths],
        out_shape=[jax.ShapeDtypeStruct((n, w_), dt) for w_, dt in widths],
        compiler_params=_params("parallel"),
        name="proj",
    )(x, gain, w, head_gain)


def _compress_kernel(k_ref, v_ref, pe_ref, w1_ref, w2_ref, kg_ref, kc_ref, vc_ref, *, n_ch):
    outs = []
    for s, src_ref in enumerate((k_ref, v_ref)):
        first = jnp.zeros((n_ch, KV_W), F32)
        second = jnp.zeros((n_ch, KV_W), F32)
        for l in range(CMP_STRIDE):
            xl = src_ref[pl.ds(l, n_ch, stride=CMP_STRIDE), :]
            first += _dot((xl + pe_ref[s, l:l + 1, :]).astype(BF16), w1_ref[s, l])
            second += _dot((xl + pe_ref[s, CMP_STRIDE + l:CMP_STRIDE + l + 1, :]).astype(BF16),
                           w1_ref[s, CMP_STRIDE + l])
        hid = first + pltpu.roll(second, n_ch - 1, 0)
        act = hid * jax.nn.sigmoid(hid)
        outs.append(_dot(act.astype(BF16), w2_ref[s]))
    kc = _head_rms(outs[0], kg_ref[...])
    rows = kc_ref.shape[0]
    for ref, val in ((kc_ref, kc), (vc_ref, outs[1])):
        ref[0:CMP_PAD, :] = jnp.zeros((CMP_PAD, KV_W), F32)
        ref[CMP_PAD:CMP_PAD + n_ch, :] = val
        ref[CMP_PAD + n_ch:rows, :] = jnp.zeros((rows - CMP_PAD - n_ch, KV_W), F32)


def _cmp_rows(t):
    return -(-(CMP_PAD + t // CMP_STRIDE) // LANES) * LANES


def _compress_prompt(kv4, pe2, w1bd, w2bd, kgain, b, t):
    n_ch = t // CMP_STRIDE
    rows = _cmp_rows(t)
    full = lambda a: pl.BlockSpec(a.shape, lambda i: (0,) * a.ndim)
    out = jax.ShapeDtypeStruct((b, rows, KV_W), F32)
    return pl.pallas_call(
        functools.partial(_compress_kernel, n_ch=n_ch),
        grid=(b,),
        in_specs=[pl.BlockSpec((t, KV_W), lambda i: (i, 0)), pl.BlockSpec((t, KV_W), lambda i: (i, 1)),
                  full(pe2), full(w1bd), full(w2bd), full(kgain)],
        out_specs=[pl.BlockSpec((None, rows, KV_W), lambda i: (i, 0, 0))] * 2,
        out_shape=[out, out],
        compiler_params=_params("parallel"),
        name="compress_prompt",
    )(kv4, kv4, pe2, w1bd, w2bd, kgain)


def _lane_tiles(x):
    return [x[:, k * LANES:(k + 1) * LANES] for k in range(x.shape[1] // LANES)]


def _softmax_lane_tiles(tiles):
    m = functools.reduce(jnp.maximum, tiles)
    m = jnp.broadcast_to(jnp.max(m, axis=1, keepdims=True), m.shape)
    m = jnp.where(m == NEG_INF, 0.0, m)
    e = [jnp.exp(x - m) for x in tiles]
    den = functools.reduce(jnp.add, e)
    den = jnp.broadcast_to(jnp.sum(den, axis=1, keepdims=True), den.shape)
    return e, 1.0 / jnp.where(den > 0, den, 1.0)


def _top_k_mask(score, k):
    col = lax.broadcasted_iota(jnp.int32, score.shape, 1)
    sel = jnp.zeros(score.shape, F32)
    work = score
    for _ in range(k):
        m = jnp.max(work, axis=1, keepdims=True)
        first = jnp.min(jnp.where(work == m, col, score.shape[1]), axis=1, keepdims=True)
        pick = col == first
        sel = jnp.maximum(sel, pick.astype(F32))
        work = jnp.where(pick, NEG_INF, work)
    return sel


def _attn_prompt_kernel(q_ref, ng_ref, kc_ref, vc_ref, kall_ref, emat_ref, smap_ref, d01_ref, bc_ref, wtab_ref, o_ref,
                        m_scr, acc_scr, ob_scr):
    i = pl.program_id(1)
    lane = lax.broadcasted_iota(jnp.int32, (Q_BLOCK, LANES), 1)
    near0 = pl.multiple_of(i * (Q_BLOCK // CMP_STRIDE), 8)
    own = pl.multiple_of(i * Q_BLOCK, Q_BLOCK)
    scores, lhs_g = [], []

    for g in range(N_KV_HEADS):
        in_group = (lane >= HEAD_DIM) == (g == 1)
        lhs = jnp.concatenate(
            [jnp.where(in_group, q_ref[:, h * LANES:(h + 1) * LANES], jnp.zeros((), BF16)) for h in range(HPG)], axis=0)

        s_far = _dot_nt(lhs, kc_ref[...].astype(BF16))
        pcol = lax.broadcasted_iota(jnp.int32, s_far.shape, 1)
        s_far = jnp.where((pcol >= CMP_PAD) & (pcol < near0), s_far, NEG_INF)
        s_near = _dot_nt(lhs, kc_ref[pl.ds(near0, LANES), :].astype(BF16)) + bc_ref[g]
        ncol = lax.broadcasted_iota(jnp.int32, s_near.shape, 1)
        s_near = jnp.where(near0 + ncol >= CMP_PAD, s_near, NEG_INF)
        e_c, inv_c = _softmax_lane_tiles(_lane_tiles(s_far) + [s_near])
        p_far = jnp.concatenate([e * inv_c for e in e_c[:-1]], axis=1)
        p_near = e_c[-1] * inv_c
        ob_scr[g, 0] = (_dot(p_far.astype(BF16), vc_ref[...].astype(BF16))
                        + _dot(p_near.astype(BF16), vc_ref[pl.ds(near0, LANES), :].astype(BF16)))

        imp_far = sum(p_far[h * Q_BLOCK:(h + 1) * Q_BLOCK] for h in range(HPG))
        imp_near = sum(p_near[h * Q_BLOCK:(h + 1) * Q_BLOCK] for h in range(HPG))
        score = (_dot_split(imp_far, smap_ref[...].astype(BF16))
                 + _dot_split(imp_near, smap_ref[pl.ds(near0, LANES), :].astype(BF16)))
        jcol = lax.broadcasted_iota(jnp.int32, score.shape, 1)
        cur = 2 * i + (lax.broadcasted_iota(jnp.int32, score.shape, 0) >= SEL_BLOCK).astype(jnp.int32)
        forced = (jcol == 0) | (jcol == cur) | (jcol == cur - 1)
        score = jnp.where(forced, jnp.inf, score)
        scores.append(jnp.where(jcol <= cur, score, NEG_INF))
        lhs_g.append(lhs)

        s_w = _dot_nt(lhs, kall_ref[pl.ds(own, WINDOW + Q_BLOCK), 2 * KV_W:3 * KV_W]) + wtab_ref[g]
        wcol = lax.broadcasted_iota(jnp.int32, s_w.shape, 1)
        s_w = jnp.where(wcol >= WINDOW - own, s_w, NEG_INF)
        e_w, inv_w = _softmax_lane_tiles(_lane_tiles(s_w))
        ob_scr[g, 2] = _dot(jnp.concatenate(e_w, axis=1).astype(BF16),
                            kall_ref[pl.ds(own, WINDOW + Q_BLOCK), 3 * KV_W:4 * KV_W]) * inv_w

    sel_both = _top_k_mask(jnp.concatenate(scores, axis=0), TOP_K)
    n_plain = jnp.maximum(i - 1, 0)
    per_chunk = SEL_CHUNK // Q_BLOCK
    n_full = n_plain // per_chunk
    n_rest = n_plain % per_chunk
    ones_v = jnp.ones((2 * SEL_CHUNK, LANES), BF16)

    for g in range(N_KV_HEADS):
        lhs = lhs_g[g]
        sel = sel_both[g * Q_BLOCK:(g + 1) * Q_BLOCK]
        not_sel = jnp.concatenate([(sel - 1.0) * MASK_BIG] * HPG, axis=0).astype(BF16)
        lhs_sel = jnp.concatenate([lhs, not_sel], axis=1)

        def sel_logits(pos, width):
            keys = jnp.concatenate([kall_ref[pl.ds(WINDOW + pos, width), 0:KV_W],
                                    emat_ref[pl.ds(Q_BLOCK + pos, width), :]], axis=1)
            return _dot_nt(lhs_sel, keys)

        def sel_values(pos, width):
            return jnp.concatenate([kall_ref[pl.ds(WINDOW + pos, width), KV_W:2 * KV_W], ones_v[:width]], axis=1)

        def lane_tile_max(s):
            out = s[:, 0:LANES]
            for k in range(1, s.shape[1] // LANES):
                out = jnp.maximum(out, s[:, k * LANES:(k + 1) * LANES])
            return out

        def weights(s):
            mb = m_scr[...]
            return jnp.concatenate([jnp.exp(s[:, k * LANES:(k + 1) * LANES] - mb)
                                    for k in range(s.shape[1] // LANES)], axis=1).astype(BF16)

        def rest_logits():
            s = sel_logits(pl.multiple_of(n_full * SEL_CHUNK, SEL_CHUNK), SEL_CHUNK)
            col = lax.broadcasted_iota(jnp.int32, s.shape, 1)
            return jnp.where(col < n_rest * Q_BLOCK, s, NEG_INF)

        near = pl.multiple_of(own - Q_BLOCK, Q_BLOCK)
        s_near2 = sel_logits(near, 2 * Q_BLOCK) + d01_ref[g]
        ncol2 = lax.broadcasted_iota(jnp.int32, s_near2.shape, 1)
        s_near2 = jnp.where((i == 0) & (ncol2 < Q_BLOCK), NEG_INF, s_near2)

        def over_plain_keys(fn):
            def pair_body(c, carry):
                pos = pl.multiple_of(c * 2 * SEL_CHUNK, 2 * SEL_CHUNK)
                fn(sel_logits(pos, 2 * SEL_CHUNK), pos, 2 * SEL_CHUNK)
                return carry

            lax.fori_loop(0, n_full // 2, pair_body, 0)
            last = pl.multiple_of(n_full * SEL_CHUNK, SEL_CHUNK)

            @pl.when(n_full % 2 == 1)
            def _():
                odd = pl.multiple_of((n_full - 1) * SEL_CHUNK, SEL_CHUNK)
                fn(sel_logits(odd, SEL_CHUNK), odd, SEL_CHUNK)

            @pl.when(n_rest > 0)
            def _():
                fn(rest_logits(), last, SEL_CHUNK)

        def take_max(s, pos, width):
            m_scr[...] = jnp.maximum(m_scr[...], lane_tile_max(s))

        def accumulate(s, pos, width):
            acc_scr[...] += _dot(weights(s), sel_values(pos, width))

        m_scr[...] = lane_tile_max(s_near2)
        over_plain_keys(take_max)
        m_scr[...] = jnp.broadcast_to(jnp.max(m_scr[...], axis=1, keepdims=True), m_scr.shape)
        acc_scr[...] = _dot(weights(s_near2), sel_values(near, 2 * Q_BLOCK))
        over_plain_keys(accumulate)
        ob_scr[g, 1] = acc_scr[:, 0:LANES] / acc_scr[:, LANES:2 * LANES]

    sig = jax.nn.sigmoid(ng_ref[...])
    for h in range(HPG):
        slab = None
        for j in range(3):
            part = []
            for g in range(N_KV_HEADS):
                c = j * N_Q_HEADS + g * HPG + h
                part.append(sig[:, c:c + 1] * ob_scr[g, j, h * Q_BLOCK:(h + 1) * Q_BLOCK, :])
            term = jnp.where(lane < HEAD_DIM, part[0], part[1])
            slab = term if slab is None else slab + term
        o_ref[:, h * LANES:(h + 1) * LANES] = slab.astype(BF16)


def _attn_prompt(q, ng, kc, vc, kall, emat, smap, d01, bc, wtab, b, t):
    nq = t // Q_BLOCK
    rows = HPG * Q_BLOCK
    full = lambda a: pl.BlockSpec(a.shape, lambda bi, i: (0,) * a.ndim)
    per_batch = lambda a: pl.BlockSpec((None,) + a.shape[1:], lambda bi, i: (bi, 0, 0))
    return pl.pallas_call(
        _attn_prompt_kernel,
        grid=(b, nq),
        in_specs=[
            pl.BlockSpec((Q_BLOCK, NSA_W), lambda bi, i: (bi * nq + i, 0)),
            pl.BlockSpec((Q_BLOCK, LANES), lambda bi, i: (bi * nq + i, 0)),
            per_batch(kc), per_batch(vc), per_batch(kall),
            full(emat), full(smap), full(d01), full(bc), full(wtab),
        ],
        out_specs=pl.BlockSpec((Q_BLOCK, NSA_W), lambda bi, i: (bi * nq + i, 0)),
        out_shape=jax.ShapeDtypeStruct((b * t, NSA_W), BF16),
        scratch_shapes=[pltpu.VMEM((rows, LANES), F32), pltpu.VMEM((rows, 2 * LANES), F32),
                        pltpu.VMEM((N_KV_HEADS, 3, rows, LANES), F32)],
        compiler_params=_params("parallel", "arbitrary"),
        name="attn_prompt",
    )(q, ng, kc, vc, kall, emat, smap, d01, bc, wtab)


def _pool_rows(ext_scr, base, u, pos):
    m, width = u.shape
    lane = lax.broadcasted_iota(jnp.int32, (m, width), 1)
    gc = width // len(POOL_WINDOWS)
    run = u
    pooled = None
    for k in range(1, max(POOL_WINDOWS)):
        run = run + ext_scr[base - k:base - k + m, :]
        w = k + 1
        if w in POOL_WINDOWS:
            mean = run / jnp.minimum(pos + 1, w).astype(F32)
            pooled = mean if pooled is None else jnp.where(lane >= POOL_WINDOWS.index(w) * gc, mean, pooled)
    return pooled - u


def _conv_rows(ext_scr, base, m, cw_ref):
    y = None
    for j in range(CONV_K):
        off = base - (CONV_K - 1) + j
        term = cw_ref[j:j + 1, :] * ext_scr[off:off + m, :]
        y = term if y is None else y + term
    return y


def _mix_prompt_kernel(u_ref, uh_ref, z_ref, zh_ref, cb_ref, cw_ref, d_ref, c_ref, ext_scr, *, tm):
    i = pl.program_id(1)
    has_hist = (i > 0).astype(F32)
    pos = i * tm + lax.broadcasted_iota(jnp.int32, u_ref.shape, 0)
    ext_scr[0:HALO, :] = uh_ref[...] * has_hist
    ext_scr[HALO:HALO + tm, :] = u_ref[...]
    d_ref[...] = _pool_rows(ext_scr, HALO, u_ref[...], pos).astype(BF16)
    ext_scr[0:HALO, :] = zh_ref[...] * has_hist
    ext_scr[HALO:HALO + tm, :] = z_ref[...]
    c_ref[...] = (cb_ref[...] * _conv_rows(ext_scr, HALO, tm, cw_ref)).astype(BF16)


def _mix_prompt(u, z, cb, conv_w, b, t, tm):
    nt = t // tm
    width = u.shape[1]
    row = pl.BlockSpec((tm, width), lambda bi, i: (bi * nt + i, 0))
    halo = pl.BlockSpec((HALO, width), lambda bi, i: (jnp.maximum((bi * nt + i) * (tm // HALO) - 1, 0), 0))
    out = jax.ShapeDtypeStruct((b * t, width), BF16)
    return pl.pallas_call(
        functools.partial(_mix_prompt_kernel, tm=tm),
        grid=(b, nt),
        in_specs=[row, halo, row, halo, row, pl.BlockSpec(conv_w.shape, lambda bi, i: (0, 0))],
        out_specs=[row, row],
        out_shape=[out, out],
        scratch_shapes=[pltpu.VMEM((HALO + tm, width), F32)],
        compiler_params=_params("parallel", "parallel"),
        name="mix_prompt",
    )(u, u, z, z, cb, conv_w)


def _merge_kernel(x_ref, a_ref, d_ref, c_ref, mg_ref, wa_ref, wp_ref, ps_ref, wc_ref, wo_ref, o_ref):
    dm = x_ref.shape[1]
    gates = jax.nn.sigmoid(mg_ref[...])
    m = gates[:, 0:dm] * _dot(a_ref[...], wa_ref[...])
    m = m + gates[:, dm:2 * dm] * (_dot(d_ref[...], wp_ref[...]) * ps_ref[...])
    m = m + gates[:, 2 * dm:3 * dm] * _dot(c_ref[...], wc_ref[...])
    o_ref[...] = x_ref[...] + _dot(m.astype(BF16), wo_ref[...])


def _merge(x, attn, d, c, mg, wa, wp, ps, wc, wo, tm):
    n, dm = x.shape
    row = lambda a: pl.BlockSpec((tm, a.shape[1]), lambda i: (i, 0))
    full = lambda a: pl.BlockSpec(a.shape, lambda i: (0, 0))
    ps = ps.reshape(1, dm)
    return pl.pallas_call(
        _merge_kernel,
        grid=(n // tm,),
        in_specs=[row(x), row(attn), row(d), row(c), row(mg), full(wa), full(wp), full(ps), full(wc), full(wo)],
        out_specs=row(x),
        out_shape=jax.ShapeDtypeStruct((n, dm), F32),
        compiler_params=_params("parallel"),
        name="merge",
    )(x, attn, d, c, mg, wa, wp, ps, wc, wo)


SAMPLE_ROWS = 8
SAMPLE_NB = 4


def _cmp_partial_kernel(xt_ref, pe_ref, w1_ref, o_ref, x_scr, *, n_ch):
    page = xt_ref.shape[2]
    for p in range(xt_ref.shape[0]):
        x_scr[p * page:(p + 1) * page, :] = xt_ref[p].T
    first = jnp.zeros((n_ch, KV_W), F32)
    second = jnp.zeros((n_ch, KV_W), F32)
    for l in range(CMP_STRIDE):
        xl = x_scr[pl.ds(l, n_ch, stride=CMP_STRIDE), :]
        first += _dot((xl + pe_ref[l:l + 1, :]).astype(BF16), w1_ref[l])
        second += _dot((xl + pe_ref[CMP_STRIDE + l:CMP_STRIDE + l + 1, :]).astype(BF16), w1_ref[CMP_STRIDE + l])
    o_ref[:, 0:KV_W] = first
    o_ref[:, KV_W:2 * KV_W] = second


def _cmp_partial(cache_t, pe2, w1bd, layer, n_pool):
    page = cache_t.shape[2]
    ch = page // CMP_STRIDE
    pp = next(p for p in (64, 32, 16, 8, 4, 2, 1) if n_pool % p == 0)
    nb = n_pool // pp
    return pl.pallas_call(
        functools.partial(_cmp_partial_kernel, n_ch=pp * ch),
        grid=(nb, 2),
        in_specs=[pl.BlockSpec((pp, KV_W, page), lambda i, s: (layer * nb + i, s, 0)),
                  pl.BlockSpec((None,) + pe2.shape[1:], lambda i, s: (s, 0, 0)),
                  pl.BlockSpec((None,) + w1bd.shape[1:], lambda i, s: (s, 0, 0, 0))],
        out_specs=pl.BlockSpec((pp * ch, 2 * KV_W), lambda i, s: (i, s)),
        out_shape=jax.ShapeDtypeStruct((n_pool * ch, 4 * KV_W), F32),
        scratch_shapes=[pltpu.VMEM((pp * page, KV_W), F32)],
        compiler_params=_params("parallel", "arbitrary"),
        name="cmp_partial",
    )(cache_t, pe2, w1bd)


def _softmax_pair(s_a, vt_a, s_b, v_b):
    m = jnp.maximum(jnp.max(s_a, axis=1, keepdims=True), jnp.max(s_b, axis=1, keepdims=True))
    e_a = jnp.exp(s_a - m)
    e_b = jnp.exp(s_b - m)
    den = jnp.sum(e_a, axis=1, keepdims=True) + jnp.sum(e_b, axis=1, keepdims=True)
    return (_dot_nt(e_a.astype(BF16), vt_a) + _dot(e_b.astype(BF16), v_b)) / den


def _sample_kernel(pt_ref, q_ref, ng_ref, kv_ref, win_ref, wnt_ref, u_ref, z_ref, cb_ref, swin_ref, spool_ref,
                   sconv_ref, cache_ref, fs_ref, w2_ref, kg_ref, cw_ref, smap_ref, emat_ref, bc_ref, dsel_ref, dwin_ref,
                   dnew_ref, attn_ref, d_ref, c_ref, owin_ref, opool_ref, oconv_ref,
                   kvbuf, fsbuf, sems, ext_scr, *, layer_base, n_pages, page, s_len, past):
    i = pl.program_id(0)
    slot = i % 2
    ch = page // CMP_STRIDE
    sr = SAMPLE_ROWS

    def copies(step, slot_):
        out = []
        for sb in range(SAMPLE_NB):
            for pg in range(n_pages):
                pid = pt_ref[step * SAMPLE_NB + sb, pg]
                out.append(pltpu.make_async_copy(
                    cache_ref.at[layer_base + pid, pl.ds(2 * KV_W, 2 * KV_W), :],
                    kvbuf.at[slot_, sb, :, pl.ds(pg * page, page)], sems.at[slot_, 0]))
                out.append(pltpu.make_async_copy(
                    fs_ref.at[pid], fsbuf.at[slot_, sb, pl.ds(pg * ch, ch), :], sems.at[slot_, 1]))
        return out

    @pl.when(i == 0)
    def _():
        for c in copies(0, 0):
            c.start()

    @pl.when(i + 1 < pl.num_programs(0))
    def _():
        for c in copies(i + 1, 1 - slot):
            c.start()

    for c in copies(i, slot):
        c.wait()

    lane = lax.broadcasted_iota(jnp.int32, (sr, LANES), 1)
    grp_rows = N_KV_HEADS * sr
    q_all = q_ref[...].astype(F32)
    sig_all = jax.nn.sigmoid(ng_ref[...])
    kvn = kv_ref[...]
    winn = win_ref[...]
    pad_rows = jnp.zeros((LANES - sr, KV_W), F32)
    w_keep = swin_ref.shape[2]
    wlane = lax.broadcasted_iota(jnp.int32, swin_ref.shape[1:], 1)
    hist = spool_ref.shape[1]
    ck = sconv_ref.shape[1]
    attn_rows, d_rows, c_rows = [], [], []

    every = range(SAMPLE_NB)

    def new_rows(x8):
        return jnp.concatenate([x8, pad_rows], axis=0).astype(BF16)

    lhs, ks_new, vs_new, kw_new, vw_new = [], [], [], [], []
    for sb in every:
        r0 = sb * sr
        q8 = q_all[r0:r0 + sr]
        pieces = []
        for h in range(HPG):
            for g in range(N_KV_HEADS):
                pieces.append(jnp.where((lane >= HEAD_DIM) == (g == 1), q8[:, h * LANES:(h + 1) * LANES], 0.0))
        lhs.append(jnp.concatenate(pieces, axis=0).astype(BF16))
        ks_new.append(new_rows(kvn[r0:r0 + sr, 2 * KV_W:3 * KV_W]))
        vs_new.append(new_rows(kvn[r0:r0 + sr, 3 * KV_W:4 * KV_W]))
        kw_new.append(new_rows(winn[r0:r0 + sr, 0:KV_W]))
        vw_new.append(new_rows(winn[r0:r0 + sr, KV_W:2 * KV_W]))

    n_ch = fsbuf.shape[2]
    cmp = []
    for s_ in range(2):
        hid = jnp.concatenate(
            [fsbuf[slot, sb, :, 2 * s_ * KV_W:(2 * s_ + 1) * KV_W]
             + pltpu.roll(fsbuf[slot, sb, :, (2 * s_ + 1) * KV_W:(2 * s_ + 2) * KV_W], n_ch - 1, 0) for sb in every],
            axis=0)
        cmp.append(_dot((hid * jax.nn.sigmoid(hid)).astype(BF16), w2_ref[s_]))
    kc_all = _head_rms(cmp[0], kg_ref[...]).astype(BF16)
    vc_all = cmp[1].astype(BF16)
    p_c, o_c = [], []
    for sb in every:
        s_c = _dot_nt(lhs[sb], kc_all[sb * n_ch:(sb + 1) * n_ch]) + bc_ref[...]
        e_c = jnp.exp(s_c - jnp.max(s_c, axis=1, keepdims=True))
        p_c.append(e_c / jnp.sum(e_c, axis=1, keepdims=True))
        o_c.append(_dot(p_c[sb].astype(BF16), vc_all[sb * n_ch:(sb + 1) * n_ch]))

    o_w = [_softmax_pair(_dot(lhs[sb], swin_ref[sb, 0:KV_W, :].astype(BF16)) + dwin_ref[...],
                         swin_ref[sb, KV_W:2 * KV_W, :].astype(BF16),
                         _dot_nt(lhs[sb], kw_new[sb]) + dnew_ref[...], vw_new[sb]) for sb in every]

    imp = jnp.concatenate([sum(p_c[sb][h * grp_rows:(h + 1) * grp_rows] for h in range(HPG)) for sb in every], axis=0)
    score = _dot_split(imp, smap_ref[...])
    jcol = lax.broadcasted_iota(jnp.int32, score.shape, 1)
    cur = (past + lax.broadcasted_iota(jnp.int32, score.shape, 0) % sr) // SEL_BLOCK
    forced = (jcol == 0) | (jcol == cur) | (jcol == cur - 1)
    score = jnp.where(forced, jnp.inf, score)
    score = jnp.where(jcol <= cur, score, NEG_INF)
    not_sel_all = (_top_k_mask(score, TOP_K) - 1.0) * MASK_BIG

    o_s = []
    for sb in every:
        not_sel = jnp.concatenate([not_sel_all[sb * grp_rows:(sb + 1) * grp_rows]] * HPG, axis=0).astype(BF16)
        lhs_sel = jnp.concatenate([lhs[sb], not_sel], axis=1)
        k_past = jnp.concatenate([kvbuf[slot, sb, 0:KV_W, :].astype(BF16), emat_ref[...]], axis=0)
        o_s.append(_softmax_pair(_dot(lhs_sel, k_past) + dsel_ref[...],
                                 kvbuf[slot, sb, KV_W:2 * KV_W, :].astype(BF16),
                                 _dot_nt(lhs[sb], ks_new[sb]) + dnew_ref[...], vs_new[sb]))

    for sb in every:
        r0 = sb * sr
        branch = (o_c[sb], o_s[sb], o_w[sb])
        sig8 = sig_all[r0:r0 + sr]
        slabs = []
        for h in range(HPG):
            slab = None
            for j in range(3):
                part = []
                for g in range(N_KV_HEADS):
                    c = j * N_Q_HEADS + g * HPG + h
                    lo = h * grp_rows + g * sr
                    part.append(sig8[:, c:c + 1] * branch[j][lo:lo + sr])
                term = jnp.where(lane < HEAD_DIM, part[0], part[1])
                slab = term if slab is None else slab + term
            slabs.append(slab)
        attn_rows.append(jnp.concatenate(slabs, axis=1))

        state = pltpu.roll(swin_ref[sb], w_keep - s_len, 1)
        for tt in range(s_len):
            state = jnp.where(wlane == w_keep - s_len + tt, wnt_ref[sb, :, tt:tt + 1], state)
        owin_ref[sb] = state

        u8 = u_ref[r0:r0 + sr, :]
        ext_scr[0:hist, :] = spool_ref[sb]
        ext_scr[hist:hist + sr, :] = u8
        d_rows.append(_pool_rows(ext_scr, hist, u8, hist + lax.broadcasted_iota(jnp.int32, u8.shape, 0)))
        opool_ref[sb] = ext_scr[s_len:s_len + hist, :]
        ext_scr[0:ck, :] = sconv_ref[sb]
        ext_scr[ck:ck + sr, :] = z_ref[r0:r0 + sr, :]
        c_rows.append(cb_ref[r0:r0 + sr, :] * _conv_rows(ext_scr, ck, sr, cw_ref))
        oconv_ref[sb] = ext_scr[s_len:s_len + ck, :]

    attn_ref[...] = jnp.concatenate(attn_rows, axis=0).astype(BF16)
    d_ref[...] = jnp.concatenate(d_rows, axis=0).astype(BF16)
    c_ref[...] = jnp.concatenate(c_rows, axis=0).astype(BF16)


def _sample_layer(layer, page_table, q, ng, kv4, win, wnt, u, z, cb, swin3, spool3, sconv3, cache3, fs3, w2bd, kgain,
                  conv_w, smap_s, emat, bc_s, dsel, dwin, dnew, n_p, db, s_len, n_pool):
    nsteps = db // SAMPLE_NB
    rb = SAMPLE_NB * SAMPLE_ROWS
    r0 = n_p // rb
    page = cache3.shape[2]
    n_pages = page_table.shape[1]
    past = n_pages * page
    tok = lambda a: pl.BlockSpec((rb, a.shape[1]), lambda i, pt: (r0 + i, 0))
    new_t = pl.BlockSpec((SAMPLE_NB,) + wnt.shape[1:], lambda i, pt: (i, 0, 0))
    state = lambda a: pl.BlockSpec((SAMPLE_NB,) + a.shape[1:], lambda i, pt: (layer * nsteps + i, 0, 0))
    full = lambda a: pl.BlockSpec(a.shape, lambda i, pt: (0,) * a.ndim)
    hbm = pl.BlockSpec(memory_space=pl.ANY)
    out_tok = lambda w_: pl.BlockSpec((rb, w_), lambda i, pt: (i, 0))
    out_state = lambda a: pl.BlockSpec((SAMPLE_NB,) + a.shape[1:], lambda i, pt: (i, 0, 0))
    width = u.shape[1]
    n_tok = db * SAMPLE_ROWS
    return pl.pallas_call(
        functools.partial(_sample_kernel, layer_base=layer * n_pool, n_pages=n_pages, page=page, s_len=s_len, past=past),
        grid_spec=pltpu.PrefetchScalarGridSpec(
            num_scalar_prefetch=1,
            grid=(nsteps,),
            in_specs=[tok(q), tok(ng), tok(kv4), tok(win), new_t, tok(u), tok(z), tok(cb), state(swin3), state(spool3),
                      state(sconv3), hbm, hbm, full(w2bd), full(kgain), full(conv_w), full(smap_s), full(emat),
                      full(bc_s), full(dsel), full(dwin), full(dnew)],
            out_specs=[out_tok(NSA_W), out_tok(width), out_tok(width), out_state(swin3), out_state(spool3),
                       out_state(sconv3)],
            scratch_shapes=[pltpu.VMEM((2, SAMPLE_NB, 2 * KV_W, past), F32),
                            pltpu.VMEM((2, SAMPLE_NB, past // CMP_STRIDE, 4 * KV_W), F32),
                            pltpu.SemaphoreType.DMA((2, 2)),
                            pltpu.VMEM((4 * SAMPLE_ROWS, width), F32)]),
        out_shape=[jax.ShapeDtypeStruct((n_tok, NSA_W), BF16), jax.ShapeDtypeStruct((n_tok, width), BF16),
                   jax.ShapeDtypeStruct((n_tok, width), BF16), jax.ShapeDtypeStruct((db,) + swin3.shape[1:], F32),
                   jax.ShapeDtypeStruct((db,) + spool3.shape[1:], F32),
                   jax.ShapeDtypeStruct((db,) + sconv3.shape[1:], F32)],
        compiler_params=_params("arbitrary"),
        name="sample_mix",
    )(page_table, q, ng, kv4, win, wnt, u, z, cb, swin3, spool3, sconv3, cache3, fs3, w2bd, kgain, conv_w, smap_s,
      emat, bc_s, dsel, dwin, dnew)


def _bucket_table(n):
    d = np.arange(n)
    nf = np.maximum(d, 1).astype(np.float32)
    large = MAX_EXACT + (np.log(nf / np.float32(MAX_EXACT)) / np.float32(math.log(MAX_DISTANCE / MAX_EXACT))
                         * np.float32(N_BUCKETS - MAX_EXACT)).astype(np.int32)
    return np.where(d < MAX_EXACT, d, np.minimum(large, N_BUCKETS - 1)).astype(np.int32)


def _bias_values(rel_bias, dist, visible=None):
    visible = (dist >= 0) if visible is None else (visible & (dist >= 0))
    bkt = _bucket_table(max(int(dist.max()) + 1, MAX_DISTANCE))[np.clip(dist, 0, None)]
    tab = rel_bias - rel_bias[N_BUCKETS - 1:N_BUCKETS]
    vals = jnp.where(jnp.asarray(visible)[..., None], tab[jnp.asarray(bkt)], NEG_INF)
    return vals.transpose(2, 0, 1).reshape((N_KV_HEADS, HPG) + dist.shape)


def _bias_table(rel_bias, dist):
    r, c = dist.shape
    return _bias_values(rel_bias, dist).reshape(N_KV_HEADS, HPG * r, c)


def _bias_table_sample(rel_bias, dist, visible=None):
    r, c = dist.shape
    return _bias_values(rel_bias, dist, visible).transpose(1, 0, 2, 3).reshape(HPG * N_KV_HEADS * r, c)


def _sel_map(n_cmp, n_rows, n_cols, pad):
    r_sel = SEL_BLOCK // CMP_STRIDE
    r_cmp = CMP_BLOCK // CMP_STRIDE
    out = np.zeros((n_rows, n_cols), np.float32)
    for j in range(n_cols):
        for m in range(r_sel):
            for n in range(r_cmp):
                c = r_sel * j + m - n
                if 0 <= c < n_cmp:
                    out[pad + c, j] += 1.0
    return out


def _slab_order(w, axis):
    shape = w.shape
    w = w.reshape(shape[:axis] + (N_KV_HEADS, HPG, HEAD_DIM) + shape[axis + 1:])
    return jnp.swapaxes(w, axis, axis + 1).reshape(shape)


def _proj_weight(w_in):
    kv0, ng0 = NSA_W, NSA_W + 6 * KV_W
    rest0 = ng0 + 3 * N_Q_HEADS
    slot = lambda s: w_in[:, :, kv0 + s * KV_W:kv0 + (s + 1) * KV_W]
    ng = w_in[:, :, ng0:rest0].reshape(w_in.shape[:2] + (N_Q_HEADS, 3))
    ng = jnp.swapaxes(ng, 2, 3).reshape(w_in.shape[:2] + (3 * N_Q_HEADS,))
    ng = jnp.pad(ng, ((0, 0), (0, 0), (0, C_END - C_NG - 3 * N_Q_HEADS)))
    parts = [_slab_order(w_in[:, :, :NSA_W], 2), slot(2), slot(4), slot(0), slot(1), slot(3), slot(5),
             w_in[:, :, rest0:], ng]
    out = jnp.concatenate(parts, axis=2)
    assert out.shape[2] == C_END
    return out


def _block_diag2(w):
    z = jnp.zeros_like(w)
    return jnp.concatenate([jnp.concatenate([w, z], axis=-1), jnp.concatenate([z, w], axis=-1)], axis=-2)


def _token_tile(n):
    for tm in (512, 256, 128):
        if n % tm == 0:
            return tm
    raise ValueError(f"token count {n} is not a multiple of 128")


def kernel(x_prompt, x_sample, cache_kv, state_win, state_pool, state_conv, page_table, rel_bias, ln_ffn1, w_ffn1_gate, w_ffn1_up, w_ffn1_down, ln_mix, w_in, q_norm, k_norm, cmp_pe, cmp_w1, cmp_w2, w_nsa_out, w_pool, pool_scale, conv_w, w_conv_out, w_o, ln_ffn2, w_ffn2_gate, w_ffn2_up, w_ffn2_down):
    b, t, dm = x_prompt.shape
    db, s, _ = x_sample.shape
    depth, n_pool, page = cache_kv.shape[:3]
    n_pages = page_table.shape[1]
    past = n_pages * page
    w_keep = state_win.shape[2]
    sr = SAMPLE_ROWS
    n_p, n_s = b * t, db * sr
    assert t % Q_BLOCK == 0 and t >= WINDOW and t // SEL_BLOCK <= LANES
    assert s <= sr and db % SAMPLE_NB == 0 and n_p % (SAMPLE_NB * sr) == 0
    assert (past + s) // CMP_STRIDE == past // CMP_STRIDE == LANES and w_keep <= past and s <= w_keep
    assert -(-(past + s) // SEL_BLOCK) <= LANES
    tm = _token_tile(n_p + n_s)
    tm_proj = min(tm, 256)
    tm_mix = _token_tile(t)

    w_proj = _proj_weight(w_in).astype(BF16)
    head_gain = jnp.concatenate([jnp.tile(q_norm, (1, N_Q_HEADS)), jnp.tile(k_norm[:, 1], (1, N_KV_HEADS)),
                                 jnp.tile(k_norm[:, 2], (1, N_KV_HEADS))], axis=1).reshape(depth, 1, N_NORMED)
    cmp_gain = jnp.tile(k_norm[:, 0], (1, N_KV_HEADS)).reshape(depth, 1, KV_W)
    pe2 = jnp.tile(cmp_pe, (1, 1, 1, N_KV_HEADS))
    w1bd = _block_diag2(cmp_w1).astype(BF16)
    w2bd = _block_diag2(cmp_w2).astype(BF16)
    wa = _slab_order(w_nsa_out, axis=1).astype(BF16)
    n_grp, gc, ge = w_pool.shape[1:]
    wp = jnp.zeros((depth, n_grp * gc, n_grp * ge), F32)
    for gi in range(n_grp):
        wp = wp.at[:, gi * gc:(gi + 1) * gc, gi * ge:(gi + 1) * ge].set(w_pool[:, gi])
    wp = wp.astype(BF16)
    wc = w_conv_out.astype(BF16)
    wo = w_o.astype(BF16)
    ffn1 = [w.astype(BF16) for w in (w_ffn1_gate, w_ffn1_up, w_ffn1_down)]
    ffn2 = [w.astype(BF16) for w in (w_ffn2_gate, w_ffn2_up, w_ffn2_down)]

    ti = np.arange(Q_BLOCK)
    d01 = _bias_table(rel_bias, Q_BLOCK + ti[:, None] - np.arange(2 * Q_BLOCK)[None, :])
    near_c = np.arange(LANES)
    bc = _bias_table(rel_bias, ti[:, None] - CMP_STRIDE * (near_c[None, :] - CMP_PAD) - (CMP_BLOCK - 1))
    n_cmp = t // CMP_STRIDE - CMP_BLOCK // CMP_STRIDE + 1
    smap = jnp.asarray(_sel_map(n_cmp, _cmp_rows(t), LANES, CMP_PAD))
    wj = np.arange(WINDOW + Q_BLOCK)[None, :]
    dist_pw = WINDOW + ti[:, None] - wj
    wtab = _bias_values(rel_bias, dist_pw, dist_pw < WINDOW).reshape(N_KV_HEADS, HPG * Q_BLOCK, WINDOW + Q_BLOCK)
    key_pos = np.arange(-Q_BLOCK, t + SEL_CHUNK)[:, None]
    emat_p = jnp.asarray((key_pos >= 0) & (key_pos < t)
                         & (key_pos // SEL_BLOCK == np.arange(LANES)[None, :])).astype(BF16)

    tq = np.arange(sr)[:, None]
    n_chunks = past // CMP_STRIDE
    n_cmp_s = n_chunks - CMP_BLOCK // CMP_STRIDE + 1
    cn = np.arange(n_chunks)[None, :]
    bc_s = _bias_table_sample(rel_bias, past + tq - (CMP_STRIDE * cn + CMP_BLOCK - 1), cn < n_cmp_s)
    dsel = _bias_table_sample(rel_bias, past + tq - np.arange(past)[None, :])
    tn = np.arange(LANES)[None, :]
    dnew = _bias_table_sample(rel_bias, tq - tn, tn < s)
    dist_w = w_keep + tq - np.arange(w_keep)[None, :]
    dwin = _bias_table_sample(rel_bias, dist_w, dist_w < WINDOW)
    smap_s = jnp.asarray(_sel_map(n_cmp_s, n_chunks, LANES, 0)).astype(BF16)
    emat = jnp.asarray(np.arange(LANES)[:, None] == np.arange(past)[None, :] // SEL_BLOCK).astype(BF16)

    cache3 = cache_kv.transpose(0, 1, 3, 4, 5, 2).reshape(depth * n_pool, -1, page)
    swin3 = state_win.transpose(0, 1, 3, 4, 5, 2).reshape(depth * db, -1, w_keep)
    spool3 = state_pool.reshape((depth * db,) + state_pool.shape[2:])
    sconv3 = state_conv.reshape((depth * db,) + state_conv.shape[2:])

    xs = jnp.pad(x_sample, ((0, 0), (0, sr - s), (0, 0)))
    x = jnp.concatenate([x_prompt.reshape(n_p, dm), xs.reshape(n_s, dm)], axis=0)
    outs = [[] for _ in range(8)]
    for l in range(depth):
        x = _ffn(x, ln_ffn1[l], ffn1[0][l], ffn1[1][l], ffn1[2][l], tm)
        q, kv4, win, ksv, u, cb, z, mg, ng = _proj(x, ln_mix[l], w_proj[l], head_gain[l], tm_proj)

        kc, vc = _compress_prompt(kv4, pe2[l], w1bd[l], w2bd[l], cmp_gain[l], b, t)
        kall = jnp.pad(ksv[:n_p].reshape(b, t, 4 * KV_W), ((0, 0), (WINDOW, SEL_CHUNK), (0, 0)))
        attn_p = _attn_prompt(q, ng, kc, vc, kall, emat_p, smap, d01, bc, wtab, b, t)
        d_p, c_p = _mix_prompt(u, z, cb, conv_w[l], b, t, tm_mix)

        fs3 = _cmp_partial(cache3, pe2[l], w1bd[l], l, n_pool).reshape(n_pool, page // CMP_STRIDE, -1)
        wnt = win[n_p:].reshape(db, sr, -1).transpose(0, 2, 1)
        attn_s, d_s, c_s, s_win, s_pool, s_conv = _sample_layer(
            l, page_table, q, ng, kv4, win, wnt, u, z, cb, swin3, spool3, sconv3, cache3, fs3, w2bd[l], cmp_gain[l],
            conv_w[l], smap_s, emat, bc_s, dsel, dwin, dnew, n_p, db, s, n_pool)

        x = _merge(x, jnp.concatenate([attn_p, attn_s]), jnp.concatenate([d_p, d_s]), jnp.concatenate([c_p, c_s]),
                   mg, wa[l], wp[l], pool_scale[l], wc[l], wo[l], tm)
        x = _ffn(x, ln_ffn2[l], ffn2[0][l], ffn2[1][l], ffn2[2][l], tm)

        p_keep = min(WINDOW, t)
        outs[0].append(kv4[:n_p].reshape(b, t, 4, N_KV_HEADS, HEAD_DIM))
        outs[1].append(win[:n_p].reshape(b, t, 2, N_KV_HEADS, HEAD_DIM)[:, t - p_keep:])
        outs[2].append(u[:n_p].reshape(b, t, -1)[:, t - POOL_STATE:])
        outs[3].append(z[:n_p].reshape(b, t, -1)[:, t - (CONV_K - 1):])
        outs[4].append(kv4[n_p:].reshape(db, sr, 4, N_KV_HEADS, HEAD_DIM)[:, :s])
        outs[5].append(s_win.reshape(db, 2, N_KV_HEADS, HEAD_DIM, w_keep).transpose(0, 4, 1, 2, 3))
        outs[6].append(s_pool)
        outs[7].append(s_conv)

    y_prompt = x[:n_p].reshape(b, t, dm)
    y_sample = x[n_p:].reshape(db, sr, dm)[:, :s]
    return (y_prompt, y_sample) + tuple(jnp.stack(o) for o in outs)
```

```python
import functools
import math

import numpy as np
import jax
import jax.numpy as jnp
from jax import lax
from jax.experimental import pallas as pl
from jax.experimental.pallas import tpu as pltpu

F32 = jnp.float32
BF16 = jnp.bfloat16
NEG_INF = float("-inf")

HEAD_DIM = 64
N_KV_HEADS = 2
HPG = 4
N_Q_HEADS = N_KV_HEADS * HPG
NSA_W = N_Q_HEADS * HEAD_DIM
KV_W = N_KV_HEADS * HEAD_DIM
CMP_BLOCK = 32
CMP_STRIDE = 16
SEL_BLOCK = 64
TOP_K = 8
WINDOW = 512
Q_BLOCK = 128
N_BUCKETS = 32
MAX_EXACT = 16
MAX_DISTANCE = 128
POOL_WINDOWS = (2, 4, 8, 16)
POOL_STATE = 15
CONV_K = 3
RMS_EPS = 1e-6
LANES = 128
HALO = 16
CMP_PAD = 112
SEL_CHUNK = 512
MASK_BIG = 1e30
VMEM_LIMIT = 56 * 1024 * 1024


def _dot(a, b):
    return jnp.dot(a, b, preferred_element_type=F32)


def _dot_nt(a, b):
    return lax.dot_general(a, b, (((1,), (1,)), ((), ())), preferred_element_type=F32)


def _dot_split(a, b):
    hi = a.astype(BF16)
    lo = (a - hi.astype(F32)).astype(BF16)
    return _dot(hi, b) + _dot(lo, b)


def _group_mean_sq(x):
    r = lax.broadcasted_iota(jnp.int32, (LANES, LANES), 0) // HEAD_DIM
    c = lax.broadcasted_iota(jnp.int32, (LANES, LANES), 1) // HEAD_DIM
    ones_bd = (r == c).astype(BF16)
    return _dot_split(x * x, ones_bd) * (1.0 / HEAD_DIM)


def _head_rms(x, gain):
    parts = []
    for k in range(x.shape[1] // LANES):
        xs = x[:, k * LANES:(k + 1) * LANES]
        parts.append(xs * lax.rsqrt(_group_mean_sq(xs) + RMS_EPS))
    y = parts[0] if len(parts) == 1 else jnp.concatenate(parts, axis=1)
    return y * gain


def _row_rms(x, gain):
    return x * lax.rsqrt(jnp.mean(x * x, axis=-1, keepdims=True) + RMS_EPS) * gain


def _params(*sem):
    return pltpu.CompilerParams(dimension_semantics=sem, vmem_limit_bytes=VMEM_LIMIT)


def _ffn_kernel(x_ref, g_ref, wg_ref, wu_ref, wd_ref, o_ref, h_scr, acc_scr):
    f = pl.program_id(1)

    @pl.when(f == 0)
    def _():
        h_scr[...] = _row_rms(x_ref[...], g_ref[...]).astype(BF16)
        acc_scr[...] = jnp.zeros_like(acc_scr)

    h = h_scr[...]
    a = _dot(h, wg_ref[...])
    b = _dot(h, wu_ref[...])
    act = (a * jax.nn.sigmoid(a) * b).astype(BF16)
    acc_scr[...] += _dot(act, wd_ref[...])

    @pl.when(f == pl.num_programs(1) - 1)
    def _():
        o_ref[...] = x_ref[...] + 0.5 * acc_scr[...]


def _ffn(x, gain, wg, wu, wd, tm):
    n, d = x.shape
    dff = wg.shape[1]
    tf = dff // 2
    return pl.pallas_call(
        _ffn_kernel,
        grid=(n // tm, dff // tf),
        in_specs=[
            pl.BlockSpec((tm, d), lambda i, f: (i, 0)),
            pl.BlockSpec((1, d), lambda i, f: (0, 0)),
            pl.BlockSpec((d, tf), lambda i, f: (0, f)),
            pl.BlockSpec((d, tf), lambda i, f: (0, f)),
            pl.BlockSpec((tf, d), lambda i, f: (f, 0)),
        ],
        out_specs=pl.BlockSpec((tm, d), lambda i, f: (i, 0)),
        out_shape=jax.ShapeDtypeStruct((n, d), F32),
        scratch_shapes=[pltpu.VMEM((tm, d), BF16), pltpu.VMEM((tm, d), F32)],
        compiler_params=_params("parallel", "arbitrary"),
        name="ffn",
    )(x, gain.reshape(1, d), wg, wu, wd)


C_Q = 0
C_KSEL = 512
C_KWIN = 640
C_KCMP = 768
C_VCMP = 896
C_VSEL = 1024
C_VWIN = 1152
C_U = 1280
C_CB = 1536
C_CC = 1792
C_CH = 2048
C_MG = 2304
C_NG = 5376
C_END = 5504
N_NORMED = 768


def _proj_kernel(x_ref, g_ref, w_ref, hg_ref, q_ref, kv_ref, win_ref, ksv_ref, u_ref, cb_ref, z_ref, mg_ref, ng_ref):
    h = _row_rms(x_ref[...], g_ref[...]).astype(BF16)

    def sec(lo, hi):
        return _dot(h, w_ref[:, lo:hi])

    nrm = _head_rms(sec(0, N_NORMED), hg_ref[...])
    ksel = nrm[:, C_KSEL:C_KSEL + KV_W]
    kwin = nrm[:, C_KWIN:C_KWIN + KV_W]
    q_ref[...] = (nrm[:, :NSA_W] * (HEAD_DIM ** -0.5)).astype(BF16)
    rest = sec(C_KCMP, C_U)
    vsel = rest[:, 2 * KV_W:3 * KV_W]
    vwin = rest[:, 3 * KV_W:4 * KV_W]
    kv_ref[:, 0:2 * KV_W] = rest[:, 0:2 * KV_W]
    kv_ref[:, 2 * KV_W:3 * KV_W] = ksel
    kv_ref[:, 3 * KV_W:4 * KV_W] = vsel
    win_ref[:, 0:KV_W] = kwin
    win_ref[:, KV_W:2 * KV_W] = vwin
    ksv_ref[:, 0:KV_W] = ksel.astype(BF16)
    ksv_ref[:, KV_W:2 * KV_W] = vsel.astype(BF16)
    ksv_ref[:, 2 * KV_W:3 * KV_W] = kwin.astype(BF16)
    ksv_ref[:, 3 * KV_W:4 * KV_W] = vwin.astype(BF16)
    u_ref[...] = sec(C_U, C_CB)
    cv = sec(C_CB, C_MG)
    cw = C_CC - C_CB
    cb_ref[...] = cv[:, 0:cw]
    z_ref[...] = cv[:, cw:2 * cw] * cv[:, 2 * cw:3 * cw]
    mg_ref[...] = sec(C_MG, C_NG)
    ng_ref[...] = sec(C_NG, C_END)


def _proj(x, gain, w, head_gain, tm):
    n, d = x.shape
    row = lambda w_: pl.BlockSpec((tm, w_), lambda i: (i, 0))
    full = lambda a: pl.BlockSpec(a.shape, lambda i: (0, 0))
    gain = gain.reshape(1, d)
    widths = ((NSA_W, BF16), (4 * KV_W, F32), (2 * KV_W, F32), (4 * KV_W, BF16), (C_CB - C_U, F32),
              (C_CC - C_CB, F32), (C_CC - C_CB, F32), (C_NG - C_MG, F32), (C_END - C_NG, F32))
    return pl.pallas_call(
        _proj_kernel,
        grid=(n // tm,),
        in_specs=[row(d), full(gain), full(w), full(head_gain)],
        out_specs=[row(w_) for w_, _ in widths],
        out_shape=[jax.ShapeDtypeStruct((n, w_), dt) for w_, dt in widths],
        compiler_params=_params("parallel"),
        name="proj",
    )(x, gain, w, head_gain)


def _compress_kernel(k_ref, v_ref, pe_ref, w1_ref, w2_ref, kg_ref, kc_ref, vc_ref, *, n_ch):
    outs = []
    for s, src_ref in enumerate((k_ref, v_ref)):
        first = jnp.zeros((n_ch, KV_W), F32)
        second = jnp.zeros((n_ch, KV_W), F32)
        for l in range(CMP_STRIDE):
            xl = src_ref[pl.ds(l, n_ch, stride=CMP_STRIDE), :]
            first += _dot((xl + pe_ref[s, l:l + 1, :]).astype(BF16), w1_ref[s, l])
            second += _dot((xl + pe_ref[s, CMP_STRIDE + l:CMP_STRIDE + l + 1, :]).astype(BF16),
                           w1_ref[s, CMP_STRIDE + l])
        hid = first + pltpu.roll(second, n_ch - 1, 0)
        act = hid * jax.nn.sigmoid(hid)
        outs.append(_dot(act.astype(BF16), w2_ref[s]))
    kc = _head_rms(outs[0], kg_ref[...])
    rows = kc_ref.shape[0]
    for ref, val in ((kc_ref, kc), (vc_ref, outs[1])):
        ref[0:CMP_PAD, :] = jnp.zeros((CMP_PAD, KV_W), F32)
        ref[CMP_PAD:CMP_PAD + n_ch, :] = val
        ref[CMP_PAD + n_ch:rows, :] = jnp.zeros((rows - CMP_PAD - n_ch, KV_W), F32)


def _cmp_rows(t):
    return -(-(CMP_PAD + t // CMP_STRIDE) // LANES) * LANES


def _compress_prompt(kv4, pe2, w1bd, w2bd, kgain, b, t):
    n_ch = t // CMP_STRIDE
    rows = _cmp_rows(t)
    full = lambda a: pl.BlockSpec(a.shape, lambda i: (0,) * a.ndim)
    out = jax.ShapeDtypeStruct((b, rows, KV_W), F32)
    return pl.pallas_call(
        functools.partial(_compress_kernel, n_ch=n_ch),
        grid=(b,),
        in_specs=[pl.BlockSpec((t, KV_W), lambda i: (i, 0)), pl.BlockSpec((t, KV_W), lambda i: (i, 1)),
                  full(pe2), full(w1bd), full(w2bd), full(kgain)],
        out_specs=[pl.BlockSpec((None, rows, KV_W), lambda i: (i, 0, 0))] * 2,
        out_shape=[out, out],
        compiler_params=_params("parallel"),
        name="compress_prompt",
    )(kv4, kv4, pe2, w1bd, w2bd, kgain)


def _lane_tiles(x):
    return [x[:, k * LANES:(k + 1) * LANES] for k in range(x.shape[1] // LANES)]


def _softmax_lane_tiles(tiles):
    m = functools.reduce(jnp.maximum, tiles)
    m = jnp.broadcast_to(jnp.max(m, axis=1, keepdims=True), m.shape)
    m = jnp.where(m == NEG_INF, 0.0, m)
    e = [jnp.exp(x - m) for x in tiles]
    den = functools.reduce(jnp.add, e)
    den = jnp.broadcast_to(jnp.sum(den, axis=1, keepdims=True), den.shape)
    return e, 1.0 / jnp.where(den > 0, den, 1.0)


def _top_k_mask(score, k):
    col = lax.broadcasted_iota(jnp.int32, score.shape, 1).astype(F32)
    sel = jnp.zeros(score.shape, F32)
    work = score
    for _ in range(k):
        m = jnp.max(work, axis=1, keepdims=True)
        first = jnp.min(jnp.where(work == m, col, float(score.shape[1])), axis=1, keepdims=True)
        pick = col == first
        sel = jnp.maximum(sel, pick.astype(F32))
        work = jnp.where(pick, NEG_INF, work)
    return sel


def _attn_prompt_kernel(q_ref, ng_ref, kc_ref, vc_ref, kall_ref, emat_ref, smap_ref, d01_ref, bc_ref, wtab_ref, o_ref,
                        m_scr, acc_scr, ob_scr):
    i = pl.program_id(1)
    lane = lax.broadcasted_iota(jnp.int32, (Q_BLOCK, LANES), 1)
    near0 = pl.multiple_of(i * (Q_BLOCK // CMP_STRIDE), 8)
    own = pl.multiple_of(i * Q_BLOCK, Q_BLOCK)
    scores, lhs_g = [], []

    for g in range(N_KV_HEADS):
        in_group = (lane >= HEAD_DIM) == (g == 1)
        lhs = jnp.concatenate(
            [jnp.where(in_group, q_ref[:, h * LANES:(h + 1) * LANES], jnp.zeros((), BF16)) for h in range(HPG)], axis=0)

        s_far = _dot_nt(lhs, kc_ref[...].astype(BF16))
        s_near = _dot_nt(lhs, kc_ref[pl.ds(near0, LANES), :].astype(BF16)) + bc_ref[g]
        pcol = lax.broadcasted_iota(jnp.int32, (1, s_far.shape[1]), 1)
        s_far = s_far + jnp.where((pcol >= CMP_PAD) & (pcol < near0), 0.0, NEG_INF)
        ncol = lax.broadcasted_iota(jnp.int32, (1, LANES), 1)
        s_near = s_near + jnp.where(near0 + ncol >= CMP_PAD, 0.0, NEG_INF)
        e_c, inv_c = _softmax_lane_tiles(_lane_tiles(s_far) + [s_near])
        p_far = jnp.concatenate([e * inv_c for e in e_c[:-1]], axis=1)
        p_near = e_c[-1] * inv_c
        ob_scr[g, 0] = (_dot(p_far.astype(BF16), vc_ref[...].astype(BF16))
                        + _dot(p_near.astype(BF16), vc_ref[pl.ds(near0, LANES), :].astype(BF16)))

        imp_far = sum(p_far[h * Q_BLOCK:(h + 1) * Q_BLOCK] for h in range(HPG))
        imp_near = sum(p_near[h * Q_BLOCK:(h + 1) * Q_BLOCK] for h in range(HPG))
        score = (_dot_split(imp_far, smap_ref[...].astype(BF16))
                 + _dot_split(imp_near, smap_ref[pl.ds(near0, LANES), :].astype(BF16)))
        jcol = lax.broadcasted_iota(jnp.int32, score.shape, 1)
        cur = 2 * i + (lax.broadcasted_iota(jnp.int32, score.shape, 0) >= SEL_BLOCK).astype(jnp.int32)
        forced = (jcol == 0) | (jcol == cur) | (jcol == cur - 1)
        score = jnp.where(forced, jnp.inf, score)
        scores.append(jnp.where(jcol <= cur, score, NEG_INF))
        lhs_g.append(lhs)

        s_w = _dot_nt(lhs, kall_ref[pl.ds(own, WINDOW + Q_BLOCK), 2 * KV_W:3 * KV_W]) + wtab_ref[g]
        wcol = lax.broadcasted_iota(jnp.int32, s_w.shape, 1)
        s_w = jnp.where(wcol >= WINDOW - own, s_w, NEG_INF)
        e_w, inv_w = _softmax_lane_tiles(_lane_tiles(s_w))
        ob_scr[g, 2] = _dot(jnp.concatenate(e_w, axis=1).astype(BF16),
                            kall_ref[pl.ds(own, WINDOW + Q_BLOCK), 3 * KV_W:4 * KV_W]) * inv_w

    sel_both = _top_k_mask(jnp.concatenate(scores, axis=0), TOP_K)
    n_plain = jnp.maximum(i - 1, 0)
    per_chunk = SEL_CHUNK // Q_BLOCK
    n_full = n_plain // per_chunk
    n_rest = n_plain % per_chunk
    ones_v = jnp.ones((4 * SEL_CHUNK, LANES), BF16)

    for g in range(N_KV_HEADS):
        lhs = lhs_g[g]
        sel = sel_both[g * Q_BLOCK:(g + 1) * Q_BLOCK]
        not_sel = jnp.concatenate([(sel - 1.0) * MASK_BIG] * HPG, axis=0).astype(BF16)
        lhs_sel = jnp.concatenate([lhs, not_sel], axis=1)

        def sel_logits(pos, width):
            keys = jnp.concatenate([kall_ref[pl.ds(WINDOW + pos, width), 0:KV_W],
                                    emat_ref[pl.ds(Q_BLOCK + pos, width), :]], axis=1)
            return _dot_nt(lhs_sel, keys)

        def sel_values(pos, width):
            return jnp.concatenate([kall_ref[pl.ds(WINDOW + pos, width), KV_W:2 * KV_W], ones_v[:width]], axis=1)

        def lane_tile_max(s):
            out = s[:, 0:LANES]
            for k in range(1, s.shape[1] // LANES):
                out = jnp.maximum(out, s[:, k * LANES:(k + 1) * LANES])
            return out

        def weights(s):
            mb = m_scr[...]
            return jnp.concatenate([jnp.exp(s[:, k * LANES:(k + 1) * LANES] - mb)
                                    for k in range(s.shape[1] // LANES)], axis=1).astype(BF16)

        def rest_logits():
            s = sel_logits(pl.multiple_of(n_full * SEL_CHUNK, SEL_CHUNK), SEL_CHUNK)
            col = lax.broadcasted_iota(jnp.int32, s.shape, 1)
            return jnp.where(col < n_rest * Q_BLOCK, s, NEG_INF)

        near = pl.multiple_of(own - Q_BLOCK, Q_BLOCK)
        s_near2 = sel_logits(near, 2 * Q_BLOCK) + d01_ref[g]
        ncol2 = lax.broadcasted_iota(jnp.int32, s_near2.shape, 1)
        s_near2 = jnp.where((i == 0) & (ncol2 < Q_BLOCK), NEG_INF, s_near2)

        def over_plain_keys(fn):
            quad = 4 * SEL_CHUNK

            def quad_body(c, carry):
                pos = pl.multiple_of(c * quad, quad)
                fn(sel_logits(pos, quad), pos, quad)
                return carry

            lax.fori_loop(0, n_full // 4, quad_body, 0)
            left = n_full % 4

            @pl.when(left >= 2)
            def _():
                pos = pl.multiple_of((n_full - left) * SEL_CHUNK, SEL_CHUNK)
                fn(sel_logits(pos, 2 * SEL_CHUNK), pos, 2 * SEL_CHUNK)

            @pl.when(left % 2 == 1)
            def _():
                pos = pl.multiple_of((n_full - 1) * SEL_CHUNK, SEL_CHUNK)
                fn(sel_logits(pos, SEL_CHUNK), pos, SEL_CHUNK)

            @pl.when(n_rest > 0)
            def _():
                fn(rest_logits(), pl.multiple_of(n_full * SEL_CHUNK, SEL_CHUNK), SEL_CHUNK)

        def take_max(s, pos, width):
            m_scr[...] = jnp.maximum(m_scr[...], lane_tile_max(s))

        def accumulate(s, pos, width):
            acc_scr[...] += _dot(weights(s), sel_values(pos, width))

        m_scr[...] = lane_tile_max(s_near2)
        over_plain_keys(take_max)
        m_scr[...] = jnp.broadcast_to(jnp.max(m_scr[...], axis=1, keepdims=True), m_scr.shape)
        acc_scr[...] = _dot(weights(s_near2), sel_values(near, 2 * Q_BLOCK))
        over_plain_keys(accumulate)
        ob_scr[g, 1] = acc_scr[:, 0:LANES] / acc_scr[:, LANES:2 * LANES]

    sig = jax.nn.sigmoid(ng_ref[...])
    for h in range(HPG):
        slab = None
        for j in range(3):
            part = []
            for g in range(N_KV_HEADS):
                c = j * N_Q_HEADS + g * HPG + h
                part.append(sig[:, c:c + 1] * ob_scr[g, j, h * Q_BLOCK:(h + 1) * Q_BLOCK, :])
            term = jnp.where(lane < HEAD_DIM, part[0], part[1])
            slab = term if slab is None else slab + term
        o_ref[:, h * LANES:(h + 1) * LANES] = slab.astype(BF16)


def _attn_prompt(q, ng, kc, vc, kall, emat, smap, d01, bc, wtab, b, t):
    nq = t // Q_BLOCK
    rows = HPG * Q_BLOCK
    full = lambda a: pl.BlockSpec(a.shape, lambda bi, i: (0,) * a.ndim)
    per_batch = lambda a: pl.BlockSpec((None,) + a.shape[1:], lambda bi, i: (bi, 0, 0))
    return pl.pallas_call(
        _attn_prompt_kernel,
        grid=(b, nq),
        in_specs=[
            pl.BlockSpec((Q_BLOCK, NSA_W), lambda bi, i: (bi * nq + i, 0)),
            pl.BlockSpec((Q_BLOCK, LANES), lambda bi, i: (bi * nq + i, 0)),
            per_batch(kc), per_batch(vc), per_batch(kall),
            full(emat), full(smap), full(d01), full(bc), full(wtab),
        ],
        out_specs=pl.BlockSpec((Q_BLOCK, NSA_W), lambda bi, i: (bi * nq + i, 0)),
        out_shape=jax.ShapeDtypeStruct((b * t, NSA_W), BF16),
        scratch_shapes=[pltpu.VMEM((rows, LANES), F32), pltpu.VMEM((rows, 2 * LANES), F32),
                        pltpu.VMEM((N_KV_HEADS, 3, rows, LANES), F32)],
        compiler_params=_params("parallel", "arbitrary"),
        name="attn_prompt",
    )(q, ng, kc, vc, kall, emat, smap, d01, bc, wtab)


def _pool_rows(ext_scr, base, u, pos):
    m, width = u.shape
    lane = lax.broadcasted_iota(jnp.int32, (m, width), 1)
    gc = width // len(POOL_WINDOWS)
    run = u
    pooled = None
    for k in range(1, max(POOL_WINDOWS)):
        run = run + ext_scr[base - k:base - k + m, :]
        w = k + 1
        if w in POOL_WINDOWS:
            mean = run / jnp.minimum(pos + 1, w).astype(F32)
            pooled = mean if pooled is None else jnp.where(lane >= POOL_WINDOWS.index(w) * gc, mean, pooled)
    return pooled - u


def _conv_rows(ext_scr, base, m, cw_ref):
    y = None
    for j in range(CONV_K):
        off = base - (CONV_K - 1) + j
        term = cw_ref[j:j + 1, :] * ext_scr[off:off + m, :]
        y = term if y is None else y + term
    return y


def _mix_prompt_kernel(u_ref, uh_ref, z_ref, zh_ref, cb_ref, cw_ref, d_ref, c_ref, ext_scr, *, tm):
    i = pl.program_id(1)
    has_hist = (i > 0).astype(F32)
    pos = i * tm + lax.broadcasted_iota(jnp.int32, u_ref.shape, 0)
    ext_scr[0:HALO, :] = uh_ref[...] * has_hist
    ext_scr[HALO:HALO + tm, :] = u_ref[...]
    d_ref[...] = _pool_rows(ext_scr, HALO, u_ref[...], pos).astype(BF16)
    ext_scr[0:HALO, :] = zh_ref[...] * has_hist
    ext_scr[HALO:HALO + tm, :] = z_ref[...]
    c_ref[...] = (cb_ref[...] * _conv_rows(ext_scr, HALO, tm, cw_ref)).astype(BF16)


def _mix_prompt(u, z, cb, conv_w, b, t, tm):
    nt = t // tm
    width = u.shape[1]
    row = pl.BlockSpec((tm, width), lambda bi, i: (bi * nt + i, 0))
    halo = pl.BlockSpec((HALO, width), lambda bi, i: (jnp.maximum((bi * nt + i) * (tm // HALO) - 1, 0), 0))
    out = jax.ShapeDtypeStruct((b * t, width), BF16)
    return pl.pallas_call(
        functools.partial(_mix_prompt_kernel, tm=tm),
        grid=(b, nt),
        in_specs=[row, halo, row, halo, row, pl.BlockSpec(conv_w.shape, lambda bi, i: (0, 0))],
        out_specs=[row, row],
        out_shape=[out, out],
        scratch_shapes=[pltpu.VMEM((HALO + tm, width), F32)],
        compiler_params=_params("parallel", "parallel"),
        name="mix_prompt",
    )(u, u, z, z, cb, conv_w)


def _merge_kernel(x_ref, ap_ref, as_ref, dp_ref, ds_ref, cp_ref, cs_ref, mg_ref, wa_ref, wp_ref, ps_ref, wc_ref, wo_ref,
                  o_ref, *, prompt_tiles):
    dm = x_ref.shape[1]
    is_prompt = pl.program_id(0) < prompt_tiles
    a = jnp.where(is_prompt, ap_ref[...], as_ref[...])
    d = jnp.where(is_prompt, dp_ref[...], ds_ref[...])
    c = jnp.where(is_prompt, cp_ref[...], cs_ref[...])
    gates = jax.nn.sigmoid(mg_ref[...])
    m = gates[:, 0:dm] * _dot(a, wa_ref[...])
    m = m + gates[:, dm:2 * dm] * (_dot(d, wp_ref[...]) * ps_ref[...])
    m = m + gates[:, 2 * dm:3 * dm] * _dot(c, wc_ref[...])
    o_ref[...] = x_ref[...] + _dot(m.astype(BF16), wo_ref[...])


def _merge(x, attn_p, attn_s, d_p, d_s, c_p, c_s, mg, wa, wp, ps, wc, wo, tm):
    n, dm = x.shape
    n_p, n_s = attn_p.shape[0], attn_s.shape[0]
    assert n_p % tm == 0 and n_s % tm == 0 and n_p + n_s == n
    pt = n_p // tm
    row = lambda a: pl.BlockSpec((tm, a.shape[1]), lambda i: (i, 0))
    prompt = lambda a: pl.BlockSpec((tm, a.shape[1]), lambda i: (jnp.minimum(i, pt - 1), 0))
    sample = lambda a: pl.BlockSpec((tm, a.shape[1]), lambda i: (jnp.maximum(i - pt, 0), 0))
    full = lambda a: pl.BlockSpec(a.shape, lambda i: (0, 0))
    ps = ps.reshape(1, dm)
    return pl.pallas_call(
        functools.partial(_merge_kernel, prompt_tiles=pt),
        grid=(n // tm,),
        in_specs=[row(x), prompt(attn_p), sample(attn_s), prompt(d_p), sample(d_s), prompt(c_p), sample(c_s), row(mg),
                  full(wa), full(wp), full(ps), full(wc), full(wo)],
        out_specs=row(x),
        out_shape=jax.ShapeDtypeStruct((n, dm), F32),
        compiler_params=_params("parallel"),
        name="merge",
    )(x, attn_p, attn_s, d_p, d_s, c_p, c_s, mg, wa, wp, ps, wc, wo)


SAMPLE_ROWS = 8
SAMPLE_NB = 4


def _cmp_partial_kernel(xt_ref, pe_ref, w1_ref, o_ref, x_scr, *, n_ch):
    page = xt_ref.shape[2]
    for p in range(xt_ref.shape[0]):
        x_scr[p * page:(p + 1) * page, :] = xt_ref[p].T
    first = jnp.zeros((n_ch, KV_W), F32)
    second = jnp.zeros((n_ch, KV_W), F32)
    for l in range(CMP_STRIDE):
        xl = x_scr[pl.ds(l, n_ch, stride=CMP_STRIDE), :]
        first += _dot((xl + pe_ref[l:l + 1, :]).astype(BF16), w1_ref[l])
        second += _dot((xl + pe_ref[CMP_STRIDE + l:CMP_STRIDE + l + 1, :]).astype(BF16), w1_ref[CMP_STRIDE + l])
    o_ref[:, 0:KV_W] = first
    o_ref[:, KV_W:2 * KV_W] = second


def _cmp_partial(cache_t, pe2, w1bd, layer, n_pool):
    page = cache_t.shape[2]
    ch = page // CMP_STRIDE
    pp = next(p for p in (64, 32, 16, 8, 4, 2, 1) if n_pool % p == 0)
    nb = n_pool // pp
    return pl.pallas_call(
        functools.partial(_cmp_partial_kernel, n_ch=pp * ch),
        grid=(nb, 2),
        in_specs=[pl.BlockSpec((pp, KV_W, page), lambda i, s: (layer * nb + i, s, 0)),
                  pl.BlockSpec((None,) + pe2.shape[1:], lambda i, s: (s, 0, 0)),
                  pl.BlockSpec((None,) + w1bd.shape[1:], lambda i, s: (s, 0, 0, 0))],
        out_specs=pl.BlockSpec((pp * ch, 2 * KV_W), lambda i, s: (i, s)),
        out_shape=jax.ShapeDtypeStruct((n_pool * ch, 4 * KV_W), F32),
        scratch_shapes=[pltpu.VMEM((pp * page, KV_W), F32)],
        compiler_params=_params("parallel", "arbitrary"),
        name="cmp_partial",
    )(cache_t, pe2, w1bd)


def _softmax_pair(s_a, vt_a, s_b, v_b):
    m = jnp.maximum(jnp.max(s_a, axis=1, keepdims=True), jnp.max(s_b, axis=1, keepdims=True))
    e_a = jnp.exp(s_a - m)
    e_b = jnp.exp(s_b - m)
    den = jnp.sum(e_a, axis=1, keepdims=True) + jnp.sum(e_b, axis=1, keepdims=True)
    return (_dot_nt(e_a.astype(BF16), vt_a) + _dot(e_b.astype(BF16), v_b)) / den


def _sample_kernel(pt_ref, q_ref, ng_ref, kv_ref, win_ref, wnt_ref, u_ref, z_ref, cb_ref, swin_ref, spool_ref,
                   sconv_ref, cache_ref, fs_ref, w2_ref, kg_ref, cw_ref, smap_ref, emat_ref, bc_ref, dsel_ref, dwin_ref,
                   dnew_ref, attn_ref, d_ref, c_ref, owin_ref, opool_ref, oconv_ref,
                   kvbuf, fsbuf, sems, ext_scr, *, layer_base, n_pages, page, s_len, past):
    i = pl.program_id(0)
    slot = i % 2
    ch = page // CMP_STRIDE
    sr = SAMPLE_ROWS

    def copies(step, slot_):
        out = []
        for sb in range(SAMPLE_NB):
            for pg in range(n_pages):
                pid = pt_ref[step * SAMPLE_NB + sb, pg]
                out.append(pltpu.make_async_copy(
                    cache_ref.at[layer_base + pid, pl.ds(2 * KV_W, 2 * KV_W), :],
                    kvbuf.at[slot_, sb, :, pl.ds(pg * page, page)], sems.at[slot_, 0]))
                out.append(pltpu.make_async_copy(
                    fs_ref.at[pid], fsbuf.at[slot_, sb, pl.ds(pg * ch, ch), :], sems.at[slot_, 1]))
        return out

    @pl.when(i == 0)
    def _():
        for c in copies(0, 0):
            c.start()

    @pl.when(i + 1 < pl.num_programs(0))
    def _():
        for c in copies(i + 1, 1 - slot):
            c.start()

    for c in copies(i, slot):
        c.wait()

    lane = lax.broadcasted_iota(jnp.int32, (sr, LANES), 1)
    grp_rows = N_KV_HEADS * sr
    q_all = q_ref[...].astype(F32)
    sig_all = jax.nn.sigmoid(ng_ref[...])
    kvn = kv_ref[...]
    winn = win_ref[...]
    pad_rows = jnp.zeros((LANES - sr, KV_W), F32)
    w_keep = swin_ref.shape[2]
    wlane = lax.broadcasted_iota(jnp.int32, swin_ref.shape[1:], 1)
    hist = spool_ref.shape[1]
    ck = sconv_ref.shape[1]
    attn_rows, d_rows, c_rows = [], [], []

    every = range(SAMPLE_NB)

    def new_rows(x8):
        return jnp.concatenate([x8, pad_rows], axis=0).astype(BF16)

    lhs, ks_new, vs_new, kw_new, vw_new = [], [], [], [], []
    for sb in every:
        r0 = sb * sr
        q8 = q_all[r0:r0 + sr]
        pieces = []
        for h in range(HPG):
            for g in range(N_KV_HEADS):
                pieces.append(jnp.where((lane >= HEAD_DIM) == (g == 1), q8[:, h * LANES:(h + 1) * LANES], 0.0))
        lhs.append(jnp.concatenate(pieces, axis=0).astype(BF16))
        ks_new.append(new_rows(kvn[r0:r0 + sr, 2 * KV_W:3 * KV_W]))
        vs_new.append(new_rows(kvn[r0:r0 + sr, 3 * KV_W:4 * KV_W]))
        kw_new.append(new_rows(winn[r0:r0 + sr, 0:KV_W]))
        vw_new.append(new_rows(winn[r0:r0 + sr, KV_W:2 * KV_W]))

    n_ch = fsbuf.shape[2]
    cmp = []
    for s_ in range(2):
        hid = jnp.concatenate(
            [fsbuf[slot, sb, :, 2 * s_ * KV_W:(2 * s_ + 1) * KV_W]
             + pltpu.roll(fsbuf[slot, sb, :, (2 * s_ + 1) * KV_W:(2 * s_ + 2) * KV_W], n_ch - 1, 0) for sb in every],
            axis=0)
        cmp.append(_dot((hid * jax.nn.sigmoid(hid)).astype(BF16), w2_ref[s_]))
    kc_all = _head_rms(cmp[0], kg_ref[...]).astype(BF16)
    vc_all = cmp[1].astype(BF16)
    p_c, o_c = [], []
    for sb in every:
        s_c = _dot_nt(lhs[sb], kc_all[sb * n_ch:(sb + 1) * n_ch]) + bc_ref[...]
        e_c = jnp.exp(s_c - jnp.max(s_c, axis=1, keepdims=True))
        p_c.append(e_c / jnp.sum(e_c, axis=1, keepdims=True))
        o_c.append(_dot(p_c[sb].astype(BF16), vc_all[sb * n_ch:(sb + 1) * n_ch]))

    o_w = [_softmax_pair(_dot(lhs[sb], swin_ref[sb, 0:KV_W, :].astype(BF16)) + dwin_ref[...],
                         swin_ref[sb, KV_W:2 * KV_W, :].astype(BF16),
                         _dot_nt(lhs[sb], kw_new[sb]) + dnew_ref[...], vw_new[sb]) for sb in every]

    imp = jnp.concatenate([sum(p_c[sb][h * grp_rows:(h + 1) * grp_rows] for h in range(HPG)) for sb in every], axis=0)
    score = _dot_split(imp, smap_ref[...])
    jcol = lax.broadcasted_iota(jnp.int32, score.shape, 1)
    cur = (past + lax.broadcasted_iota(jnp.int32, score.shape, 0) % sr) // SEL_BLOCK
    forced = (jcol == 0) | (jcol == cur) | (jcol == cur - 1)
    score = jnp.where(forced, jnp.inf, score)
    score = jnp.where(jcol <= cur, score, NEG_INF)
    not_sel_all = (_top_k_mask(score, TOP_K) - 1.0) * MASK_BIG

    o_s = []
    for sb in every:
        not_sel = jnp.concatenate([not_sel_all[sb * grp_rows:(sb + 1) * grp_rows]] * HPG, axis=0).astype(BF16)
        lhs_sel = jnp.concatenate([lhs[sb], not_sel], axis=1)
        k_past = jnp.concatenate([kvbuf[slot, sb, 0:KV_W, :].astype(BF16), emat_ref[...]], axis=0)
        o_s.append(_softmax_pair(_dot(lhs_sel, k_past) + dsel_ref[...],
                                 kvbuf[slot, sb, KV_W:2 * KV_W, :].astype(BF16),
                                 _dot_nt(lhs[sb], ks_new[sb]) + dnew_ref[...], vs_new[sb]))

    for sb in every:
        r0 = sb * sr
        branch = (o_c[sb], o_s[sb], o_w[sb])
        sig8 = sig_all[r0:r0 + sr]
        slabs = []
        for h in range(HPG):
            slab = None
            for j in range(3):
                part = []
                for g in range(N_KV_HEADS):
                    c = j * N_Q_HEADS + g * HPG + h
                    lo = h * grp_rows + g * sr
                    part.append(sig8[:, c:c + 1] * branch[j][lo:lo + sr])
                term = jnp.where(lane < HEAD_DIM, part[0], part[1])
                slab = term if slab is None else slab + term
            slabs.append(slab)
        attn_rows.append(jnp.concatenate(slabs, axis=1))

        state = pltpu.roll(swin_ref[sb], w_keep - s_len, 1)
        for tt in range(s_len):
            state = jnp.where(wlane == w_keep - s_len + tt, wnt_ref[sb, :, tt:tt + 1], state)
        owin_ref[sb] = state

        u8 = u_ref[r0:r0 + sr, :]
        ext_scr[0:hist, :] = spool_ref[sb]
        ext_scr[hist:hist + sr, :] = u8
        d_rows.append(_pool_rows(ext_scr, hist, u8, hist + lax.broadcasted_iota(jnp.int32, u8.shape, 0)))
        opool_ref[sb] = ext_scr[s_len:s_len + hist, :]
        ext_scr[0:ck, :] = sconv_ref[sb]
        ext_scr[ck:ck + sr, :] = z_ref[r0:r0 + sr, :]
        c_rows.append(cb_ref[r0:r0 + sr, :] * _conv_rows(ext_scr, ck, sr, cw_ref))
        oconv_ref[sb] = ext_scr[s_len:s_len + ck, :]

    attn_ref[...] = jnp.concatenate(attn_rows, axis=0).astype(BF16)
    d_ref[...] = jnp.concatenate(d_rows, axis=0).astype(BF16)
    c_ref[...] = jnp.concatenate(c_rows, axis=0).astype(BF16)


def _sample_layer(layer, page_table, q, ng, kv4, win, wnt, u, z, cb, swin3, spool3, sconv3, cache3, fs3, w2bd, kgain,
                  conv_w, smap_s, emat, bc_s, dsel, dwin, dnew, n_p, db, s_len, n_pool):
    nsteps = db // SAMPLE_NB
    rb = SAMPLE_NB * SAMPLE_ROWS
    r0 = n_p // rb
    page = cache3.shape[2]
    n_pages = page_table.shape[1]
    past = n_pages * page
    tok = lambda a: pl.BlockSpec((rb, a.shape[1]), lambda i, pt: (r0 + i, 0))
    new_t = pl.BlockSpec((SAMPLE_NB,) + wnt.shape[1:], lambda i, pt: (i, 0, 0))
    state = lambda a: pl.BlockSpec((SAMPLE_NB,) + a.shape[1:], lambda i, pt: (layer * nsteps + i, 0, 0))
    full = lambda a: pl.BlockSpec(a.shape, lambda i, pt: (0,) * a.ndim)
    hbm = pl.BlockSpec(memory_space=pl.ANY)
    out_tok = lambda w_: pl.BlockSpec((rb, w_), lambda i, pt: (i, 0))
    out_state = lambda a: pl.BlockSpec((SAMPLE_NB,) + a.shape[1:], lambda i, pt: (i, 0, 0))
    width = u.shape[1]
    n_tok = db * SAMPLE_ROWS
    return pl.pallas_call(
        functools.partial(_sample_kernel, layer_base=layer * n_pool, n_pages=n_pages, page=page, s_len=s_len, past=past),
        grid_spec=pltpu.PrefetchScalarGridSpec(
            num_scalar_prefetch=1,
            grid=(nsteps,),
            in_specs=[tok(q), tok(ng), tok(kv4), tok(win), new_t, tok(u), tok(z), tok(cb), state(swin3), state(spool3),
                      state(sconv3), hbm, hbm, full(w2bd), full(kgain), full(conv_w), full(smap_s), full(emat),
                      full(bc_s), full(dsel), full(dwin), full(dnew)],
            out_specs=[out_tok(NSA_W), out_tok(width), out_tok(width), out_state(swin3), out_state(spool3),
                       out_state(sconv3)],
            scratch_shapes=[pltpu.VMEM((2, SAMPLE_NB, 2 * KV_W, past), F32),
                            pltpu.VMEM((2, SAMPLE_NB, past // CMP_STRIDE, 4 * KV_W), F32),
                            pltpu.SemaphoreType.DMA((2, 2)),
                            pltpu.VMEM((4 * SAMPLE_ROWS, width), F32)]),
        out_shape=[jax.ShapeDtypeStruct((n_tok, NSA_W), BF16), jax.ShapeDtypeStruct((n_tok, width), BF16),
                   jax.ShapeDtypeStruct((n_tok, width), BF16), jax.ShapeDtypeStruct((db,) + swin3.shape[1:], F32),
                   jax.ShapeDtypeStruct((db,) + spool3.shape[1:], F32),
                   jax.ShapeDtypeStruct((db,) + sconv3.shape[1:], F32)],
        compiler_params=_params("arbitrary"),
        name="sample_mix",
    )(page_table, q, ng, kv4, win, wnt, u, z, cb, swin3, spool3, sconv3, cache3, fs3, w2bd, kgain, conv_w, smap_s,
      emat, bc_s, dsel, dwin, dnew)


def _bucket_table(n):
    d = np.arange(n)
    nf = np.maximum(d, 1).astype(np.float32)
    large = MAX_EXACT + (np.log(nf / np.float32(MAX_EXACT)) / np.float32(math.log(MAX_DISTANCE / MAX_EXACT))
                         * np.float32(N_BUCKETS - MAX_EXACT)).astype(np.int32)
    return np.where(d < MAX_EXACT, d, np.minimum(large, N_BUCKETS - 1)).astype(np.int32)


def _bias_values(rel_bias, dist, visible=None):
    r, c = dist.shape
    offset = int(dist[0, 0])
    stride = int(dist[0, 0] - dist[0, 1]) if c > 1 else 1
    assert stride >= 1 and np.array_equal(dist, offset + np.arange(r)[:, None] - stride * np.arange(c)[None, :])
    visible = (dist >= 0) if visible is None else (visible & (dist >= 0))
    cw = stride * (c - 1) + 1
    period = r + cw
    m = np.arange(period)
    by_wrap = offset + np.where(m < cw, -m, period - m)
    bkt = _bucket_table(max(int(by_wrap.max()) + 1, MAX_DISTANCE))[np.clip(by_wrap, 0, None)]
    tab = rel_bias - rel_bias[N_BUCKETS - 1:N_BUCKETS]
    per_dist = tab[jnp.asarray(bkt)].T
    vals = jnp.tile(per_dist, (1, r))[:, :r * (period - 1)].reshape(-1, r, period - 1)[:, :, :cw:stride]
    vals = jnp.where(jnp.asarray(visible)[None], vals, NEG_INF)
    return vals.reshape((N_KV_HEADS, HPG) + dist.shape)


def _bias_table(rel_bias, dist):
    r, c = dist.shape
    return _bias_values(rel_bias, dist).reshape(N_KV_HEADS, HPG * r, c)


def _bias_table_sample(rel_bias, dist, visible=None):
    r, c = dist.shape
    return _bias_values(rel_bias, dist, visible).transpose(1, 0, 2, 3).reshape(HPG * N_KV_HEADS * r, c)


def _sel_map(n_cmp, n_rows, n_cols, pad):
    r_sel = SEL_BLOCK // CMP_STRIDE
    r_cmp = CMP_BLOCK // CMP_STRIDE
    out = np.zeros((n_rows, n_cols), np.float32)
    for j in range(n_cols):
        for m in range(r_sel):
            for n in range(r_cmp):
                c = r_sel * j + m - n
                if 0 <= c < n_cmp:
                    out[pad + c, j] += 1.0
    return out


def _slab_order(w, axis):
    shape = w.shape
    w = w.reshape(shape[:axis] + (N_KV_HEADS, HPG, HEAD_DIM) + shape[axis + 1:])
    return jnp.swapaxes(w, axis, axis + 1).reshape(shape)


def _proj_weight(w_in):
    kv0, ng0 = NSA_W, NSA_W + 6 * KV_W
    rest0 = ng0 + 3 * N_Q_HEADS
    slot = lambda s: w_in[:, :, kv0 + s * KV_W:kv0 + (s + 1) * KV_W]
    ng = w_in[:, :, ng0:rest0].reshape(w_in.shape[:2] + (N_Q_HEADS, 3))
    ng = jnp.swapaxes(ng, 2, 3).reshape(w_in.shape[:2] + (3 * N_Q_HEADS,))
    ng = jnp.pad(ng, ((0, 0), (0, 0), (0, C_END - C_NG - 3 * N_Q_HEADS)))
    parts = [_slab_order(w_in[:, :, :NSA_W], 2), slot(2), slot(4), slot(0), slot(1), slot(3), slot(5),
             w_in[:, :, rest0:], ng]
    out = jnp.concatenate(parts, axis=2)
    assert out.shape[2] == C_END
    return out


def _block_diag2(w):
    z = jnp.zeros_like(w)
    return jnp.concatenate([jnp.concatenate([w, z], axis=-1), jnp.concatenate([z, w], axis=-1)], axis=-2)


def _token_tile(n):
    for tm in (512, 256, 128):
        if n % tm == 0:
            return tm
    raise ValueError(f"token count {n} is not a multiple of 128")


def kernel(x_prompt, x_sample, cache_kv, state_win, state_pool, state_conv, page_table, rel_bias, ln_ffn1, w_ffn1_gate, w_ffn1_up, w_ffn1_down, ln_mix, w_in, q_norm, k_norm, cmp_pe, cmp_w1, cmp_w2, w_nsa_out, w_pool, pool_scale, conv_w, w_conv_out, w_o, ln_ffn2, w_ffn2_gate, w_ffn2_up, w_ffn2_down):
    b, t, dm = x_prompt.shape
    db, s, _ = x_sample.shape
    depth, n_pool, page = cache_kv.shape[:3]
    n_pages = page_table.shape[1]
    past = n_pages * page
    w_keep = state_win.shape[2]
    sr = SAMPLE_ROWS
    n_p, n_s = b * t, db * sr
    assert t % Q_BLOCK == 0 and t >= WINDOW and t // SEL_BLOCK <= LANES
    assert s <= sr and db % SAMPLE_NB == 0 and n_p % (SAMPLE_NB * sr) == 0
    assert (past + s) // CMP_STRIDE == past // CMP_STRIDE == LANES and w_keep <= past and s <= w_keep
    assert -(-(past + s) // SEL_BLOCK) <= LANES
    tm = _token_tile(n_p + n_s)
    tm_proj = min(tm, 256)
    tm_mix = _token_tile(t)

    w_proj = _proj_weight(w_in).astype(BF16)
    head_gain = jnp.concatenate([jnp.tile(q_norm, (1, N_Q_HEADS)), jnp.tile(k_norm[:, 1], (1, N_KV_HEADS)),
                                 jnp.tile(k_norm[:, 2], (1, N_KV_HEADS))], axis=1).reshape(depth, 1, N_NORMED)
    cmp_gain = jnp.tile(k_norm[:, 0], (1, N_KV_HEADS)).reshape(depth, 1, KV_W)
    pe2 = jnp.tile(cmp_pe, (1, 1, 1, N_KV_HEADS))
    w1bd = _block_diag2(cmp_w1).astype(BF16)
    w2bd = _block_diag2(cmp_w2).astype(BF16)
    wa = _slab_order(w_nsa_out, axis=1).astype(BF16)
    n_grp, gc, ge = w_pool.shape[1:]
    wp = jnp.zeros((depth, n_grp * gc, n_grp * ge), F32)
    for gi in range(n_grp):
        wp = wp.at[:, gi * gc:(gi + 1) * gc, gi * ge:(gi + 1) * ge].set(w_pool[:, gi])
    wp = wp.astype(BF16)
    wc = w_conv_out.astype(BF16)
    wo = w_o.astype(BF16)
    ffn1 = [w.astype(BF16) for w in (w_ffn1_gate, w_ffn1_up, w_ffn1_down)]
    ffn2 = [w.astype(BF16) for w in (w_ffn2_gate, w_ffn2_up, w_ffn2_down)]

    ti = np.arange(Q_BLOCK)
    d01 = _bias_table(rel_bias, Q_BLOCK + ti[:, None] - np.arange(2 * Q_BLOCK)[None, :])
    near_c = np.arange(LANES)
    bc = _bias_table(rel_bias, ti[:, None] - CMP_STRIDE * (near_c[None, :] - CMP_PAD) - (CMP_BLOCK - 1))
    n_cmp = t // CMP_STRIDE - CMP_BLOCK // CMP_STRIDE + 1
    smap = jnp.asarray(_sel_map(n_cmp, _cmp_rows(t), LANES, CMP_PAD))
    wj = np.arange(WINDOW + Q_BLOCK)[None, :]
    dist_pw = WINDOW + ti[:, None] - wj
    wtab = _bias_values(rel_bias, dist_pw, dist_pw < WINDOW).reshape(N_KV_HEADS, HPG * Q_BLOCK, WINDOW + Q_BLOCK)
    key_pos = np.arange(-Q_BLOCK, t + SEL_CHUNK)[:, None]
    emat_p = jnp.asarray((key_pos >= 0) & (key_pos < t)
                         & (key_pos // SEL_BLOCK == np.arange(LANES)[None, :])).astype(BF16)

    tq = np.arange(sr)[:, None]
    n_chunks = past // CMP_STRIDE
    n_cmp_s = n_chunks - CMP_BLOCK // CMP_STRIDE + 1
    cn = np.arange(n_chunks)[None, :]
    bc_s = _bias_table_sample(rel_bias, past + tq - (CMP_STRIDE * cn + CMP_BLOCK - 1), cn < n_cmp_s)
    dsel = _bias_table_sample(rel_bias, past + tq - np.arange(past)[None, :])
    tn = np.arange(LANES)[None, :]
    dnew = _bias_table_sample(rel_bias, tq - tn, tn < s)
    dist_w = w_keep + tq - np.arange(w_keep)[None, :]
    dwin = _bias_table_sample(rel_bias, dist_w, dist_w < WINDOW)
    smap_s = jnp.asarray(_sel_map(n_cmp_s, n_chunks, LANES, 0)).astype(BF16)
    emat = jnp.asarray(np.arange(LANES)[:, None] == np.arange(past)[None, :] // SEL_BLOCK).astype(BF16)

    cache3 = cache_kv.transpose(0, 1, 3, 4, 5, 2).reshape(depth * n_pool, -1, page)
    swin3 = state_win.transpose(0, 1, 3, 4, 5, 2).reshape(depth * db, -1, w_keep)
    spool3 = state_pool.reshape((depth * db,) + state_pool.shape[2:])
    sconv3 = state_conv.reshape((depth * db,) + state_conv.shape[2:])

    xs = jnp.pad(x_sample, ((0, 0), (0, sr - s), (0, 0)))
    x = jnp.concatenate([x_prompt.reshape(n_p, dm), xs.reshape(n_s, dm)], axis=0)
    outs = [[] for _ in range(8)]
    for l in range(depth):
        x = _ffn(x, ln_ffn1[l], ffn1[0][l], ffn1[1][l], ffn1[2][l], tm)
        q, kv4, win, ksv, u, cb, z, mg, ng = _proj(x, ln_mix[l], w_proj[l], head_gain[l], tm_proj)

        kc, vc = _compress_prompt(kv4, pe2[l], w1bd[l], w2bd[l], cmp_gain[l], b, t)
        kall = jnp.pad(ksv[:n_p].reshape(b, t, 4 * KV_W), ((0, 0), (WINDOW, SEL_CHUNK), (0, 0)))
        attn_p = _attn_prompt(q, ng, kc, vc, kall, emat_p, smap, d01, bc, wtab, b, t)
        d_p, c_p = _mix_prompt(u, z, cb, conv_w[l], b, t, tm_mix)

        fs3 = _cmp_partial(cache3, pe2[l], w1bd[l], l, n_pool).reshape(n_pool, page // CMP_STRIDE, -1)
        wnt = win[n_p:].reshape(db, sr, -1).transpose(0, 2, 1)
        attn_s, d_s, c_s, s_win, s_pool, s_conv = _sample_layer(
            l, page_table, q, ng, kv4, win, wnt, u, z, cb, swin3, spool3, sconv3, cache3, fs3, w2bd[l], cmp_gain[l],
            conv_w[l], smap_s, emat, bc_s, dsel, dwin, dnew, n_p, db, s, n_pool)

        x = _merge(x, attn_p, attn_s, d_p, d_s, c_p, c_s, mg, wa[l], wp[l], pool_scale[l], wc[l], wo[l], tm)
        x = _ffn(x, ln_ffn2[l], ffn2[0][l], ffn2[1][l], ffn2[2][l], tm)

        p_keep = min(WINDOW, t)
        outs[0].append(kv4[:n_p].reshape(b, t, 4, N_KV_HEADS, HEAD_DIM))
        outs[1].append(win[:n_p].reshape(b, t, 2, N_KV_HEADS, HEAD_DIM)[:, t - p_keep:])
        outs[2].append(u[:n_p].reshape(b, t, -1)[:, t - POOL_STATE:])
        outs[3].append(z[:n_p].reshape(b, t, -1)[:, t - (CONV_K - 1):])
        outs[4].append(kv4[n_p:].reshape(db, sr, 4, N_KV_HEADS, HEAD_DIM)[:, :s])
        outs[5].append(s_win.reshape(db, 2, N_KV_HEADS, HEAD_DIM, w_keep).transpose(0, 4, 1, 2, 3))
        outs[6].append(s_pool)
        outs[7].append(s_conv)

    y_prompt = x[:n_p].reshape(b, t, dm)
    y_sample = x[n_p:].reshape(db, sr, dm)[:, :s]
    return (y_prompt, y_sample) + tuple(jnp.stack(o) for o in outs)
```

```python
import functools
import math

import numpy as np
import jax
import jax.numpy as jnp
from jax import lax
from jax.experimental import pallas as pl
from jax.experimental.pallas import tpu as pltpu

F32 = jnp.float32
BF16 = jnp.bfloat16
NEG_INF = float("-inf")

HEAD_DIM = 64
N_KV_HEADS = 2
HPG = 4
N_Q_HEADS = N_KV_HEADS * HPG
NSA_W = N_Q_HEADS * HEAD_DIM
KV_W = N_KV_HEADS * HEAD_DIM
CMP_BLOCK = 32
CMP_STRIDE = 16
SEL_BLOCK = 64
TOP_K = 8
WINDOW = 512
Q_BLOCK = 128
N_BUCKETS = 32
MAX_EXACT = 16
MAX_DISTANCE = 128
POOL_WINDOWS = (2, 4, 8, 16)
POOL_STATE = 15
CONV_K = 3
RMS_EPS = 1e-6
LANES = 128
HALO = 16
CMP_PAD = 112
SEL_CHUNK = 512
MASK_BIG = 1e30
VMEM_LIMIT = 56 * 1024 * 1024


def _dot(a, b):
    return jnp.dot(a, b, preferred_element_type=F32)


def _dot_nt(a, b):
    return lax.dot_general(a, b, (((1,), (1,)), ((), ())), preferred_element_type=F32)


def _dot_split(a, b):
    hi = a.astype(BF16)
    lo = (a - hi.astype(F32)).astype(BF16)
    return _dot(hi, b) + _dot(lo, b)


def _group_mean_sq(x):
    r = lax.broadcasted_iota(jnp.int32, (LANES, LANES), 0) // HEAD_DIM
    c = lax.broadcasted_iota(jnp.int32, (LANES, LANES), 1) // HEAD_DIM
    ones_bd = (r == c).astype(BF16)
    return _dot_split(x * x, ones_bd) * (1.0 / HEAD_DIM)


def _head_rms(x, gain):
    parts = []
    for k in range(x.shape[1] // LANES):
        xs = x[:, k * LANES:(k + 1) * LANES]
        parts.append(xs * lax.rsqrt(_group_mean_sq(xs) + RMS_EPS))
    y = parts[0] if len(parts) == 1 else jnp.concatenate(parts, axis=1)
    return y * gain


def _row_rms(x, gain):
    return x * lax.rsqrt(jnp.mean(x * x, axis=-1, keepdims=True) + RMS_EPS) * gain


def _params(*sem):
    return pltpu.CompilerParams(dimension_semantics=sem, vmem_limit_bytes=VMEM_LIMIT)


def _ffn_kernel(x_ref, g_ref, wg_ref, wu_ref, wd_ref, o_ref, *, n_split):
    x = x_ref[...]
    h = _row_rms(x, g_ref[...]).astype(BF16)
    tf = wg_ref.shape[1] // n_split
    acc = None
    for f in range(n_split):
        a = _dot(h, wg_ref[:, f * tf:(f + 1) * tf])
        b = _dot(h, wu_ref[:, f * tf:(f + 1) * tf])
        part = _dot((a * jax.nn.sigmoid(a) * b).astype(BF16), wd_ref[f * tf:(f + 1) * tf, :])
        acc = part if acc is None else acc + part
    o_ref[...] = x + 0.5 * acc


def _ffn(x, gain, wg, wu, wd, tm):
    n, d = x.shape
    dff = wg.shape[1]
    n_split = 2
    assert dff % (n_split * LANES) == 0
    resident = lambda a: pl.BlockSpec(a.shape, lambda i: (0, 0), pipeline_mode=pl.Buffered(1))
    return pl.pallas_call(
        functools.partial(_ffn_kernel, n_split=n_split),
        grid=(n // tm,),
        in_specs=[pl.BlockSpec((tm, d), lambda i: (i, 0)), pl.BlockSpec((1, d), lambda i: (0, 0)),
                  resident(wg), resident(wu), resident(wd)],
        out_specs=pl.BlockSpec((tm, d), lambda i: (i, 0)),
        out_shape=jax.ShapeDtypeStruct((n, d), F32),
        compiler_params=_params("parallel"),
        name="ffn",
    )(x, gain.reshape(1, d), wg, wu, wd)


C_Q = 0
C_KSEL = 512
C_KWIN = 640
C_KCMP = 768
C_VCMP = 896
C_VSEL = 1024
C_VWIN = 1152
C_U = 1280
C_CB = 1536
C_CC = 1792
C_CH = 2048
C_MG = 2304
C_NG = 5376
C_END = 5504
N_NORMED = 768


def _proj_kernel(x_ref, g_ref, w_ref, hg_ref, q_ref, kv_ref, win_ref, ksv_ref, u_ref, cb_ref, z_ref, mg_ref, ng_ref):
    h = _row_rms(x_ref[...], g_ref[...]).astype(BF16)

    def sec(lo, hi):
        return _dot(h, w_ref[:, lo:hi])

    nrm = _head_rms(sec(0, N_NORMED), hg_ref[...])
    ksel = nrm[:, C_KSEL:C_KSEL + KV_W]
    kwin = nrm[:, C_KWIN:C_KWIN + KV_W]
    q_ref[...] = (nrm[:, :NSA_W] * (HEAD_DIM ** -0.5)).astype(BF16)
    rest = sec(C_KCMP, C_U)
    vsel = rest[:, 2 * KV_W:3 * KV_W]
    vwin = rest[:, 3 * KV_W:4 * KV_W]
    kv_ref[:, 0:2 * KV_W] = rest[:, 0:2 * KV_W]
    kv_ref[:, 2 * KV_W:3 * KV_W] = ksel
    kv_ref[:, 3 * KV_W:4 * KV_W] = vsel
    win_ref[:, 0:KV_W] = kwin
    win_ref[:, KV_W:2 * KV_W] = vwin
    ksv_ref[:, 0:KV_W] = ksel.astype(BF16)
    ksv_ref[:, KV_W:2 * KV_W] = vsel.astype(BF16)
    ksv_ref[:, 2 * KV_W:3 * KV_W] = kwin.astype(BF16)
    ksv_ref[:, 3 * KV_W:4 * KV_W] = vwin.astype(BF16)
    u_ref[...] = sec(C_U, C_CB)
    cv = sec(C_CB, C_MG)
    cw = C_CC - C_CB
    cb_ref[...] = cv[:, 0:cw]
    z_ref[...] = cv[:, cw:2 * cw] * cv[:, 2 * cw:3 * cw]
    mg_ref[...] = sec(C_MG, C_NG)
    ng_ref[...] = sec(C_NG, C_END)


def _proj(x, gain, w, head_gain, tm):
    n, d = x.shape
    row = lambda w_: pl.BlockSpec((tm, w_), lambda i: (i, 0))
    full = lambda a: pl.BlockSpec(a.shape, lambda i: (0, 0))
    gain = gain.reshape(1, d)
    widths = ((NSA_W, BF16), (4 * KV_W, F32), (2 * KV_W, F32), (4 * KV_W, BF16), (C_CB - C_U, F32),
              (C_CC - C_CB, F32), (C_CC - C_CB, F32), (C_NG - C_MG, F32), (C_END - C_NG, F32))
    return pl.pallas_call(
        _proj_kernel,
        grid=(n // tm,),
        in_specs=[row(d), full(gain), full(w), full(head_gain)],
        out_specs=[row(w_) for w_, _ in widths],
        out_shape=[jax.ShapeDtypeStruct((n, w_), dt) for w_, dt in widths],
        compiler_params=_params("parallel"),
        name="proj",
    )(x, gain, w, head_gain)


def _compress_kernel(k_ref, v_ref, pe_ref, w1_ref, w2_ref, kg_ref, kc_ref, vc_ref, *, n_ch):
    outs = []
    for s, src_ref in enumerate((k_ref, v_ref)):
        first = jnp.zeros((n_ch, KV_W), F32)
        second = jnp.zeros((n_ch, KV_W), F32)
        for l in range(CMP_STRIDE):
            xl = src_ref[pl.ds(l, n_ch, stride=CMP_STRIDE), :]
            first += _dot((xl + pe_ref[s, l:l + 1, :]).astype(BF16), w1_ref[s, l])
            second += _dot((xl + pe_ref[s, CMP_STRIDE + l:CMP_STRIDE + l + 1, :]).astype(BF16),
                           w1_ref[s, CMP_STRIDE + l])
        hid = first + pltpu.roll(second, n_ch - 1, 0)
        act = hid * jax.nn.sigmoid(hid)
        outs.append(_dot(act.astype(BF16), w2_ref[s]))
    kc = _head_rms(outs[0], kg_ref[...])
    rows = kc_ref.shape[0]
    for ref, val in ((kc_ref, kc), (vc_ref, outs[1])):
        ref[0:CMP_PAD, :] = jnp.zeros((CMP_PAD, KV_W), F32)
        ref[CMP_PAD:CMP_PAD + n_ch, :] = val
        ref[CMP_PAD + n_ch:rows, :] = jnp.zeros((rows - CMP_PAD - n_ch, KV_W), F32)


def _cmp_rows(t):
    return -(-(CMP_PAD + t // CMP_STRIDE) // LANES) * LANES


def _compress_prompt(kv4, pe2, w1bd, w2bd, kgain, b, t):
    n_ch = t // CMP_STRIDE
    rows = _cmp_rows(t)
    full = lambda a: pl.BlockSpec(a.shape, lambda i: (0,) * a.ndim)
    out = jax.ShapeDtypeStruct((b, rows, KV_W), F32)
    return pl.pallas_call(
        functools.partial(_compress_kernel, n_ch=n_ch),
        grid=(b,),
        in_specs=[pl.BlockSpec((t, KV_W), lambda i: (i, 0)), pl.BlockSpec((t, KV_W), lambda i: (i, 1)),
                  full(pe2), full(w1bd), full(w2bd), full(kgain)],
        out_specs=[pl.BlockSpec((None, rows, KV_W), lambda i: (i, 0, 0))] * 2,
        out_shape=[out, out],
        compiler_params=_params("parallel"),
        name="compress_prompt",
    )(kv4, kv4, pe2, w1bd, w2bd, kgain)


def _lane_tiles(x):
    return [x[:, k * LANES:(k + 1) * LANES] for k in range(x.shape[1] // LANES)]


def _softmax_lane_tiles(tiles):
    m = functools.reduce(jnp.maximum, tiles)
    m = jnp.broadcast_to(jnp.max(m, axis=1, keepdims=True), m.shape)
    m = jnp.where(m == NEG_INF, 0.0, m)
    e = [jnp.exp(x - m) for x in tiles]
    den = functools.reduce(jnp.add, e)
    den = jnp.broadcast_to(jnp.sum(den, axis=1, keepdims=True), den.shape)
    return e, 1.0 / jnp.where(den > 0, den, 1.0)


def _top_k_mask(score, k):
    col = lax.broadcasted_iota(jnp.int32, score.shape, 1).astype(F32)
    sel = jnp.zeros(score.shape, F32)
    work = score
    for _ in range(k):
        m = jnp.max(work, axis=1, keepdims=True)
        first = jnp.min(jnp.where(work == m, col, float(score.shape[1])), axis=1, keepdims=True)
        pick = col == first
        sel = jnp.maximum(sel, pick.astype(F32))
        work = jnp.where(pick, NEG_INF, work)
    return sel


def _attn_prompt_kernel(q_ref, ng_ref, kc_ref, vc_ref, kall_ref, emat_ref, smap_ref, d01_ref, bc_ref, wtab_ref, o_ref,
                        m_scr, acc_scr, ob_scr):
    i = pl.program_id(1)
    lane = lax.broadcasted_iota(jnp.int32, (Q_BLOCK, LANES), 1)
    near0 = pl.multiple_of(i * (Q_BLOCK // CMP_STRIDE), 8)
    own = pl.multiple_of(i * Q_BLOCK, Q_BLOCK)
    scores, lhs_g = [], []

    for g in range(N_KV_HEADS):
        in_group = (lane >= HEAD_DIM) == (g == 1)
        lhs = jnp.concatenate(
            [jnp.where(in_group, q_ref[:, h * LANES:(h + 1) * LANES], jnp.zeros((), BF16)) for h in range(HPG)], axis=0)

        s_far = _dot_nt(lhs, kc_ref[...].astype(BF16))
        s_near = _dot_nt(lhs, kc_ref[pl.ds(near0, LANES), :].astype(BF16)) + bc_ref[g]
        pcol = lax.broadcasted_iota(jnp.int32, (1, s_far.shape[1]), 1)
        s_far = s_far + jnp.where((pcol >= CMP_PAD) & (pcol < near0), 0.0, NEG_INF)
        ncol = lax.broadcasted_iota(jnp.int32, (1, LANES), 1)
        s_near = s_near + jnp.where(near0 + ncol >= CMP_PAD, 0.0, NEG_INF)
        e_c, inv_c = _softmax_lane_tiles(_lane_tiles(s_far) + [s_near])
        p_far = jnp.concatenate([e * inv_c for e in e_c[:-1]], axis=1)
        p_near = e_c[-1] * inv_c
        ob_scr[g, 0] = (_dot(p_far.astype(BF16), vc_ref[...].astype(BF16))
                        + _dot(p_near.astype(BF16), vc_ref[pl.ds(near0, LANES), :].astype(BF16)))

        imp_far = sum(p_far[h * Q_BLOCK:(h + 1) * Q_BLOCK] for h in range(HPG))
        imp_near = sum(p_near[h * Q_BLOCK:(h + 1) * Q_BLOCK] for h in range(HPG))
        score = (_dot_split(imp_far, smap_ref[...].astype(BF16))
                 + _dot_split(imp_near, smap_ref[pl.ds(near0, LANES), :].astype(BF16)))
        jcol = lax.broadcasted_iota(jnp.int32, score.shape, 1)
        cur = 2 * i + (lax.broadcasted_iota(jnp.int32, score.shape, 0) >= SEL_BLOCK).astype(jnp.int32)
        forced = (jcol == 0) | (jcol == cur) | (jcol == cur - 1)
        score = jnp.where(forced, jnp.inf, score)
        scores.append(jnp.where(jcol <= cur, score, NEG_INF))
        lhs_g.append(lhs)

        s_w = _dot_nt(lhs, kall_ref[pl.ds(own, WINDOW + Q_BLOCK), 2 * KV_W:3 * KV_W]) + wtab_ref[g]
        wcol = lax.broadcasted_iota(jnp.int32, s_w.shape, 1)
        s_w = jnp.where(wcol >= WINDOW - own, s_w, NEG_INF)
        e_w, inv_w = _softmax_lane_tiles(_lane_tiles(s_w))
        ob_scr[g, 2] = _dot(jnp.concatenate(e_w, axis=1).astype(BF16),
                            kall_ref[pl.ds(own, WINDOW + Q_BLOCK), 3 * KV_W:4 * KV_W]) * inv_w

    sel_both = _top_k_mask(jnp.concatenate(scores, axis=0), TOP_K)
    n_plain = jnp.maximum(i - 1, 0)
    per_chunk = SEL_CHUNK // Q_BLOCK
    n_full = n_plain // per_chunk
    n_rest = n_plain % per_chunk
    ones_v = jnp.ones((4 * SEL_CHUNK, LANES), BF16)

    for g in range(N_KV_HEADS):
        lhs = lhs_g[g]
        sel = sel_both[g * Q_BLOCK:(g + 1) * Q_BLOCK]
        not_sel = jnp.concatenate([(sel - 1.0) * MASK_BIG] * HPG, axis=0).astype(BF16)
        lhs_sel = jnp.concatenate([lhs, not_sel], axis=1)

        def sel_logits(pos, width):
            keys = jnp.concatenate([kall_ref[pl.ds(WINDOW + pos, width), 0:KV_W],
                                    emat_ref[pl.ds(Q_BLOCK + pos, width), :]], axis=1)
            return _dot_nt(lhs_sel, keys)

        def sel_values(pos, width):
            return jnp.concatenate([kall_ref[pl.ds(WINDOW + pos, width), KV_W:2 * KV_W], ones_v[:width]], axis=1)

        def lane_tile_max(s):
            out = s[:, 0:LANES]
            for k in range(1, s.shape[1] // LANES):
                out = jnp.maximum(out, s[:, k * LANES:(k + 1) * LANES])
            return out

        def weights(s):
            mb = m_scr[...]
            return jnp.concatenate([jnp.exp(s[:, k * LANES:(k + 1) * LANES] - mb)
                                    for k in range(s.shape[1] // LANES)], axis=1).astype(BF16)

        def rest_logits():
            s = sel_logits(pl.multiple_of(n_full * SEL_CHUNK, SEL_CHUNK), SEL_CHUNK)
            col = lax.broadcasted_iota(jnp.int32, s.shape, 1)
            return jnp.where(col < n_rest * Q_BLOCK, s, NEG_INF)

        near = pl.multiple_of(own - Q_BLOCK, Q_BLOCK)
        s_near2 = sel_logits(near, 2 * Q_BLOCK) + d01_ref[g]
        ncol2 = lax.broadcasted_iota(jnp.int32, s_near2.shape, 1)
        s_near2 = jnp.where((i == 0) & (ncol2 < Q_BLOCK), NEG_INF, s_near2)

        def over_plain_keys(fn):
            quad = 4 * SEL_CHUNK

            def quad_body(c, carry):
                pos = pl.multiple_of(c * quad, quad)
                fn(sel_logits(pos, quad), pos, quad)
                return carry

            lax.fori_loop(0, n_full // 4, quad_body, 0)
            left = n_full % 4

            @pl.when(left >= 2)
            def _():
                pos = pl.multiple_of((n_full - left) * SEL_CHUNK, SEL_CHUNK)
                fn(sel_logits(pos, 2 * SEL_CHUNK), pos, 2 * SEL_CHUNK)

            @pl.when(left % 2 == 1)
            def _():
                pos = pl.multiple_of((n_full - 1) * SEL_CHUNK, SEL_CHUNK)
                fn(sel_logits(pos, SEL_CHUNK), pos, SEL_CHUNK)

            @pl.when(n_rest > 0)
            def _():
                fn(rest_logits(), pl.multiple_of(n_full * SEL_CHUNK, SEL_CHUNK), SEL_CHUNK)

        def take_max(s, pos, width):
            m_scr[...] = jnp.maximum(m_scr[...], lane_tile_max(s))

        def accumulate(s, pos, width):
            acc_scr[...] += _dot(weights(s), sel_values(pos, width))

        m_scr[...] = lane_tile_max(s_near2)
        over_plain_keys(take_max)
        m_scr[...] = jnp.broadcast_to(jnp.max(m_scr[...], axis=1, keepdims=True), m_scr.shape)
        acc_scr[...] = _dot(weights(s_near2), sel_values(near, 2 * Q_BLOCK))
        over_plain_keys(accumulate)
        ob_scr[g, 1] = acc_scr[:, 0:LANES] / acc_scr[:, LANES:2 * LANES]

    sig = jax.nn.sigmoid(ng_ref[...])
    for h in range(HPG):
        slab = None
        for j in range(3):
            part = []
            for g in range(N_KV_HEADS):
                c = j * N_Q_HEADS + g * HPG + h
                part.append(sig[:, c:c + 1] * ob_scr[g, j, h * Q_BLOCK:(h + 1) * Q_BLOCK, :])
            term = jnp.where(lane < HEAD_DIM, part[0], part[1])
            slab = term if slab is None else slab + term
        o_ref[:, h * LANES:(h + 1) * LANES] = slab.astype(BF16)


def _attn_prompt(q, ng, kc, vc, kall, emat, smap, d01, bc, wtab, b, t):
    nq = t // Q_BLOCK
    rows = HPG * Q_BLOCK
    full = lambda a: pl.BlockSpec(a.shape, lambda bi, i: (0,) * a.ndim)
    per_batch = lambda a: pl.BlockSpec((None,) + a.shape[1:], lambda bi, i: (bi, 0, 0))
    return pl.pallas_call(
        _attn_prompt_kernel,
        grid=(b, nq),
        in_specs=[
            pl.BlockSpec((Q_BLOCK, NSA_W), lambda bi, i: (bi * nq + i, 0)),
            pl.BlockSpec((Q_BLOCK, LANES), lambda bi, i: (bi * nq + i, 0)),
            per_batch(kc), per_batch(vc), per_batch(kall),
            full(emat), full(smap), full(d01), full(bc), full(wtab),
        ],
        out_specs=pl.BlockSpec((Q_BLOCK, NSA_W), lambda bi, i: (bi * nq + i, 0)),
        out_shape=jax.ShapeDtypeStruct((b * t, NSA_W), BF16),
        scratch_shapes=[pltpu.VMEM((rows, LANES), F32), pltpu.VMEM((rows, 2 * LANES), F32),
                        pltpu.VMEM((N_KV_HEADS, 3, rows, LANES), F32)],
        compiler_params=_params("parallel", "arbitrary"),
        name="attn_prompt",
    )(q, ng, kc, vc, kall, emat, smap, d01, bc, wtab)


def _pool_rows(ext_scr, base, u, pos):
    m, width = u.shape
    lane = lax.broadcasted_iota(jnp.int32, (m, width), 1)
    gc = width // len(POOL_WINDOWS)
    run = u
    pooled = None
    for k in range(1, max(POOL_WINDOWS)):
        run = run + ext_scr[base - k:base - k + m, :]
        w = k + 1
        if w in POOL_WINDOWS:
            mean = run / jnp.minimum(pos + 1, w).astype(F32)
            pooled = mean if pooled is None else jnp.where(lane >= POOL_WINDOWS.index(w) * gc, mean, pooled)
    return pooled - u


def _conv_rows(ext_scr, base, m, cw_ref):
    y = None
    for j in range(CONV_K):
        off = base - (CONV_K - 1) + j
        term = cw_ref[j:j + 1, :] * ext_scr[off:off + m, :]
        y = term if y is None else y + term
    return y


def _mix_prompt_kernel(u_ref, uh_ref, z_ref, zh_ref, cb_ref, cw_ref, d_ref, c_ref, ext_scr, *, tm):
    i = pl.program_id(1)
    has_hist = (i > 0).astype(F32)
    pos = i * tm + lax.broadcasted_iota(jnp.int32, u_ref.shape, 0)
    ext_scr[0:HALO, :] = uh_ref[...] * has_hist
    ext_scr[HALO:HALO + tm, :] = u_ref[...]
    d_ref[...] = _pool_rows(ext_scr, HALO, u_ref[...], pos).astype(BF16)
    ext_scr[0:HALO, :] = zh_ref[...] * has_hist
    ext_scr[HALO:HALO + tm, :] = z_ref[...]
    c_ref[...] = (cb_ref[...] * _conv_rows(ext_scr, HALO, tm, cw_ref)).astype(BF16)


def _mix_prompt(u, z, cb, conv_w, b, t, tm):
    nt = t // tm
    width = u.shape[1]
    row = pl.BlockSpec((tm, width), lambda bi, i: (bi * nt + i, 0))
    halo = pl.BlockSpec((HALO, width), lambda bi, i: (jnp.maximum((bi * nt + i) * (tm // HALO) - 1, 0), 0))
    out = jax.ShapeDtypeStruct((b * t, width), BF16)
    return pl.pallas_call(
        functools.partial(_mix_prompt_kernel, tm=tm),
        grid=(b, nt),
        in_specs=[row, halo, row, halo, row, pl.BlockSpec(conv_w.shape, lambda bi, i: (0, 0))],
        out_specs=[row, row],
        out_shape=[out, out],
        scratch_shapes=[pltpu.VMEM((HALO + tm, width), F32)],
        compiler_params=_params("parallel", "parallel"),
        name="mix_prompt",
    )(u, u, z, z, cb, conv_w)


def _merge_kernel(x_ref, ap_ref, as_ref, dp_ref, ds_ref, cp_ref, cs_ref, mg_ref, wa_ref, wp_ref, ps_ref, wc_ref, wo_ref,
                  o_ref, *, prompt_tiles):
    dm = x_ref.shape[1]
    is_prompt = pl.program_id(0) < prompt_tiles
    a = jnp.where(is_prompt, ap_ref[...], as_ref[...])
    d = jnp.where(is_prompt, dp_ref[...], ds_ref[...])
    c = jnp.where(is_prompt, cp_ref[...], cs_ref[...])
    gates = jax.nn.sigmoid(mg_ref[...])
    m = gates[:, 0:dm] * _dot(a, wa_ref[...])
    m = m + gates[:, dm:2 * dm] * (_dot(d, wp_ref[...]) * ps_ref[...])
    m = m + gates[:, 2 * dm:3 * dm] * _dot(c, wc_ref[...])
    o_ref[...] = x_ref[...] + _dot(m.astype(BF16), wo_ref[...])


def _merge(x, attn_p, attn_s, d_p, d_s, c_p, c_s, mg, wa, wp, ps, wc, wo, tm):
    n, dm = x.shape
    n_p, n_s = attn_p.shape[0], attn_s.shape[0]
    assert n_p % tm == 0 and n_s % tm == 0 and n_p + n_s == n
    pt = n_p // tm
    row = lambda a: pl.BlockSpec((tm, a.shape[1]), lambda i: (i, 0))
    prompt = lambda a: pl.BlockSpec((tm, a.shape[1]), lambda i: (jnp.minimum(i, pt - 1), 0))
    sample = lambda a: pl.BlockSpec((tm, a.shape[1]), lambda i: (jnp.maximum(i - pt, 0), 0))
    full = lambda a: pl.BlockSpec(a.shape, lambda i: (0, 0))
    ps = ps.reshape(1, dm)
    return pl.pallas_call(
        functools.partial(_merge_kernel, prompt_tiles=pt),
        grid=(n // tm,),
        in_specs=[row(x), prompt(attn_p), sample(attn_s), prompt(d_p), sample(d_s), prompt(c_p), sample(c_s), row(mg),
                  full(wa), full(wp), full(ps), full(wc), full(wo)],
        out_specs=row(x),
        out_shape=jax.ShapeDtypeStruct((n, dm), F32),
        compiler_params=_params("parallel"),
        name="merge",
    )(x, attn_p, attn_s, d_p, d_s, c_p, c_s, mg, wa, wp, ps, wc, wo)


SAMPLE_ROWS = 8
SAMPLE_NB = 4


def _cmp_partial_kernel(xt_ref, pe_ref, w1_ref, o_ref, x_scr, *, n_ch):
    pages, _, page = xt_ref.shape
    n_part = 4 if pages % 4 == 0 else 1
    pp, pc = pages // n_part, n_ch // n_part
    for part in range(n_part):
        for p in range(part * pp, (part + 1) * pp):
            x_scr[p * page:(p + 1) * page, :] = xt_ref[p].T
        first = jnp.zeros((pc, KV_W), F32)
        second = jnp.zeros((pc, KV_W), F32)
        for l in range(CMP_STRIDE):
            xl = x_scr[pl.ds(part * pp * page + l, pc, stride=CMP_STRIDE), :]
            first += _dot((xl + pe_ref[l:l + 1, :]).astype(BF16), w1_ref[l])
            second += _dot((xl + pe_ref[CMP_STRIDE + l:CMP_STRIDE + l + 1, :]).astype(BF16),
                           w1_ref[CMP_STRIDE + l])
        o_ref[part * pc:(part + 1) * pc, 0:KV_W] = first
        o_ref[part * pc:(part + 1) * pc, KV_W:2 * KV_W] = second


def _cmp_partial(cache_t, pe2, w1bd, layer, n_pool):
    page = cache_t.shape[2]
    ch = page // CMP_STRIDE
    pp = next(p for p in (64, 32, 16, 8, 4, 2, 1) if n_pool % p == 0)
    nb = n_pool // pp
    return pl.pallas_call(
        functools.partial(_cmp_partial_kernel, n_ch=pp * ch),
        grid=(nb, 2),
        in_specs=[pl.BlockSpec((pp, KV_W, page), lambda i, s: (layer * nb + i, s, 0)),
                  pl.BlockSpec((None,) + pe2.shape[1:], lambda i, s: (s, 0, 0)),
                  pl.BlockSpec((None,) + w1bd.shape[1:], lambda i, s: (s, 0, 0, 0))],
        out_specs=pl.BlockSpec((pp * ch, 2 * KV_W), lambda i, s: (i, s)),
        out_shape=jax.ShapeDtypeStruct((n_pool * ch, 4 * KV_W), F32),
        scratch_shapes=[pltpu.VMEM((pp * page, KV_W), F32)],
        compiler_params=_params("parallel", "arbitrary"),
        name="cmp_partial",
    )(cache_t, pe2, w1bd)


def _softmax_pair(s_a, vt_a, s_b, v_b):
    m = jnp.maximum(jnp.max(s_a, axis=1, keepdims=True), jnp.max(s_b, axis=1, keepdims=True))
    e_a = jnp.exp(s_a - m)
    e_b = jnp.exp(s_b - m)
    den = jnp.sum(e_a, axis=1, keepdims=True) + jnp.sum(e_b, axis=1, keepdims=True)
    return (_dot_nt(e_a.astype(BF16), vt_a) + _dot(e_b.astype(BF16), v_b)) / den


def _sample_kernel(pt_ref, q_ref, ng_ref, kv_ref, win_ref, wnt_ref, u_ref, z_ref, cb_ref, swin_ref, spool_ref,
                   sconv_ref, cache_ref, fs_ref, w2_ref, kg_ref, cw_ref, smap_ref, emat_ref, bc_ref, dsel_ref, dwin_ref,
                   dnew_ref, attn_ref, d_ref, c_ref, owin_ref, opool_ref, oconv_ref,
                   kvbuf, fsbuf, sems, ext_scr, *, layer_base, n_pages, page, s_len, past):
    i = pl.program_id(0)
    slot = i % 2
    ch = page // CMP_STRIDE
    sr = SAMPLE_ROWS

    def copies(step, slot_):
        out = []
        for sb in range(SAMPLE_NB):
            for pg in range(n_pages):
                pid = pt_ref[step * SAMPLE_NB + sb, pg]
                out.append(pltpu.make_async_copy(
                    cache_ref.at[layer_base + pid, pl.ds(2 * KV_W, 2 * KV_W), :],
                    kvbuf.at[slot_, sb, :, pl.ds(pg * page, page)], sems.at[slot_, 0]))
                out.append(pltpu.make_async_copy(
                    fs_ref.at[pid], fsbuf.at[slot_, sb, pl.ds(pg * ch, ch), :], sems.at[slot_, 1]))
        return out

    @pl.when(i == 0)
    def _():
        for c in copies(0, 0):
            c.start()

    @pl.when(i + 1 < pl.num_programs(0))
    def _():
        for c in copies(i + 1, 1 - slot):
            c.start()

    for c in copies(i, slot):
        c.wait()

    lane = lax.broadcasted_iota(jnp.int32, (sr, LANES), 1)
    grp_rows = N_KV_HEADS * sr
    q_all = q_ref[...].astype(F32)
    sig_all = jax.nn.sigmoid(ng_ref[...])
    kvn = kv_ref[...]
    winn = win_ref[...]
    pad_rows = jnp.zeros((LANES - sr, KV_W), F32)
    w_keep = swin_ref.shape[2]
    wlane = lax.broadcasted_iota(jnp.int32, swin_ref.shape[1:], 1)
    hist = spool_ref.shape[1]
    ck = sconv_ref.shape[1]
    attn_rows, d_rows, c_rows = [], [], []

    every = range(SAMPLE_NB)

    def new_rows(x8):
        return jnp.concatenate([x8, pad_rows], axis=0).astype(BF16)

    lhs, ks_new, vs_new, kw_new, vw_new = [], [], [], [], []
    for sb in every:
        r0 = sb * sr
        q8 = q_all[r0:r0 + sr]
        pieces = []
        for h in range(HPG):
            for g in range(N_KV_HEADS):
                pieces.append(jnp.where((lane >= HEAD_DIM) == (g == 1), q8[:, h * LANES:(h + 1) * LANES], 0.0))
        lhs.append(jnp.concatenate(pieces, axis=0).astype(BF16))
        ks_new.append(new_rows(kvn[r0:r0 + sr, 2 * KV_W:3 * KV_W]))
        vs_new.append(new_rows(kvn[r0:r0 + sr, 3 * KV_W:4 * KV_W]))
        kw_new.append(new_rows(winn[r0:r0 + sr, 0:KV_W]))
        vw_new.append(new_rows(winn[r0:r0 + sr, KV_W:2 * KV_W]))

    n_ch = fsbuf.shape[2]
    cmp = []
    for s_ in range(2):
        hid = jnp.concatenate(
            [fsbuf[slot, sb, :, 2 * s_ * KV_W:(2 * s_ + 1) * KV_W]
             + pltpu.roll(fsbuf[slot, sb, :, (2 * s_ + 1) * KV_W:(2 * s_ + 2) * KV_W], n_ch - 1, 0) for sb in every],
            axis=0)
        cmp.append(_dot((hid * jax.nn.sigmoid(hid)).astype(BF16), w2_ref[s_]))
    kc_all = _head_rms(cmp[0], kg_ref[...]).astype(BF16)
    vc_all = cmp[1].astype(BF16)
    p_c, o_c = [], []
    for sb in every:
        s_c = _dot_nt(lhs[sb], kc_all[sb * n_ch:(sb + 1) * n_ch]) + bc_ref[...]
        e_c = jnp.exp(s_c - jnp.max(s_c, axis=1, keepdims=True))
        p_c.append(e_c / jnp.sum(e_c, axis=1, keepdims=True))
        o_c.append(_dot(p_c[sb].astype(BF16), vc_all[sb * n_ch:(sb + 1) * n_ch]))

    o_w = [_softmax_pair(_dot(lhs[sb], swin_ref[sb, 0:KV_W, :].astype(BF16)) + dwin_ref[...],
                         swin_ref[sb, KV_W:2 * KV_W, :].astype(BF16),
                         _dot_nt(lhs[sb], kw_new[sb]) + dnew_ref[...], vw_new[sb]) for sb in every]

    imp = jnp.concatenate([sum(p_c[sb][h * grp_rows:(h + 1) * grp_rows] for h in range(HPG)) for sb in every], axis=0)
    score = _dot_split(imp, smap_ref[...])
    jcol = lax.broadcasted_iota(jnp.int32, score.shape, 1)
    cur = (past + lax.broadcasted_iota(jnp.int32, score.shape, 0) % sr) // SEL_BLOCK
    forced = (jcol == 0) | (jcol == cur) | (jcol == cur - 1)
    score = jnp.where(forced, jnp.inf, score)
    score = jnp.where(jcol <= cur, score, NEG_INF)
    not_sel_all = (_top_k_mask(score, TOP_K) - 1.0) * MASK_BIG

    o_s = []
    for sb in every:
        not_sel = jnp.concatenate([not_sel_all[sb * grp_rows:(sb + 1) * grp_rows]] * HPG, axis=0).astype(BF16)
        lhs_sel = jnp.concatenate([lhs[sb], not_sel], axis=1)
        k_past = jnp.concatenate([kvbuf[slot, sb, 0:KV_W, :].astype(BF16), emat_ref[...]], axis=0)
        o_s.append(_softmax_pair(_dot(lhs_sel, k_past) + dsel_ref[...],
                                 kvbuf[slot, sb, KV_W:2 * KV_W, :].astype(BF16),
                                 _dot_nt(lhs[sb], ks_new[sb]) + dnew_ref[...], vs_new[sb]))

    for sb in every:
        r0 = sb * sr
        branch = (o_c[sb], o_s[sb], o_w[sb])
        sig8 = sig_all[r0:r0 + sr]
        slabs = []
        for h in range(HPG):
            slab = None
            for j in range(3):
                part = []
                for g in range(N_KV_HEADS):
                    c = j * N_Q_HEADS + g * HPG + h
                    lo = h * grp_rows + g * sr
                    part.append(sig8[:, c:c + 1] * branch[j][lo:lo + sr])
                term = jnp.where(lane < HEAD_DIM, part[0], part[1])
                slab = term if slab is None else slab + term
            slabs.append(slab)
        attn_rows.append(jnp.concatenate(slabs, axis=1))

        state = pltpu.roll(swin_ref[sb], w_keep - s_len, 1)
        for tt in range(s_len):
            state = jnp.where(wlane == w_keep - s_len + tt, wnt_ref[sb, :, tt:tt + 1], state)
        owin_ref[sb] = state

        u8 = u_ref[r0:r0 + sr, :]
        ext_scr[0:hist, :] = spool_ref[sb]
        ext_scr[hist:hist + sr, :] = u8
        d_rows.append(_pool_rows(ext_scr, hist, u8, hist + lax.broadcasted_iota(jnp.int32, u8.shape, 0)))
        opool_ref[sb] = ext_scr[s_len:s_len + hist, :]
        ext_scr[0:ck, :] = sconv_ref[sb]
        ext_scr[ck:ck + sr, :] = z_ref[r0:r0 + sr, :]
        c_rows.append(cb_ref[r0:r0 + sr, :] * _conv_rows(ext_scr, ck, sr, cw_ref))
        oconv_ref[sb] = ext_scr[s_len:s_len + ck, :]

    attn_ref[...] = jnp.concatenate(attn_rows, axis=0).astype(BF16)
    d_ref[...] = jnp.concatenate(d_rows, axis=0).astype(BF16)
    c_ref[...] = jnp.concatenate(c_rows, axis=0).astype(BF16)


def _sample_layer(layer, page_table, q, ng, kv4, win, wnt, u, z, cb, swin3, spool3, sconv3, cache3, fs3, w2bd, kgain,
                  conv_w, smap_s, emat, bc_s, dsel, dwin, dnew, n_p, db, s_len, n_pool):
    nsteps = db // SAMPLE_NB
    rb = SAMPLE_NB * SAMPLE_ROWS
    r0 = n_p // rb
    page = cache3.shape[2]
    n_pages = page_table.shape[1]
    past = n_pages * page
    tok = lambda a: pl.BlockSpec((rb, a.shape[1]), lambda i, pt: (r0 + i, 0))
    new_t = pl.BlockSpec((SAMPLE_NB,) + wnt.shape[1:], lambda i, pt: (i, 0, 0))
    state = lambda a: pl.BlockSpec((SAMPLE_NB,) + a.shape[1:], lambda i, pt: (layer * nsteps + i, 0, 0))
    full = lambda a: pl.BlockSpec(a.shape, lambda i, pt: (0,) * a.ndim)
    hbm = pl.BlockSpec(memory_space=pl.ANY)
    out_tok = lambda w_: pl.BlockSpec((rb, w_), lambda i, pt: (i, 0))
    out_state = lambda a: pl.BlockSpec((SAMPLE_NB,) + a.shape[1:], lambda i, pt: (i, 0, 0))
    width = u.shape[1]
    n_tok = db * SAMPLE_ROWS
    return pl.pallas_call(
        functools.partial(_sample_kernel, layer_base=layer * n_pool, n_pages=n_pages, page=page, s_len=s_len, past=past),
        grid_spec=pltpu.PrefetchScalarGridSpec(
            num_scalar_prefetch=1,
            grid=(nsteps,),
            in_specs=[tok(q), tok(ng), tok(kv4), tok(win), new_t, tok(u), tok(z), tok(cb), state(swin3), state(spool3),
                      state(sconv3), hbm, hbm, full(w2bd), full(kgain), full(conv_w), full(smap_s), full(emat),
                      full(bc_s), full(dsel), full(dwin), full(dnew)],
            out_specs=[out_tok(NSA_W), out_tok(width), out_tok(width), out_state(swin3), out_state(spool3),
                       out_state(sconv3)],
            scratch_shapes=[pltpu.VMEM((2, SAMPLE_NB, 2 * KV_W, past), F32),
                            pltpu.VMEM((2, SAMPLE_NB, past // CMP_STRIDE, 4 * KV_W), F32),
                            pltpu.SemaphoreType.DMA((2, 2)),
                            pltpu.VMEM((4 * SAMPLE_ROWS, width), F32)]),
        out_shape=[jax.ShapeDtypeStruct((n_tok, NSA_W), BF16), jax.ShapeDtypeStruct((n_tok, width), BF16),
                   jax.ShapeDtypeStruct((n_tok, width), BF16), jax.ShapeDtypeStruct((db,) + swin3.shape[1:], F32),
                   jax.ShapeDtypeStruct((db,) + spool3.shape[1:], F32),
                   jax.ShapeDtypeStruct((db,) + sconv3.shape[1:], F32)],
        compiler_params=_params("arbitrary"),
        name="sample_mix",
    )(page_table, q, ng, kv4, win, wnt, u, z, cb, swin3, spool3, sconv3, cache3, fs3, w2bd, kgain, conv_w, smap_s,
      emat, bc_s, dsel, dwin, dnew)


def _bucket_table(n):
    d = np.arange(n)
    nf = np.maximum(d, 1).astype(np.float32)
    large = MAX_EXACT + (np.log(nf / np.float32(MAX_EXACT)) / np.float32(math.log(MAX_DISTANCE / MAX_EXACT))
                         * np.float32(N_BUCKETS - MAX_EXACT)).astype(np.int32)
    return np.where(d < MAX_EXACT, d, np.minimum(large, N_BUCKETS - 1)).astype(np.int32)


def _bias_values(rel_bias, dist, visible=None):
    r, c = dist.shape
    offset = int(dist[0, 0])
    stride = int(dist[0, 0] - dist[0, 1]) if c > 1 else 1
    assert stride >= 1 and np.array_equal(dist, offset + np.arange(r)[:, None] - stride * np.arange(c)[None, :])
    visible = (dist >= 0) if visible is None else (visible & (dist >= 0))
    cw = stride * (c - 1) + 1
    period = r + cw
    m = np.arange(period)
    by_wrap = offset + np.where(m < cw, -m, period - m)
    bkt = _bucket_table(max(int(by_wrap.max()) + 1, MAX_DISTANCE))[np.clip(by_wrap, 0, None)]
    tab = rel_bias - rel_bias[N_BUCKETS - 1:N_BUCKETS]
    per_dist = tab[jnp.asarray(bkt)].T
    vals = jnp.tile(per_dist, (1, r))[:, :r * (period - 1)].reshape(-1, r, period - 1)[:, :, :cw:stride]
    vals = jnp.where(jnp.asarray(visible)[None], vals, NEG_INF)
    return vals.reshape((N_KV_HEADS, HPG) + dist.shape)


def _bias_table(rel_bias, dist):
    r, c = dist.shape
    return _bias_values(rel_bias, dist).reshape(N_KV_HEADS, HPG * r, c)


def _bias_table_sample(rel_bias, dist, visible=None):
    r, c = dist.shape
    return _bias_values(rel_bias, dist, visible).transpose(1, 0, 2, 3).reshape(HPG * N_KV_HEADS * r, c)


def _sel_map(n_cmp, n_rows, n_cols, pad):
    r_sel = SEL_BLOCK // CMP_STRIDE
    r_cmp = CMP_BLOCK // CMP_STRIDE
    out = np.zeros((n_rows, n_cols), np.float32)
    for j in range(n_cols):
        for m in range(r_sel):
            for n in range(r_cmp):
                c = r_sel * j + m - n
                if 0 <= c < n_cmp:
                    out[pad + c, j] += 1.0
    return out


def _slab_order(w, axis):
    shape = w.shape
    w = w.reshape(shape[:axis] + (N_KV_HEADS, HPG, HEAD_DIM) + shape[axis + 1:])
    return jnp.swapaxes(w, axis, axis + 1).reshape(shape)


def _proj_weight(w_in):
    kv0, ng0 = NSA_W, NSA_W + 6 * KV_W
    rest0 = ng0 + 3 * N_Q_HEADS
    slot = lambda s: w_in[:, :, kv0 + s * KV_W:kv0 + (s + 1) * KV_W]
    ng = w_in[:, :, ng0:rest0].reshape(w_in.shape[:2] + (N_Q_HEADS, 3))
    ng = jnp.swapaxes(ng, 2, 3).reshape(w_in.shape[:2] + (3 * N_Q_HEADS,))
    ng = jnp.pad(ng, ((0, 0), (0, 0), (0, C_END - C_NG - 3 * N_Q_HEADS)))
    parts = [_slab_order(w_in[:, :, :NSA_W], 2), slot(2), slot(4), slot(0), slot(1), slot(3), slot(5),
             w_in[:, :, rest0:], ng]
    out = jnp.concatenate(parts, axis=2)
    assert out.shape[2] == C_END
    return out


def _block_diag2(w):
    z = jnp.zeros_like(w)
    return jnp.concatenate([jnp.concatenate([w, z], axis=-1), jnp.concatenate([z, w], axis=-1)], axis=-2)


def _token_tile(n):
    for tm in (512, 256, 128):
        if n % tm == 0:
            return tm
    raise ValueError(f"token count {n} is not a multiple of 128")


def kernel(x_prompt, x_sample, cache_kv, state_win, state_pool, state_conv, page_table, rel_bias, ln_ffn1, w_ffn1_gate, w_ffn1_up, w_ffn1_down, ln_mix, w_in, q_norm, k_norm, cmp_pe, cmp_w1, cmp_w2, w_nsa_out, w_pool, pool_scale, conv_w, w_conv_out, w_o, ln_ffn2, w_ffn2_gate, w_ffn2_up, w_ffn2_down):
    b, t, dm = x_prompt.shape
    db, s, _ = x_sample.shape
    depth, n_pool, page = cache_kv.shape[:3]
    n_pages = page_table.shape[1]
    past = n_pages * page
    w_keep = state_win.shape[2]
    sr = SAMPLE_ROWS
    n_p, n_s = b * t, db * sr
    assert t % Q_BLOCK == 0 and t >= WINDOW and t // SEL_BLOCK <= LANES
    assert s <= sr and db % SAMPLE_NB == 0 and n_p % (SAMPLE_NB * sr) == 0
    assert (past + s) // CMP_STRIDE == past // CMP_STRIDE == LANES and w_keep <= past and s <= w_keep
    assert -(-(past + s) // SEL_BLOCK) <= LANES
    tm = _token_tile(n_p + n_s)
    tm_proj = min(tm, 256)
    tm_mix = _token_tile(t)

    w_proj = _proj_weight(w_in).astype(BF16)
    head_gain = jnp.concatenate([jnp.tile(q_norm, (1, N_Q_HEADS)), jnp.tile(k_norm[:, 1], (1, N_KV_HEADS)),
                                 jnp.tile(k_norm[:, 2], (1, N_KV_HEADS))], axis=1).reshape(depth, 1, N_NORMED)
    cmp_gain = jnp.tile(k_norm[:, 0], (1, N_KV_HEADS)).reshape(depth, 1, KV_W)
    pe2 = jnp.tile(cmp_pe, (1, 1, 1, N_KV_HEADS))
    w1bd = _block_diag2(cmp_w1).astype(BF16)
    w2bd = _block_diag2(cmp_w2).astype(BF16)
    wa = _slab_order(w_nsa_out, axis=1).astype(BF16)
    n_grp, gc, ge = w_pool.shape[1:]
    wp = jnp.zeros((depth, n_grp * gc, n_grp * ge), F32)
    for gi in range(n_grp):
        wp = wp.at[:, gi * gc:(gi + 1) * gc, gi * ge:(gi + 1) * ge].set(w_pool[:, gi])
    wp = wp.astype(BF16)
    wc = w_conv_out.astype(BF16)
    wo = w_o.astype(BF16)
    ffn1 = [w.astype(BF16) for w in (w_ffn1_gate, w_ffn1_up, w_ffn1_down)]
    ffn2 = [w.astype(BF16) for w in (w_ffn2_gate, w_ffn2_up, w_ffn2_down)]

    ti = np.arange(Q_BLOCK)
    d01 = _bias_table(rel_bias, Q_BLOCK + ti[:, None] - np.arange(2 * Q_BLOCK)[None, :])
    near_c = np.arange(LANES)
    bc = _bias_table(rel_bias, ti[:, None] - CMP_STRIDE * (near_c[None, :] - CMP_PAD) - (CMP_BLOCK - 1))
    n_cmp = t // CMP_STRIDE - CMP_BLOCK // CMP_STRIDE + 1
    smap = jnp.asarray(_sel_map(n_cmp, _cmp_rows(t), LANES, CMP_PAD))
    wj = np.arange(WINDOW + Q_BLOCK)[None, :]
    dist_pw = WINDOW + ti[:, None] - wj
    wtab = _bias_values(rel_bias, dist_pw, dist_pw < WINDOW).reshape(N_KV_HEADS, HPG * Q_BLOCK, WINDOW + Q_BLOCK)
    key_pos = np.arange(-Q_BLOCK, t + SEL_CHUNK)[:, None]
    emat_p = jnp.asarray((key_pos >= 0) & (key_pos < t)
                         & (key_pos // SEL_BLOCK == np.arange(LANES)[None, :])).astype(BF16)

    tq = np.arange(sr)[:, None]
    n_chunks = past // CMP_STRIDE
    n_cmp_s = n_chunks - CMP_BLOCK // CMP_STRIDE + 1
    cn = np.arange(n_chunks)[None, :]
    bc_s = _bias_table_sample(rel_bias, past + tq - (CMP_STRIDE * cn + CMP_BLOCK - 1), cn < n_cmp_s)
    dsel = _bias_table_sample(rel_bias, past + tq - np.arange(past)[None, :])
    tn = np.arange(LANES)[None, :]
    dnew = _bias_table_sample(rel_bias, tq - tn, tn < s)
    dist_w = w_keep + tq - np.arange(w_keep)[None, :]
    dwin = _bias_table_sample(rel_bias, dist_w, dist_w < WINDOW)
    smap_s = jnp.asarray(_sel_map(n_cmp_s, n_chunks, LANES, 0)).astype(BF16)
    emat = jnp.asarray(np.arange(LANES)[:, None] == np.arange(past)[None, :] // SEL_BLOCK).astype(BF16)

    cache3 = cache_kv.transpose(0, 1, 3, 4, 5, 2).reshape(depth * n_pool, -1, page)
    swin3 = state_win.transpose(0, 1, 3, 4, 5, 2).reshape(depth * db, -1, w_keep)
    spool3 = state_pool.reshape((depth * db,) + state_pool.shape[2:])
    sconv3 = state_conv.reshape((depth * db,) + state_conv.shape[2:])

    xs = jnp.pad(x_sample, ((0, 0), (0, sr - s), (0, 0)))
    x = jnp.concatenate([x_prompt.reshape(n_p, dm), xs.reshape(n_s, dm)], axis=0)
    outs = [[] for _ in range(8)]
    for l in range(depth):
        x = _ffn(x, ln_ffn1[l], ffn1[0][l], ffn1[1][l], ffn1[2][l], tm)
        q, kv4, win, ksv, u, cb, z, mg, ng = _proj(x, ln_mix[l], w_proj[l], head_gain[l], tm_proj)

        kc, vc = _compress_prompt(kv4, pe2[l], w1bd[l], w2bd[l], cmp_gain[l], b, t)
        kall = jnp.pad(ksv[:n_p].reshape(b, t, 4 * KV_W), ((0, 0), (WINDOW, SEL_CHUNK), (0, 0)))
        attn_p = _attn_prompt(q, ng, kc, vc, kall, emat_p, smap, d01, bc, wtab, b, t)
        d_p, c_p = _mix_prompt(u, z, cb, conv_w[l], b, t, tm_mix)

        fs3 = _cmp_partial(cache3, pe2[l], w1bd[l], l, n_pool).reshape(n_pool, page // CMP_STRIDE, -1)
        wnt = win[n_p:].reshape(db, sr, -1).transpose(0, 2, 1)
        attn_s, d_s, c_s, s_win, s_pool, s_conv = _sample_layer(
            l, page_table, q, ng, kv4, win, wnt, u, z, cb, swin3, spool3, sconv3, cache3, fs3, w2bd[l], cmp_gain[l],
            conv_w[l], smap_s, emat, bc_s, dsel, dwin, dnew, n_p, db, s, n_pool)

        x = _merge(x, attn_p, attn_s, d_p, d_s, c_p, c_s, mg, wa[l], wp[l], pool_scale[l], wc[l], wo[l], tm)
        x = _ffn(x, ln_ffn2[l], ffn2[0][l], ffn2[1][l], ffn2[2][l], tm)

        p_keep = min(WINDOW, t)
        outs[0].append(kv4[:n_p].reshape(b, t, 4, N_KV_HEADS, HEAD_DIM))
        outs[1].append(win[:n_p].reshape(b, t, 2, N_KV_HEADS, HEAD_DIM)[:, t - p_keep:])
        outs[2].append(u[:n_p].reshape(b, t, -1)[:, t - POOL_STATE:])
        outs[3].append(z[:n_p].reshape(b, t, -1)[:, t - (CONV_K - 1):])
        outs[4].append(kv4[n_p:].reshape(db, sr, 4, N_KV_HEADS, HEAD_DIM)[:, :s])
        outs[5].append(s_win.reshape(db, 2, N_KV_HEADS, HEAD_DIM, w_keep).transpose(0, 4, 1, 2, 3))
        outs[6].append(s_pool)
        outs[7].append(s_conv)

    y_prompt = x[:n_p].reshape(b, t, dm)
    y_sample = x[n_p:].reshape(db, sr, dm)[:, :s]
    return (y_prompt, y_sample) + tuple(jnp.stack(o) for o in outs)
```

```python
import functools
import math

import numpy as np
import jax
import jax.numpy as jnp
from jax import lax
from jax.experimental import pallas as pl
from jax.experimental.pallas import tpu as pltpu

F32 = jnp.float32
BF16 = jnp.bfloat16
NEG_INF = float("-inf")

HEAD_DIM = 64
N_KV_HEADS = 2
HPG = 4
N_Q_HEADS = N_KV_HEADS * HPG
NSA_W = N_Q_HEADS * HEAD_DIM
KV_W = N_KV_HEADS * HEAD_DIM
CMP_BLOCK = 32
CMP_STRIDE = 16
SEL_BLOCK = 64
TOP_K = 8
WINDOW = 512
Q_BLOCK = 128
N_BUCKETS = 32
MAX_EXACT = 16
MAX_DISTANCE = 128
POOL_WINDOWS = (2, 4, 8, 16)
POOL_STATE = 15
CONV_K = 3
RMS_EPS = 1e-6
LANES = 128
HALO = 16
CMP_PAD = 112
SEL_CHUNK = 512
MASK_BIG = 1e30
VMEM_LIMIT = 56 * 1024 * 1024


def _dot(a, b):
    return jnp.dot(a, b, preferred_element_type=F32)


def _dot_nt(a, b):
    return lax.dot_general(a, b, (((1,), (1,)), ((), ())), preferred_element_type=F32)


def _dot_split(a, b):
    hi = a.astype(BF16)
    lo = (a - hi.astype(F32)).astype(BF16)
    return _dot(hi, b) + _dot(lo, b)


def _group_mean_sq(x):
    r = lax.broadcasted_iota(jnp.int32, (LANES, LANES), 0) // HEAD_DIM
    c = lax.broadcasted_iota(jnp.int32, (LANES, LANES), 1) // HEAD_DIM
    ones_bd = (r == c).astype(BF16)
    return _dot_split(x * x, ones_bd) * (1.0 / HEAD_DIM)


def _head_rms(x, gain):
    parts = []
    for k in range(x.shape[1] // LANES):
        xs = x[:, k * LANES:(k + 1) * LANES]
        parts.append(xs * lax.rsqrt(_group_mean_sq(xs) + RMS_EPS))
    y = parts[0] if len(parts) == 1 else jnp.concatenate(parts, axis=1)
    return y * gain


def _row_rms(x, gain):
    return x * lax.rsqrt(jnp.mean(x * x, axis=-1, keepdims=True) + RMS_EPS) * gain


def _params(*sem):
    return pltpu.CompilerParams(dimension_semantics=sem, vmem_limit_bytes=VMEM_LIMIT)


def _ffn_kernel(x_ref, g_ref, wg_ref, wu_ref, wd_ref, o_ref, *, n_split):
    x = x_ref[...]
    h = _row_rms(x, g_ref[...]).astype(BF16)
    tf = wg_ref.shape[1] // n_split
    acc = None
    for f in range(n_split):
        a = _dot(h, wg_ref[:, f * tf:(f + 1) * tf])
        b = _dot(h, wu_ref[:, f * tf:(f + 1) * tf])
        part = _dot((a * jax.nn.sigmoid(a) * b).astype(BF16), wd_ref[f * tf:(f + 1) * tf, :])
        acc = part if acc is None else acc + part
    o_ref[...] = x + 0.5 * acc


def _ffn(x, gain, wg, wu, wd, tm):
    n, d = x.shape
    dff = wg.shape[1]
    n_split = 2
    assert dff % (n_split * LANES) == 0
    resident = lambda a: pl.BlockSpec(a.shape, lambda i: (0, 0), pipeline_mode=pl.Buffered(1))
    return pl.pallas_call(
        functools.partial(_ffn_kernel, n_split=n_split),
        grid=(n // tm,),
        in_specs=[pl.BlockSpec((tm, d), lambda i: (i, 0)), pl.BlockSpec((1, d), lambda i: (0, 0)),
                  resident(wg), resident(wu), resident(wd)],
        out_specs=pl.BlockSpec((tm, d), lambda i: (i, 0)),
        out_shape=jax.ShapeDtypeStruct((n, d), F32),
        compiler_params=_params("parallel"),
        name="ffn",
    )(x, gain.reshape(1, d), wg, wu, wd)


C_Q = 0
C_KSEL = 512
C_KWIN = 640
C_KCMP = 768
C_VCMP = 896
C_VSEL = 1024
C_VWIN = 1152
C_U = 1280
C_CB = 1536
C_CC = 1792
C_CH = 2048
C_MG = 2304
C_NG = 5376
C_END = 5504
N_NORMED = 768


def _proj_kernel(x_ref, g_ref, w_ref, hg_ref, q_ref, kv_ref, win_ref, ksv_ref, u_ref, cb_ref, z_ref, ng_ref):
    h = _row_rms(x_ref[...], g_ref[...]).astype(BF16)

    def sec(lo, hi):
        return _dot(h, w_ref[:, lo:hi])

    nrm = _head_rms(sec(0, N_NORMED), hg_ref[...])
    ksel = nrm[:, C_KSEL:C_KSEL + KV_W]
    kwin = nrm[:, C_KWIN:C_KWIN + KV_W]
    q_ref[...] = (nrm[:, :NSA_W] * (HEAD_DIM ** -0.5)).astype(BF16)
    rest = sec(C_KCMP, C_U)
    vsel = rest[:, 2 * KV_W:3 * KV_W]
    vwin = rest[:, 3 * KV_W:4 * KV_W]
    kv_ref[:, 0:2 * KV_W] = rest[:, 0:2 * KV_W]
    kv_ref[:, 2 * KV_W:3 * KV_W] = ksel
    kv_ref[:, 3 * KV_W:4 * KV_W] = vsel
    win_ref[:, 0:KV_W] = kwin
    win_ref[:, KV_W:2 * KV_W] = vwin
    ksv_ref[:, 0:KV_W] = ksel.astype(BF16)
    ksv_ref[:, KV_W:2 * KV_W] = vsel.astype(BF16)
    ksv_ref[:, 2 * KV_W:3 * KV_W] = kwin.astype(BF16)
    ksv_ref[:, 3 * KV_W:4 * KV_W] = vwin.astype(BF16)
    u_ref[...] = sec(C_U, C_CB)
    cv = sec(C_CB, C_MG)
    cw = C_CC - C_CB
    cb_ref[...] = cv[:, 0:cw]
    z_ref[...] = cv[:, cw:2 * cw] * cv[:, 2 * cw:3 * cw]
    ng_ref[...] = sec(C_NG, C_END)


def _proj(x, gain, w, head_gain, tm):
    n, d = x.shape
    row = lambda w_: pl.BlockSpec((tm, w_), lambda i: (i, 0))
    full = lambda a: pl.BlockSpec(a.shape, lambda i: (0, 0))
    gain = gain.reshape(1, d)
    widths = ((NSA_W, BF16), (4 * KV_W, F32), (2 * KV_W, F32), (4 * KV_W, BF16), (C_CB - C_U, F32),
              (C_CC - C_CB, F32), (C_CC - C_CB, F32), (C_END - C_NG, F32))
    return pl.pallas_call(
        _proj_kernel,
        grid=(n // tm,),
        in_specs=[row(d), full(gain), pl.BlockSpec(w.shape, lambda i: (0, 0), pipeline_mode=pl.Buffered(1)),
                  full(head_gain)],
        out_specs=[row(w_) for w_, _ in widths],
        out_shape=[jax.ShapeDtypeStruct((n, w_), dt) for w_, dt in widths],
        compiler_params=_params("parallel"),
        name="proj",
    )(x, gain, w, head_gain)


def _compress_kernel(k_ref, v_ref, pe_ref, w1_ref, w2_ref, kg_ref, kc_ref, vc_ref, *, n_ch):
    outs = []
    for s, src_ref in enumerate((k_ref, v_ref)):
        first = jnp.zeros((n_ch, KV_W), F32)
        second = jnp.zeros((n_ch, KV_W), F32)
        for l in range(CMP_STRIDE):
            xl = src_ref[pl.ds(l, n_ch, stride=CMP_STRIDE), :]
            first += _dot((xl + pe_ref[s, l:l + 1, :]).astype(BF16), w1_ref[s, l])
            second += _dot((xl + pe_ref[s, CMP_STRIDE + l:CMP_STRIDE + l + 1, :]).astype(BF16),
                           w1_ref[s, CMP_STRIDE + l])
        hid = first + pltpu.roll(second, n_ch - 1, 0)
        act = hid * jax.nn.sigmoid(hid)
        outs.append(_dot(act.astype(BF16), w2_ref[s]))
    kc = _head_rms(outs[0], kg_ref[...])
    rows = kc_ref.shape[0]
    for ref, val in ((kc_ref, kc), (vc_ref, outs[1])):
        ref[0:CMP_PAD, :] = jnp.zeros((CMP_PAD, KV_W), F32)
        ref[CMP_PAD:CMP_PAD + n_ch, :] = val
        ref[CMP_PAD + n_ch:rows, :] = jnp.zeros((rows - CMP_PAD - n_ch, KV_W), F32)


def _cmp_rows(t):
    return -(-(CMP_PAD + t // CMP_STRIDE) // LANES) * LANES


def _compress_prompt(kv4, pe2, w1bd, w2bd, kgain, b, t):
    n_ch = t // CMP_STRIDE
    rows = _cmp_rows(t)
    full = lambda a: pl.BlockSpec(a.shape, lambda i: (0,) * a.ndim)
    out = jax.ShapeDtypeStruct((b, rows, KV_W), F32)
    return pl.pallas_call(
        functools.partial(_compress_kernel, n_ch=n_ch),
        grid=(b,),
        in_specs=[pl.BlockSpec((t, KV_W), lambda i: (i, 0)), pl.BlockSpec((t, KV_W), lambda i: (i, 1)),
                  full(pe2), full(w1bd), full(w2bd), full(kgain)],
        out_specs=[pl.BlockSpec((None, rows, KV_W), lambda i: (i, 0, 0))] * 2,
        out_shape=[out, out],
        compiler_params=_params("parallel"),
        name="compress_prompt",
    )(kv4, kv4, pe2, w1bd, w2bd, kgain)


def _lane_tiles(x):
    return [x[:, k * LANES:(k + 1) * LANES] for k in range(x.shape[1] // LANES)]


def _softmax_lane_tiles(tiles):
    m = functools.reduce(jnp.maximum, tiles)
    m = jnp.broadcast_to(jnp.max(m, axis=1, keepdims=True), m.shape)
    m = jnp.where(m == NEG_INF, 0.0, m)
    e = [jnp.exp(x - m) for x in tiles]
    den = functools.reduce(jnp.add, e)
    den = jnp.broadcast_to(jnp.sum(den, axis=1, keepdims=True), den.shape)
    return e, 1.0 / jnp.where(den > 0, den, 1.0)


def _top_k_mask(score, k):
    col = lax.broadcasted_iota(jnp.int32, score.shape, 1).astype(F32)
    sel = jnp.zeros(score.shape, F32)
    work = score
    for _ in range(k):
        m = jnp.max(work, axis=1, keepdims=True)
        first = jnp.min(jnp.where(work == m, col, float(score.shape[1])), axis=1, keepdims=True)
        pick = col == first
        sel = jnp.maximum(sel, pick.astype(F32))
        work = jnp.where(pick, NEG_INF, work)
    return sel


def _attn_prompt_kernel(q_ref, ng_ref, kc_ref, vc_ref, kall_ref, emat_ref, smap_ref, d01_ref, bc_ref, wtab_ref, o_ref,
                        m_scr, acc_scr, ob_scr):
    i = pl.program_id(1)
    lane = lax.broadcasted_iota(jnp.int32, (Q_BLOCK, LANES), 1)
    near0 = pl.multiple_of(i * (Q_BLOCK // CMP_STRIDE), 8)
    own = pl.multiple_of(i * Q_BLOCK, Q_BLOCK)
    scores, lhs_g = [], []

    for g in range(N_KV_HEADS):
        in_group = (lane >= HEAD_DIM) == (g == 1)
        lhs = jnp.concatenate(
            [jnp.where(in_group, q_ref[:, h * LANES:(h + 1) * LANES], jnp.zeros((), BF16)) for h in range(HPG)], axis=0)

        s_far = _dot_nt(lhs, kc_ref[...].astype(BF16))
        s_near = _dot_nt(lhs, kc_ref[pl.ds(near0, LANES), :].astype(BF16)) + bc_ref[g]
        pcol = lax.broadcasted_iota(jnp.int32, (1, s_far.shape[1]), 1)
        s_far = s_far + jnp.where((pcol >= CMP_PAD) & (pcol < near0), 0.0, NEG_INF)
        ncol = lax.broadcasted_iota(jnp.int32, (1, LANES), 1)
        s_near = s_near + jnp.where(near0 + ncol >= CMP_PAD, 0.0, NEG_INF)
        e_c, inv_c = _softmax_lane_tiles(_lane_tiles(s_far) + [s_near])
        p_far = jnp.concatenate([e * inv_c for e in e_c[:-1]], axis=1)
        p_near = e_c[-1] * inv_c
        ob_scr[g, 0] = (_dot(p_far.astype(BF16), vc_ref[...].astype(BF16))
                        + _dot(p_near.astype(BF16), vc_ref[pl.ds(near0, LANES), :].astype(BF16)))

        imp_far = sum(p_far[h * Q_BLOCK:(h + 1) * Q_BLOCK] for h in range(HPG))
        imp_near = sum(p_near[h * Q_BLOCK:(h + 1) * Q_BLOCK] for h in range(HPG))
        score = (_dot_split(imp_far, smap_ref[...].astype(BF16))
                 + _dot_split(imp_near, smap_ref[pl.ds(near0, LANES), :].astype(BF16)))
        jcol = lax.broadcasted_iota(jnp.int32, score.shape, 1)
        cur = 2 * i + (lax.broadcasted_iota(jnp.int32, score.shape, 0) >= SEL_BLOCK).astype(jnp.int32)
        forced = (jcol == 0) | (jcol == cur) | (jcol == cur - 1)
        score = jnp.where(forced, jnp.inf, score)
        scores.append(jnp.where(jcol <= cur, score, NEG_INF))
        lhs_g.append(lhs)

        s_w = _dot_nt(lhs, kall_ref[pl.ds(own, WINDOW + Q_BLOCK), 2 * KV_W:3 * KV_W]) + wtab_ref[g]
        wcol = lax.broadcasted_iota(jnp.int32, s_w.shape, 1)
        s_w = jnp.where(wcol >= WINDOW - own, s_w, NEG_INF)
        e_w, inv_w = _softmax_lane_tiles(_lane_tiles(s_w))
        ob_scr[g, 2] = _dot(jnp.concatenate(e_w, axis=1).astype(BF16),
                            kall_ref[pl.ds(own, WINDOW + Q_BLOCK), 3 * KV_W:4 * KV_W]) * inv_w

    sel_both = _top_k_mask(jnp.concatenate(scores, axis=0), TOP_K)
    n_plain = jnp.maximum(i - 1, 0)
    per_chunk = SEL_CHUNK // Q_BLOCK
    n_full = n_plain // per_chunk
    n_rest = n_plain % per_chunk
    ones_v = jnp.ones((4 * SEL_CHUNK, LANES), BF16)

    for g in range(N_KV_HEADS):
        lhs = lhs_g[g]
        sel = sel_both[g * Q_BLOCK:(g + 1) * Q_BLOCK]
        not_sel = jnp.concatenate([(sel - 1.0) * MASK_BIG] * HPG, axis=0).astype(BF16)
        lhs_sel = jnp.concatenate([lhs, not_sel], axis=1)

        def sel_logits(pos, width):
            keys = jnp.concatenate([kall_ref[pl.ds(WINDOW + pos, width), 0:KV_W],
                                    emat_ref[pl.ds(Q_BLOCK + pos, width), :]], axis=1)
            return _dot_nt(lhs_sel, keys)

        def sel_values(pos, width):
            return jnp.concatenate([kall_ref[pl.ds(WINDOW + pos, width), KV_W:2 * KV_W], ones_v[:width]], axis=1)

        def lane_tile_max(s):
            out = s[:, 0:LANES]
            for k in range(1, s.shape[1] // LANES):
                out = jnp.maximum(out, s[:, k * LANES:(k + 1) * LANES])
            return out

        def weights(s):
            mb = m_scr[...]
            return jnp.concatenate([jnp.exp(s[:, k * LANES:(k + 1) * LANES] - mb)
                                    for k in range(s.shape[1] // LANES)], axis=1).astype(BF16)

        def rest_logits():
            s = sel_logits(pl.multiple_of(n_full * SEL_CHUNK, SEL_CHUNK), SEL_CHUNK)
            col = lax.broadcasted_iota(jnp.int32, s.shape, 1)
            return jnp.where(col < n_rest * Q_BLOCK, s, NEG_INF)

        near = pl.multiple_of(own - Q_BLOCK, Q_BLOCK)
        s_near2 = sel_logits(near, 2 * Q_BLOCK) + d01_ref[g]
        ncol2 = lax.broadcasted_iota(jnp.int32, s_near2.shape, 1)
        s_near2 = jnp.where((i == 0) & (ncol2 < Q_BLOCK), NEG_INF, s_near2)

        def over_plain_keys(fn):
            quad = 4 * SEL_CHUNK

            def quad_body(c, carry):
                pos = pl.multiple_of(c * quad, quad)
                fn(sel_logits(pos, quad), pos, quad)
                return carry

            lax.fori_loop(0, n_full // 4, quad_body, 0)
            left = n_full % 4

            @pl.when(left >= 2)
            def _():
                pos = pl.multiple_of((n_full - left) * SEL_CHUNK, SEL_CHUNK)
                fn(sel_logits(pos, 2 * SEL_CHUNK), pos, 2 * SEL_CHUNK)

            @pl.when(left % 2 == 1)
            def _():
                pos = pl.multiple_of((n_full - 1) * SEL_CHUNK, SEL_CHUNK)
                fn(sel_logits(pos, SEL_CHUNK), pos, SEL_CHUNK)

            @pl.when(n_rest > 0)
            def _():
                fn(rest_logits(), pl.multiple_of(n_full * SEL_CHUNK, SEL_CHUNK), SEL_CHUNK)

        def take_max(s, pos, width):
            m_scr[...] = jnp.maximum(m_scr[...], lane_tile_max(s))

        def accumulate(s, pos, width):
            acc_scr[...] += _dot(weights(s), sel_values(pos, width))

        m_scr[...] = lane_tile_max(s_near2)
        over_plain_keys(take_max)
        m_scr[...] = jnp.broadcast_to(jnp.max(m_scr[...], axis=1, keepdims=True), m_scr.shape)
        acc_scr[...] = _dot(weights(s_near2), sel_values(near, 2 * Q_BLOCK))
        over_plain_keys(accumulate)
        ob_scr[g, 1] = acc_scr[:, 0:LANES] / acc_scr[:, LANES:2 * LANES]

    sig = jax.nn.sigmoid(ng_ref[...])
    for h in range(HPG):
        slab = None
        for j in range(3):
            part = []
            for g in range(N_KV_HEADS):
                c = j * N_Q_HEADS + g * HPG + h
                part.append(sig[:, c:c + 1] * ob_scr[g, j, h * Q_BLOCK:(h + 1) * Q_BLOCK, :])
            term = jnp.where(lane < HEAD_DIM, part[0], part[1])
            slab = term if slab is None else slab + term
        o_ref[:, h * LANES:(h + 1) * LANES] = slab.astype(BF16)


def _attn_prompt(q, ng, kc, vc, kall, emat, smap, d01, bc, wtab, b, t):
    nq = t // Q_BLOCK
    rows = HPG * Q_BLOCK
    full = lambda a: pl.BlockSpec(a.shape, lambda bi, i: (0,) * a.ndim)
    per_batch = lambda a: pl.BlockSpec((None,) + a.shape[1:], lambda bi, i: (bi, 0, 0))
    return pl.pallas_call(
        _attn_prompt_kernel,
        grid=(b, nq),
        in_specs=[
            pl.BlockSpec((Q_BLOCK, NSA_W), lambda bi, i: (bi * nq + i, 0)),
            pl.BlockSpec((Q_BLOCK, LANES), lambda bi, i: (bi * nq + i, 0)),
            per_batch(kc), per_batch(vc), per_batch(kall),
            full(emat), full(smap), full(d01), full(bc), full(wtab),
        ],
        out_specs=pl.BlockSpec((Q_BLOCK, NSA_W), lambda bi, i: (bi * nq + i, 0)),
        out_shape=jax.ShapeDtypeStruct((b * t, NSA_W), BF16),
        scratch_shapes=[pltpu.VMEM((rows, LANES), F32), pltpu.VMEM((rows, 2 * LANES), F32),
                        pltpu.VMEM((N_KV_HEADS, 3, rows, LANES), F32)],
        compiler_params=_params("parallel", "arbitrary"),
        name="attn_prompt",
    )(q, ng, kc, vc, kall, emat, smap, d01, bc, wtab)


def _pool_rows(ext_scr, base, u, pos):
    m, width = u.shape
    lane = lax.broadcasted_iota(jnp.int32, (m, width), 1)
    gc = width // len(POOL_WINDOWS)
    run = u
    pooled = None
    for k in range(1, max(POOL_WINDOWS)):
        run = run + ext_scr[base - k:base - k + m, :]
        w = k + 1
        if w in POOL_WINDOWS:
            mean = run / jnp.minimum(pos + 1, w).astype(F32)
            pooled = mean if pooled is None else jnp.where(lane >= POOL_WINDOWS.index(w) * gc, mean, pooled)
    return pooled - u


def _conv_rows(ext_scr, base, m, cw_ref):
    y = None
    for j in range(CONV_K):
        off = base - (CONV_K - 1) + j
        term = cw_ref[j:j + 1, :] * ext_scr[off:off + m, :]
        y = term if y is None else y + term
    return y


def _mix_prompt_kernel(u_ref, uh_ref, z_ref, zh_ref, cb_ref, cw_ref, d_ref, c_ref, ext_scr, *, tm):
    i = pl.program_id(1)
    has_hist = (i > 0).astype(F32)
    pos = i * tm + lax.broadcasted_iota(jnp.int32, u_ref.shape, 0)
    ext_scr[0:HALO, :] = uh_ref[...] * has_hist
    ext_scr[HALO:HALO + tm, :] = u_ref[...]
    d_ref[...] = _pool_rows(ext_scr, HALO, u_ref[...], pos).astype(BF16)
    ext_scr[0:HALO, :] = zh_ref[...] * has_hist
    ext_scr[HALO:HALO + tm, :] = z_ref[...]
    c_ref[...] = (cb_ref[...] * _conv_rows(ext_scr, HALO, tm, cw_ref)).astype(BF16)


def _mix_prompt(u, z, cb, conv_w, b, t, tm):
    nt = t // tm
    width = u.shape[1]
    row = pl.BlockSpec((tm, width), lambda bi, i: (bi * nt + i, 0))
    halo = pl.BlockSpec((HALO, width), lambda bi, i: (jnp.maximum((bi * nt + i) * (tm // HALO) - 1, 0), 0))
    out = jax.ShapeDtypeStruct((b * t, width), BF16)
    return pl.pallas_call(
        functools.partial(_mix_prompt_kernel, tm=tm),
        grid=(b, nt),
        in_specs=[row, halo, row, halo, row, pl.BlockSpec(conv_w.shape, lambda bi, i: (0, 0))],
        out_specs=[row, row],
        out_shape=[out, out],
        scratch_shapes=[pltpu.VMEM((HALO + tm, width), F32)],
        compiler_params=_params("parallel", "parallel"),
        name="mix_prompt",
    )(u, u, z, z, cb, conv_w)


def _merge_kernel(x_ref, ap_ref, as_ref, dp_ref, ds_ref, cp_ref, cs_ref, g_ref, wg_ref, wa_ref, wp_ref, ps_ref, wc_ref,
                  wo_ref, o_ref, *, prompt_tiles):
    dm = x_ref.shape[1]
    is_prompt = pl.program_id(0) < prompt_tiles
    a = jnp.where(is_prompt, ap_ref[...], as_ref[...])
    d = jnp.where(is_prompt, dp_ref[...], ds_ref[...])
    c = jnp.where(is_prompt, cp_ref[...], cs_ref[...])
    gates = jax.nn.sigmoid(_dot(_row_rms(x_ref[...], g_ref[...]).astype(BF16), wg_ref[...]))
    m = gates[:, 0:dm] * _dot(a, wa_ref[...])
    m = m + gates[:, dm:2 * dm] * (_dot(d, wp_ref[...]) * ps_ref[...])
    m = m + gates[:, 2 * dm:3 * dm] * _dot(c, wc_ref[...])
    o_ref[...] = x_ref[...] + _dot(m.astype(BF16), wo_ref[...])


def _merge(x, attn_p, attn_s, d_p, d_s, c_p, c_s, gain, w_gate, wa, wp, ps, wc, wo, tm):
    n, dm = x.shape
    gain = gain.reshape(1, dm)
    n_p, n_s = attn_p.shape[0], attn_s.shape[0]
    assert n_p % tm == 0 and n_s % tm == 0 and n_p + n_s == n
    pt = n_p // tm
    row = lambda a: pl.BlockSpec((tm, a.shape[1]), lambda i: (i, 0))
    prompt = lambda a: pl.BlockSpec((tm, a.shape[1]), lambda i: (jnp.minimum(i, pt - 1), 0))
    sample = lambda a: pl.BlockSpec((tm, a.shape[1]), lambda i: (jnp.maximum(i - pt, 0), 0))
    full = lambda a: pl.BlockSpec(a.shape, lambda i: (0, 0))
    ps = ps.reshape(1, dm)
    return pl.pallas_call(
        functools.partial(_merge_kernel, prompt_tiles=pt),
        grid=(n // tm,),
        in_specs=[row(x), prompt(attn_p), sample(attn_s), prompt(d_p), sample(d_s), prompt(c_p), sample(c_s),
                  full(gain), full(w_gate), full(wa), full(wp), full(ps), full(wc), full(wo)],
        out_specs=row(x),
        out_shape=jax.ShapeDtypeStruct((n, dm), F32),
        compiler_params=_params("parallel"),
        name="merge",
    )(x, attn_p, attn_s, d_p, d_s, c_p, c_s, gain, w_gate, wa, wp, ps, wc, wo)


SAMPLE_ROWS = 8
SAMPLE_NB = 4


def _cmp_partial_kernel(xt_ref, pe_ref, w1_ref, o_ref, x_scr, *, n_ch):
    pages, _, page = xt_ref.shape
    n_part = 4 if pages % 4 == 0 else 1
    pp, pc = pages // n_part, n_ch // n_part
    for part in range(n_part):
        for p in range(part * pp, (part + 1) * pp):
            x_scr[p * page:(p + 1) * page, :] = xt_ref[p].T
        first = jnp.zeros((pc, KV_W), F32)
        second = jnp.zeros((pc, KV_W), F32)
        for l in range(CMP_STRIDE):
            xl = x_scr[pl.ds(part * pp * page + l, pc, stride=CMP_STRIDE), :]
            first += _dot((xl + pe_ref[l:l + 1, :]).astype(BF16), w1_ref[l])
            second += _dot((xl + pe_ref[CMP_STRIDE + l:CMP_STRIDE + l + 1, :]).astype(BF16),
                           w1_ref[CMP_STRIDE + l])
        o_ref[part * pc:(part + 1) * pc, 0:KV_W] = first
        o_ref[part * pc:(part + 1) * pc, KV_W:2 * KV_W] = second


def _cmp_partial(cache_t, pe2, w1bd, layer, n_pool):
    page = cache_t.shape[2]
    ch = page // CMP_STRIDE
    pp = next(p for p in (64, 32, 16, 8, 4, 2, 1) if n_pool % p == 0)
    nb = n_pool // pp
    return pl.pallas_call(
        functools.partial(_cmp_partial_kernel, n_ch=pp * ch),
        grid=(nb, 2),
        in_specs=[pl.BlockSpec((pp, KV_W, page), lambda i, s: (layer * nb + i, s, 0)),
                  pl.BlockSpec((None,) + pe2.shape[1:], lambda i, s: (s, 0, 0)),
                  pl.BlockSpec((None,) + w1bd.shape[1:], lambda i, s: (s, 0, 0, 0))],
        out_specs=pl.BlockSpec((pp * ch, 2 * KV_W), lambda i, s: (i, s)),
        out_shape=jax.ShapeDtypeStruct((n_pool * ch, 4 * KV_W), F32),
        scratch_shapes=[pltpu.VMEM((pp * page, KV_W), F32)],
        compiler_params=_params("parallel", "arbitrary"),
        name="cmp_partial",
    )(cache_t, pe2, w1bd)


def _softmax_pair(s_a, vt_a, s_b, v_b):
    m = jnp.maximum(jnp.max(s_a, axis=1, keepdims=True), jnp.max(s_b, axis=1, keepdims=True))
    e_a = jnp.exp(s_a - m)
    e_b = jnp.exp(s_b - m)
    den = jnp.sum(e_a, axis=1, keepdims=True) + jnp.sum(e_b, axis=1, keepdims=True)
    return (_dot_nt(e_a.astype(BF16), vt_a) + _dot(e_b.astype(BF16), v_b)) / den


def _sample_kernel(pt_ref, q_ref, ng_ref, kv_ref, win_ref, wnt_ref, u_ref, z_ref, cb_ref, swin_ref, spool_ref,
                   sconv_ref, cache_ref, fs_ref, w2_ref, kg_ref, cw_ref, smap_ref, emat_ref, bc_ref, dsel_ref, dwin_ref,
                   dnew_ref, attn_ref, d_ref, c_ref, owin_ref, opool_ref, oconv_ref,
                   kvbuf, fsbuf, sems, ext_scr, *, layer_base, n_pages, page, s_len, past):
    i = pl.program_id(0)
    slot = i % 2
    ch = page // CMP_STRIDE
    sr = SAMPLE_ROWS

    def copies(step, slot_):
        out = []
        for sb in range(SAMPLE_NB):
            for pg in range(n_pages):
                pid = pt_ref[step * SAMPLE_NB + sb, pg]
                out.append(pltpu.make_async_copy(
                    cache_ref.at[layer_base + pid, pl.ds(2 * KV_W, 2 * KV_W), :],
                    kvbuf.at[slot_, sb, :, pl.ds(pg * page, page)], sems.at[slot_, 0]))
                out.append(pltpu.make_async_copy(
                    fs_ref.at[pid], fsbuf.at[slot_, sb, pl.ds(pg * ch, ch), :], sems.at[slot_, 1]))
        return out

    @pl.when(i == 0)
    def _():
        for c in copies(0, 0):
            c.start()

    @pl.when(i + 1 < pl.num_programs(0))
    def _():
        for c in copies(i + 1, 1 - slot):
            c.start()

    for c in copies(i, slot):
        c.wait()

    lane = lax.broadcasted_iota(jnp.int32, (sr, LANES), 1)
    grp_rows = N_KV_HEADS * sr
    q_all = q_ref[...].astype(F32)
    sig_all = jax.nn.sigmoid(ng_ref[...])
    kvn = kv_ref[...]
    winn = win_ref[...]
    pad_rows = jnp.zeros((LANES - sr, KV_W), F32)
    w_keep = swin_ref.shape[2]
    wlane = lax.broadcasted_iota(jnp.int32, swin_ref.shape[1:], 1)
    hist = spool_ref.shape[1]
    ck = sconv_ref.shape[1]
    attn_rows, d_rows, c_rows = [], [], []

    every = range(SAMPLE_NB)

    def new_rows(x8):
        return jnp.concatenate([x8, pad_rows], axis=0).astype(BF16)

    lhs, ks_new, vs_new, kw_new, vw_new = [], [], [], [], []
    for sb in every:
        r0 = sb * sr
        q8 = q_all[r0:r0 + sr]
        pieces = []
        for h in range(HPG):
            for g in range(N_KV_HEADS):
                pieces.append(jnp.where((lane >= HEAD_DIM) == (g == 1), q8[:, h * LANES:(h + 1) * LANES], 0.0))
        lhs.append(jnp.concatenate(pieces, axis=0).astype(BF16))
        ks_new.append(new_rows(kvn[r0:r0 + sr, 2 * KV_W:3 * KV_W]))
        vs_new.append(new_rows(kvn[r0:r0 + sr, 3 * KV_W:4 * KV_W]))
        kw_new.append(new_rows(winn[r0:r0 + sr, 0:KV_W]))
        vw_new.append(new_rows(winn[r0:r0 + sr, KV_W:2 * KV_W]))

    n_ch = fsbuf.shape[2]
    cmp = []
    for s_ in range(2):
        hid = jnp.concatenate(
            [fsbuf[slot, sb, :, 2 * s_ * KV_W:(2 * s_ + 1) * KV_W]
             + pltpu.roll(fsbuf[slot, sb, :, (2 * s_ + 1) * KV_W:(2 * s_ + 2) * KV_W], n_ch - 1, 0) for sb in every],
            axis=0)
        cmp.append(_dot((hid * jax.nn.sigmoid(hid)).astype(BF16), w2_ref[s_]))
    kc_all = _head_rms(cmp[0], kg_ref[...]).astype(BF16)
    vc_all = cmp[1].astype(BF16)
    p_c, o_c = [], []
    for sb in every:
        s_c = _dot_nt(lhs[sb], kc_all[sb * n_ch:(sb + 1) * n_ch]) + bc_ref[...]
        e_c = jnp.exp(s_c - jnp.max(s_c, axis=1, keepdims=True))
        p_c.append(e_c / jnp.sum(e_c, axis=1, keepdims=True))
        o_c.append(_dot(p_c[sb].astype(BF16), vc_all[sb * n_ch:(sb + 1) * n_ch]))

    o_w = [_softmax_pair(_dot(lhs[sb], swin_ref[sb, 0:KV_W, :].astype(BF16)) + dwin_ref[...],
                         swin_ref[sb, KV_W:2 * KV_W, :].astype(BF16),
                         _dot_nt(lhs[sb], kw_new[sb]) + dnew_ref[...], vw_new[sb]) for sb in every]

    imp = jnp.concatenate([sum(p_c[sb][h * grp_rows:(h + 1) * grp_rows] for h in range(HPG)) for sb in every], axis=0)
    score = _dot_split(imp, smap_ref[...])
    jcol = lax.broadcasted_iota(jnp.int32, score.shape, 1)
    cur = (past + lax.broadcasted_iota(jnp.int32, score.shape, 0) % sr) // SEL_BLOCK
    forced = (jcol == 0) | (jcol == cur) | (jcol == cur - 1)
    score = jnp.where(forced, jnp.inf, score)
    score = jnp.where(jcol <= cur, score, NEG_INF)
    not_sel_all = (_top_k_mask(score, TOP_K) - 1.0) * MASK_BIG

    o_s = []
    for sb in every:
        not_sel = jnp.concatenate([not_sel_all[sb * grp_rows:(sb + 1) * grp_rows]] * HPG, axis=0).astype(BF16)
        lhs_sel = jnp.concatenate([lhs[sb], not_sel], axis=1)
        k_past = jnp.concatenate([kvbuf[slot, sb, 0:KV_W, :].astype(BF16), emat_ref[...]], axis=0)
        o_s.append(_softmax_pair(_dot(lhs_sel, k_past) + dsel_ref[...],
                                 kvbuf[slot, sb, KV_W:2 * KV_W, :].astype(BF16),
                                 _dot_nt(lhs[sb], ks_new[sb]) + dnew_ref[...], vs_new[sb]))

    for sb in every:
        r0 = sb * sr
        branch = (o_c[sb], o_s[sb], o_w[sb])
        sig8 = sig_all[r0:r0 + sr]
        slabs = []
        for h in range(HPG):
            slab = None
            for j in range(3):
                part = []
                for g in range(N_KV_HEADS):
                    c = j * N_Q_HEADS + g * HPG + h
                    lo = h * grp_rows + g * sr
                    part.append(sig8[:, c:c + 1] * branch[j][lo:lo + sr])
                term = jnp.where(lane < HEAD_DIM, part[0], part[1])
                slab = term if slab is None else slab + term
            slabs.append(slab)
        attn_rows.append(jnp.concatenate(slabs, axis=1))

        state = pltpu.roll(swin_ref[sb], w_keep - s_len, 1)
        for tt in range(s_len):
            state = jnp.where(wlane == w_keep - s_len + tt, wnt_ref[sb, :, tt:tt + 1], state)
        owin_ref[sb] = state

        u8 = u_ref[r0:r0 + sr, :]
        ext_scr[0:hist, :] = spool_ref[sb]
        ext_scr[hist:hist + sr, :] = u8
        d_rows.append(_pool_rows(ext_scr, hist, u8, hist + lax.broadcasted_iota(jnp.int32, u8.shape, 0)))
        opool_ref[sb] = ext_scr[s_len:s_len + hist, :]
        ext_scr[0:ck, :] = sconv_ref[sb]
        ext_scr[ck:ck + sr, :] = z_ref[r0:r0 + sr, :]
        c_rows.append(cb_ref[r0:r0 + sr, :] * _conv_rows(ext_scr, ck, sr, cw_ref))
        oconv_ref[sb] = ext_scr[s_len:s_len + ck, :]

    attn_ref[...] = jnp.concatenate(attn_rows, axis=0).astype(BF16)
    d_ref[...] = jnp.concatenate(d_rows, axis=0).astype(BF16)
    c_ref[...] = jnp.concatenate(c_rows, axis=0).astype(BF16)


def _sample_layer(layer, page_table, q, ng, kv4, win, wnt, u, z, cb, swin3, spool3, sconv3, cache3, fs3, w2bd, kgain,
                  conv_w, smap_s, emat, bc_s, dsel, dwin, dnew, n_p, db, s_len, n_pool):
    nsteps = db // SAMPLE_NB
    rb = SAMPLE_NB * SAMPLE_ROWS
    r0 = n_p // rb
    page = cache3.shape[2]
    n_pages = page_table.shape[1]
    past = n_pages * page
    tok = lambda a: pl.BlockSpec((rb, a.shape[1]), lambda i, pt: (r0 + i, 0))
    new_t = pl.BlockSpec((SAMPLE_NB,) + wnt.shape[1:], lambda i, pt: (i, 0, 0))
    state = lambda a: pl.BlockSpec((SAMPLE_NB,) + a.shape[1:], lambda i, pt: (layer * nsteps + i, 0, 0))
    full = lambda a: pl.BlockSpec(a.shape, lambda i, pt: (0,) * a.ndim)
    hbm = pl.BlockSpec(memory_space=pl.ANY)
    out_tok = lambda w_: pl.BlockSpec((rb, w_), lambda i, pt: (i, 0))
    out_state = lambda a: pl.BlockSpec((SAMPLE_NB,) + a.shape[1:], lambda i, pt: (i, 0, 0))
    width = u.shape[1]
    n_tok = db * SAMPLE_ROWS
    return pl.pallas_call(
        functools.partial(_sample_kernel, layer_base=layer * n_pool, n_pages=n_pages, page=page, s_len=s_len, past=past),
        grid_spec=pltpu.PrefetchScalarGridSpec(
            num_scalar_prefetch=1,
            grid=(nsteps,),
            in_specs=[tok(q), tok(ng), tok(kv4), tok(win), new_t, tok(u), tok(z), tok(cb), state(swin3), state(spool3),
                      state(sconv3), hbm, hbm, full(w2bd), full(kgain), full(conv_w), full(smap_s), full(emat),
                      full(bc_s), full(dsel), full(dwin), full(dnew)],
            out_specs=[out_tok(NSA_W), out_tok(width), out_tok(width), out_state(swin3), out_state(spool3),
                       out_state(sconv3)],
            scratch_shapes=[pltpu.VMEM((2, SAMPLE_NB, 2 * KV_W, past), F32),
                            pltpu.VMEM((2, SAMPLE_NB, past // CMP_STRIDE, 4 * KV_W), F32),
                            pltpu.SemaphoreType.DMA((2, 2)),
                            pltpu.VMEM((4 * SAMPLE_ROWS, width), F32)]),
        out_shape=[jax.ShapeDtypeStruct((n_tok, NSA_W), BF16), jax.ShapeDtypeStruct((n_tok, width), BF16),
                   jax.ShapeDtypeStruct((n_tok, width), BF16), jax.ShapeDtypeStruct((db,) + swin3.shape[1:], F32),
                   jax.ShapeDtypeStruct((db,) + spool3.shape[1:], F32),
                   jax.ShapeDtypeStruct((db,) + sconv3.shape[1:], F32)],
        compiler_params=_params("arbitrary"),
        name="sample_mix",
    )(page_table, q, ng, kv4, win, wnt, u, z, cb, swin3, spool3, sconv3, cache3, fs3, w2bd, kgain, conv_w, smap_s,
      emat, bc_s, dsel, dwin, dnew)


def _bucket_table(n):
    d = np.arange(n)
    nf = np.maximum(d, 1).astype(np.float32)
    large = MAX_EXACT + (np.log(nf / np.float32(MAX_EXACT)) / np.float32(math.log(MAX_DISTANCE / MAX_EXACT))
                         * np.float32(N_BUCKETS - MAX_EXACT)).astype(np.int32)
    return np.where(d < MAX_EXACT, d, np.minimum(large, N_BUCKETS - 1)).astype(np.int32)


def _bias_values(rel_bias, dist, visible=None):
    r, c = dist.shape
    offset = int(dist[0, 0])
    stride = int(dist[0, 0] - dist[0, 1]) if c > 1 else 1
    assert stride >= 1 and np.array_equal(dist, offset + np.arange(r)[:, None] - stride * np.arange(c)[None, :])
    visible = (dist >= 0) if visible is None else (visible & (dist >= 0))
    cw = stride * (c - 1) + 1
    period = r + cw
    m = np.arange(period)
    by_wrap = offset + np.where(m < cw, -m, period - m)
    bkt = _bucket_table(max(int(by_wrap.max()) + 1, MAX_DISTANCE))[np.clip(by_wrap, 0, None)]
    tab = rel_bias - rel_bias[N_BUCKETS - 1:N_BUCKETS]
    per_dist = tab[jnp.asarray(bkt)].T
    vals = jnp.tile(per_dist, (1, r))[:, :r * (period - 1)].reshape(-1, r, period - 1)[:, :, :cw:stride]
    vals = jnp.where(jnp.asarray(visible)[None], vals, NEG_INF)
    return vals.reshape((N_KV_HEADS, HPG) + dist.shape)


def _bias_table(rel_bias, dist):
    r, c = dist.shape
    return _bias_values(rel_bias, dist).reshape(N_KV_HEADS, HPG * r, c)


def _bias_table_sample(rel_bias, dist, visible=None):
    r, c = dist.shape
    return _bias_values(rel_bias, dist, visible).transpose(1, 0, 2, 3).reshape(HPG * N_KV_HEADS * r, c)


def _sel_map(n_cmp, n_rows, n_cols, pad):
    r_sel = SEL_BLOCK // CMP_STRIDE
    r_cmp = CMP_BLOCK // CMP_STRIDE
    out = np.zeros((n_rows, n_cols), np.float32)
    for j in range(n_cols):
        for m in range(r_sel):
            for n in range(r_cmp):
                c = r_sel * j + m - n
                if 0 <= c < n_cmp:
                    out[pad + c, j] += 1.0
    return out


def _slab_order(w, axis):
    shape = w.shape
    w = w.reshape(shape[:axis] + (N_KV_HEADS, HPG, HEAD_DIM) + shape[axis + 1:])
    return jnp.swapaxes(w, axis, axis + 1).reshape(shape)


def _proj_weight(w_in):
    kv0, ng0 = NSA_W, NSA_W + 6 * KV_W
    rest0 = ng0 + 3 * N_Q_HEADS
    slot = lambda s: w_in[:, :, kv0 + s * KV_W:kv0 + (s + 1) * KV_W]
    ng = w_in[:, :, ng0:rest0].reshape(w_in.shape[:2] + (N_Q_HEADS, 3))
    ng = jnp.swapaxes(ng, 2, 3).reshape(w_in.shape[:2] + (3 * N_Q_HEADS,))
    ng = jnp.pad(ng, ((0, 0), (0, 0), (0, C_END - C_NG - 3 * N_Q_HEADS)))
    parts = [_slab_order(w_in[:, :, :NSA_W], 2), slot(2), slot(4), slot(0), slot(1), slot(3), slot(5),
             w_in[:, :, rest0:], ng]
    out = jnp.concatenate(parts, axis=2)
    assert out.shape[2] == C_END
    return out


def _block_diag2(w):
    z = jnp.zeros_like(w)
    return jnp.concatenate([jnp.concatenate([w, z], axis=-1), jnp.concatenate([z, w], axis=-1)], axis=-2)


def _token_tile(n):
    for tm in (512, 256, 128):
        if n % tm == 0:
            return tm
    raise ValueError(f"token count {n} is not a multiple of 128")


def kernel(x_prompt, x_sample, cache_kv, state_win, state_pool, state_conv, page_table, rel_bias, ln_ffn1, w_ffn1_gate, w_ffn1_up, w_ffn1_down, ln_mix, w_in, q_norm, k_norm, cmp_pe, cmp_w1, cmp_w2, w_nsa_out, w_pool, pool_scale, conv_w, w_conv_out, w_o, ln_ffn2, w_ffn2_gate, w_ffn2_up, w_ffn2_down):
    b, t, dm = x_prompt.shape
    db, s, _ = x_sample.shape
    depth, n_pool, page = cache_kv.shape[:3]
    n_pages = page_table.shape[1]
    past = n_pages * page
    w_keep = state_win.shape[2]
    sr = SAMPLE_ROWS
    n_p, n_s = b * t, db * sr
    assert t % Q_BLOCK == 0 and t >= WINDOW and t // SEL_BLOCK <= LANES
    assert s <= sr and db % SAMPLE_NB == 0 and n_p % (SAMPLE_NB * sr) == 0
    assert (past + s) // CMP_STRIDE == past // CMP_STRIDE == LANES and w_keep <= past and s <= w_keep
    assert -(-(past + s) // SEL_BLOCK) <= LANES
    tm = _token_tile(n_p + n_s)
    tm_mix = _token_tile(t)

    w_proj = _proj_weight(w_in).astype(BF16)
    w_gate = w_proj[:, :, C_MG:C_NG]
    head_gain = jnp.concatenate([jnp.tile(q_norm, (1, N_Q_HEADS)), jnp.tile(k_norm[:, 1], (1, N_KV_HEADS)),
                                 jnp.tile(k_norm[:, 2], (1, N_KV_HEADS))], axis=1).reshape(depth, 1, N_NORMED)
    cmp_gain = jnp.tile(k_norm[:, 0], (1, N_KV_HEADS)).reshape(depth, 1, KV_W)
    pe2 = jnp.tile(cmp_pe, (1, 1, 1, N_KV_HEADS))
    w1bd = _block_diag2(cmp_w1).astype(BF16)
    w2bd = _block_diag2(cmp_w2).astype(BF16)
    wa = _slab_order(w_nsa_out, axis=1).astype(BF16)
    n_grp, gc, ge = w_pool.shape[1:]
    wp = jnp.zeros((depth, n_grp * gc, n_grp * ge), F32)
    for gi in range(n_grp):
        wp = wp.at[:, gi * gc:(gi + 1) * gc, gi * ge:(gi + 1) * ge].set(w_pool[:, gi])
    wp = wp.astype(BF16)
    wc = w_conv_out.astype(BF16)
    wo = w_o.astype(BF16)
    ffn1 = [w.astype(BF16) for w in (w_ffn1_gate, w_ffn1_up, w_ffn1_down)]
    ffn2 = [w.astype(BF16) for w in (w_ffn2_gate, w_ffn2_up, w_ffn2_down)]

    ti = np.arange(Q_BLOCK)
    d01 = _bias_table(rel_bias, Q_BLOCK + ti[:, None] - np.arange(2 * Q_BLOCK)[None, :])
    near_c = np.arange(LANES)
    bc = _bias_table(rel_bias, ti[:, None] - CMP_STRIDE * (near_c[None, :] - CMP_PAD) - (CMP_BLOCK - 1))
    n_cmp = t // CMP_STRIDE - CMP_BLOCK // CMP_STRIDE + 1
    smap = jnp.asarray(_sel_map(n_cmp, _cmp_rows(t), LANES, CMP_PAD))
    wj = np.arange(WINDOW + Q_BLOCK)[None, :]
    dist_pw = WINDOW + ti[:, None] - wj
    wtab = _bias_values(rel_bias, dist_pw, dist_pw < WINDOW).reshape(N_KV_HEADS, HPG * Q_BLOCK, WINDOW + Q_BLOCK)
    key_pos = np.arange(-Q_BLOCK, t + SEL_CHUNK)[:, None]
    emat_p = jnp.asarray((key_pos >= 0) & (key_pos < t)
                         & (key_pos // SEL_BLOCK == np.arange(LANES)[None, :])).astype(BF16)

    tq = np.arange(sr)[:, None]
    n_chunks = past // CMP_STRIDE
    n_cmp_s = n_chunks - CMP_BLOCK // CMP_STRIDE + 1
    cn = np.arange(n_chunks)[None, :]
    bc_s = _bias_table_sample(rel_bias, past + tq - (CMP_STRIDE * cn + CMP_BLOCK - 1), cn < n_cmp_s)
    dsel = _bias_table_sample(rel_bias, past + tq - np.arange(past)[None, :])
    tn = np.arange(LANES)[None, :]
    dnew = _bias_table_sample(rel_bias, tq - tn, tn < s)
    dist_w = w_keep + tq - np.arange(w_keep)[None, :]
    dwin = _bias_table_sample(rel_bias, dist_w, dist_w < WINDOW)
    smap_s = jnp.asarray(_sel_map(n_cmp_s, n_chunks, LANES, 0)).astype(BF16)
    emat = jnp.asarray(np.arange(LANES)[:, None] == np.arange(past)[None, :] // SEL_BLOCK).astype(BF16)

    cache3 = cache_kv.transpose(0, 1, 3, 4, 5, 2).reshape(depth * n_pool, -1, page)
    swin3 = state_win.transpose(0, 1, 3, 4, 5, 2).reshape(depth * db, -1, w_keep)
    spool3 = state_pool.reshape((depth * db,) + state_pool.shape[2:])
    sconv3 = state_conv.reshape((depth * db,) + state_conv.shape[2:])

    xs = jnp.pad(x_sample, ((0, 0), (0, sr - s), (0, 0)))
    x = jnp.concatenate([x_prompt.reshape(n_p, dm), xs.reshape(n_s, dm)], axis=0)
    outs = [[] for _ in range(8)]
    for l in range(depth):
        x = _ffn(x, ln_ffn1[l], ffn1[0][l], ffn1[1][l], ffn1[2][l], tm)
        q, kv4, win, ksv, u, cb, z, ng = _proj(x, ln_mix[l], w_proj[l], head_gain[l], tm)

        kc, vc = _compress_prompt(kv4, pe2[l], w1bd[l], w2bd[l], cmp_gain[l], b, t)
        kall = jnp.pad(ksv[:n_p].reshape(b, t, 4 * KV_W), ((0, 0), (WINDOW, SEL_CHUNK), (0, 0)))
        attn_p = _attn_prompt(q, ng, kc, vc, kall, emat_p, smap, d01, bc, wtab, b, t)
        d_p, c_p = _mix_prompt(u, z, cb, conv_w[l], b, t, tm_mix)

        fs3 = _cmp_partial(cache3, pe2[l], w1bd[l], l, n_pool).reshape(n_pool, page // CMP_STRIDE, -1)
        wnt = win[n_p:].reshape(db, sr, -1).transpose(0, 2, 1)
        attn_s, d_s, c_s, s_win, s_pool, s_conv = _sample_layer(
            l, page_table, q, ng, kv4, win, wnt, u, z, cb, swin3, spool3, sconv3, cache3, fs3, w2bd[l], cmp_gain[l],
            conv_w[l], smap_s, emat, bc_s, dsel, dwin, dnew, n_p, db, s, n_pool)

        x = _merge(x, attn_p, attn_s, d_p, d_s, c_p, c_s, ln_mix[l], w_gate[l], wa[l], wp[l], pool_scale[l], wc[l],
                   wo[l], tm)
        x = _ffn(x, ln_ffn2[l], ffn2[0][l], ffn2[1][l], ffn2[2][l], tm)

        p_keep = min(WINDOW, t)
        outs[0].append(kv4[:n_p].reshape(b, t, 4, N_KV_HEADS, HEAD_DIM))
        outs[1].append(win[:n_p].reshape(b, t, 2, N_KV_HEADS, HEAD_DIM)[:, t - p_keep:])
        outs[2].append(u[:n_p].reshape(b, t, -1)[:, t - POOL_STATE:])
        outs[3].append(z[:n_p].reshape(b, t, -1)[:, t - (CONV_K - 1):])
        outs[4].append(kv4[n_p:].reshape(db, sr, 4, N_KV_HEADS, HEAD_DIM)[:, :s])
        outs[5].append(s_win.reshape(db, 2, N_KV_HEADS, HEAD_DIM, w_keep).transpose(0, 4, 1, 2, 3))
        outs[6].append(s_pool)
        outs[7].append(s_conv)

    y_prompt = x[:n_p].reshape(b, t, dm)
    y_sample = x[n_p:].reshape(db, sr, dm)[:, :s]
    return (y_prompt, y_sample) + tuple(jnp.stack(o) for o in outs)
```

```python
import functools
import math

import numpy as np
import jax
import jax.numpy as jnp
from jax import lax
from jax.experimental import pallas as pl
from jax.experimental.pallas import tpu as pltpu

F32 = jnp.float32
BF16 = jnp.bfloat16
NEG_INF = float("-inf")

HEAD_DIM = 64
N_KV_HEADS = 2
HPG = 4
N_Q_HEADS = N_KV_HEADS * HPG
NSA_W = N_Q_HEADS * HEAD_DIM
KV_W = N_KV_HEADS * HEAD_DIM
CMP_BLOCK = 32
CMP_STRIDE = 16
SEL_BLOCK = 64
TOP_K = 8
WINDOW = 512
Q_BLOCK = 128
N_BUCKETS = 32
MAX_EXACT = 16
MAX_DISTANCE = 128
POOL_WINDOWS = (2, 4, 8, 16)
POOL_STATE = 15
CONV_K = 3
RMS_EPS = 1e-6
LANES = 128
HALO = 16
CMP_PAD = 112
SEL_CHUNK = 512
MASK_BIG = 1e30
VMEM_LIMIT = 56 * 1024 * 1024


def _dot(a, b):
    return jnp.dot(a, b, preferred_element_type=F32)


def _dot_nt(a, b):
    return lax.dot_general(a, b, (((1,), (1,)), ((), ())), preferred_element_type=F32)


def _dot_split(a, b):
    hi = a.astype(BF16)
    lo = (a - hi.astype(F32)).astype(BF16)
    return _dot(hi, b) + _dot(lo, b)


def _group_mean_sq(x):
    r = lax.broadcasted_iota(jnp.int32, (LANES, LANES), 0) // HEAD_DIM
    c = lax.broadcasted_iota(jnp.int32, (LANES, LANES), 1) // HEAD_DIM
    ones_bd = (r == c).astype(BF16)
    return _dot_split(x * x, ones_bd) * (1.0 / HEAD_DIM)


def _head_rms(x, gain):
    parts = []
    for k in range(x.shape[1] // LANES):
        xs = x[:, k * LANES:(k + 1) * LANES]
        parts.append(xs * lax.rsqrt(_group_mean_sq(xs) + RMS_EPS))
    y = parts[0] if len(parts) == 1 else jnp.concatenate(parts, axis=1)
    return y * gain


def _row_rms(x, gain):
    return x * lax.rsqrt(jnp.mean(x * x, axis=-1, keepdims=True) + RMS_EPS) * gain


def _params(*sem):
    return pltpu.CompilerParams(dimension_semantics=sem, vmem_limit_bytes=VMEM_LIMIT)


def _ffn_kernel(x_ref, g_ref, wg_ref, wu_ref, wd_ref, o_ref, *, n_split):
    x = x_ref[...]
    h = _row_rms(x, g_ref[...]).astype(BF16)
    tf = wg_ref.shape[1] // n_split
    acc = None
    for f in range(n_split):
        a = _dot(h, wg_ref[:, f * tf:(f + 1) * tf])
        b = _dot(h, wu_ref[:, f * tf:(f + 1) * tf])
        part = _dot((a * jax.nn.sigmoid(a) * b).astype(BF16), wd_ref[f * tf:(f + 1) * tf, :])
        acc = part if acc is None else acc + part
    o_ref[...] = x + 0.5 * acc


def _ffn(x, gain, wg, wu, wd, tm):
    n, d = x.shape
    dff = wg.shape[1]
    n_split = 1
    assert dff % (n_split * LANES) == 0
    resident = lambda a: pl.BlockSpec(a.shape, lambda i: (0, 0), pipeline_mode=pl.Buffered(1))
    return pl.pallas_call(
        functools.partial(_ffn_kernel, n_split=n_split),
        grid=(n // tm,),
        in_specs=[pl.BlockSpec((tm, d), lambda i: (i, 0)), pl.BlockSpec((1, d), lambda i: (0, 0)),
                  resident(wg), resident(wu), resident(wd)],
        out_specs=pl.BlockSpec((tm, d), lambda i: (i, 0)),
        out_shape=jax.ShapeDtypeStruct((n, d), F32),
        compiler_params=_params("parallel"),
        name="ffn",
    )(x, gain.reshape(1, d), wg, wu, wd)


C_Q = 0
C_KSEL = 512
C_KWIN = 640
C_KCMP = 768
C_VCMP = 896
C_VSEL = 1024
C_VWIN = 1152
C_U = 1280
C_CB = 1536
C_CC = 1792
C_CH = 2048
C_MG = 2304
C_NG = 5376
C_END = 5504
N_NORMED = 768


def _proj_kernel(x_ref, g_ref, w_ref, hg_ref, q_ref, kv_ref, win_ref, ksv_ref, u_ref, cb_ref, z_ref, ng_ref):
    h = _row_rms(x_ref[...], g_ref[...]).astype(BF16)

    def sec(lo, hi):
        return _dot(h, w_ref[:, lo:hi])

    nrm = _head_rms(sec(0, N_NORMED), hg_ref[...])
    ksel = nrm[:, C_KSEL:C_KSEL + KV_W]
    kwin = nrm[:, C_KWIN:C_KWIN + KV_W]
    q_ref[...] = (nrm[:, :NSA_W] * (HEAD_DIM ** -0.5)).astype(BF16)
    rest = sec(C_KCMP, C_U)
    vsel = rest[:, 2 * KV_W:3 * KV_W]
    vwin = rest[:, 3 * KV_W:4 * KV_W]
    kv_ref[:, 0:2 * KV_W] = rest[:, 0:2 * KV_W]
    kv_ref[:, 2 * KV_W:3 * KV_W] = ksel
    kv_ref[:, 3 * KV_W:4 * KV_W] = vsel
    win_ref[:, 0:KV_W] = kwin
    win_ref[:, KV_W:2 * KV_W] = vwin
    ksv_ref[:, 0:KV_W] = ksel.astype(BF16)
    ksv_ref[:, KV_W:2 * KV_W] = vsel.astype(BF16)
    ksv_ref[:, 2 * KV_W:3 * KV_W] = kwin.astype(BF16)
    ksv_ref[:, 3 * KV_W:4 * KV_W] = vwin.astype(BF16)
    u_ref[...] = sec(C_U, C_CB)
    cv = sec(C_CB, C_MG)
    cw = C_CC - C_CB
    cb_ref[...] = cv[:, 0:cw]
    z_ref[...] = cv[:, cw:2 * cw] * cv[:, 2 * cw:3 * cw]
    ng_ref[...] = sec(C_NG, C_END)


def _proj(x, gain, w, head_gain, tm):
    n, d = x.shape
    row = lambda w_: pl.BlockSpec((tm, w_), lambda i: (i, 0))
    full = lambda a: pl.BlockSpec(a.shape, lambda i: (0, 0))
    gain = gain.reshape(1, d)
    widths = ((NSA_W, BF16), (4 * KV_W, F32), (2 * KV_W, F32), (4 * KV_W, BF16), (C_CB - C_U, F32),
              (C_CC - C_CB, F32), (C_CC - C_CB, F32), (C_END - C_NG, F32))
    return pl.pallas_call(
        _proj_kernel,
        grid=(n // tm,),
        in_specs=[row(d), full(gain), pl.BlockSpec(w.shape, lambda i: (0, 0), pipeline_mode=pl.Buffered(1)),
                  full(head_gain)],
        out_specs=[row(w_) for w_, _ in widths],
        out_shape=[jax.ShapeDtypeStruct((n, w_), dt) for w_, dt in widths],
        compiler_params=_params("parallel"),
        name="proj",
    )(x, gain, w, head_gain)


def _compress_kernel(k_ref, v_ref, pe_ref, w1_ref, w2_ref, kg_ref, kc_ref, vc_ref, *, n_ch):
    outs = []
    for s, src_ref in enumerate((k_ref, v_ref)):
        first = jnp.zeros((n_ch, KV_W), F32)
        second = jnp.zeros((n_ch, KV_W), F32)
        for l in range(CMP_STRIDE):
            xl = src_ref[pl.ds(l, n_ch, stride=CMP_STRIDE), :]
            first += _dot((xl + pe_ref[s, l:l + 1, :]).astype(BF16), w1_ref[s, l])
            second += _dot((xl + pe_ref[s, CMP_STRIDE + l:CMP_STRIDE + l + 1, :]).astype(BF16),
                           w1_ref[s, CMP_STRIDE + l])
        hid = first + pltpu.roll(second, n_ch - 1, 0)
        act = hid * jax.nn.sigmoid(hid)
        outs.append(_dot(act.astype(BF16), w2_ref[s]))
    kc = _head_rms(outs[0], kg_ref[...])
    rows = kc_ref.shape[0]
    for ref, val in ((kc_ref, kc), (vc_ref, outs[1])):
        ref[0:CMP_PAD, :] = jnp.zeros((CMP_PAD, KV_W), F32)
        ref[CMP_PAD:CMP_PAD + n_ch, :] = val
        ref[CMP_PAD + n_ch:rows, :] = jnp.zeros((rows - CMP_PAD - n_ch, KV_W), F32)


def _cmp_rows(t):
    return -(-(CMP_PAD + t // CMP_STRIDE) // LANES) * LANES


def _compress_prompt(kv4, pe2, w1bd, w2bd, kgain, b, t):
    n_ch = t // CMP_STRIDE
    rows = _cmp_rows(t)
    full = lambda a: pl.BlockSpec(a.shape, lambda i: (0,) * a.ndim)
    out = jax.ShapeDtypeStruct((b, rows, KV_W), F32)
    return pl.pallas_call(
        functools.partial(_compress_kernel, n_ch=n_ch),
        grid=(b,),
        in_specs=[pl.BlockSpec((t, KV_W), lambda i: (i, 0)), pl.BlockSpec((t, KV_W), lambda i: (i, 1)),
                  full(pe2), full(w1bd), full(w2bd), full(kgain)],
        out_specs=[pl.BlockSpec((None, rows, KV_W), lambda i: (i, 0, 0))] * 2,
        out_shape=[out, out],
        compiler_params=_params("parallel"),
        name="compress_prompt",
    )(kv4, kv4, pe2, w1bd, w2bd, kgain)


def _lane_tiles(x):
    return [x[:, k * LANES:(k + 1) * LANES] for k in range(x.shape[1] // LANES)]


def _softmax_lane_tiles(tiles):
    m = functools.reduce(jnp.maximum, tiles)
    m = jnp.broadcast_to(jnp.max(m, axis=1, keepdims=True), m.shape)
    m = jnp.where(m == NEG_INF, 0.0, m)
    e = [jnp.exp(x - m) for x in tiles]
    den = functools.reduce(jnp.add, e)
    den = jnp.broadcast_to(jnp.sum(den, axis=1, keepdims=True), den.shape)
    return e, 1.0 / jnp.where(den > 0, den, 1.0)


def _top_k_mask(score, k):
    col = lax.broadcasted_iota(jnp.int32, score.shape, 1).astype(F32)
    sel = jnp.zeros(score.shape, F32)
    work = score
    for _ in range(k):
        m = jnp.max(work, axis=1, keepdims=True)
        first = jnp.min(jnp.where(work == m, col, float(score.shape[1])), axis=1, keepdims=True)
        pick = col == first
        sel = jnp.maximum(sel, pick.astype(F32))
        work = jnp.where(pick, NEG_INF, work)
    return sel


def _attn_prompt_kernel(q_ref, ng_ref, kc_ref, vc_ref, kall_ref, emat_ref, smap_ref, d01_ref, bc_ref, wtab_ref, o_ref,
                        m_scr, acc_scr, ob_scr):
    i = pl.program_id(1)
    lane = lax.broadcasted_iota(jnp.int32, (Q_BLOCK, LANES), 1)
    near0 = pl.multiple_of(i * (Q_BLOCK // CMP_STRIDE), 8)
    own = pl.multiple_of(i * Q_BLOCK, Q_BLOCK)
    scores, lhs_g = [], []

    for g in range(N_KV_HEADS):
        in_group = (lane >= HEAD_DIM) == (g == 1)
        lhs = jnp.concatenate(
            [jnp.where(in_group, q_ref[:, h * LANES:(h + 1) * LANES], jnp.zeros((), BF16)) for h in range(HPG)], axis=0)

        s_far = _dot_nt(lhs, kc_ref[...].astype(BF16))
        s_near = _dot_nt(lhs, kc_ref[pl.ds(near0, LANES), :].astype(BF16)) + bc_ref[g]
        pcol = lax.broadcasted_iota(jnp.int32, (1, s_far.shape[1]), 1)
        s_far = s_far + jnp.where((pcol >= CMP_PAD) & (pcol < near0), 0.0, NEG_INF)
        ncol = lax.broadcasted_iota(jnp.int32, (1, LANES), 1)
        s_near = s_near + jnp.where(near0 + ncol >= CMP_PAD, 0.0, NEG_INF)
        e_c, inv_c = _softmax_lane_tiles(_lane_tiles(s_far) + [s_near])
        p_far = jnp.concatenate([e * inv_c for e in e_c[:-1]], axis=1)
        p_near = e_c[-1] * inv_c
        ob_scr[g, 0] = (_dot(p_far.astype(BF16), vc_ref[...].astype(BF16))
                        + _dot(p_near.astype(BF16), vc_ref[pl.ds(near0, LANES), :].astype(BF16)))

        imp_far = sum(p_far[h * Q_BLOCK:(h + 1) * Q_BLOCK] for h in range(HPG))
        imp_near = sum(p_near[h * Q_BLOCK:(h + 1) * Q_BLOCK] for h in range(HPG))
        score = (_dot_split(imp_far, smap_ref[...].astype(BF16))
                 + _dot_split(imp_near, smap_ref[pl.ds(near0, LANES), :].astype(BF16)))
        jcol = lax.broadcasted_iota(jnp.int32, score.shape, 1)
        cur = 2 * i + (lax.broadcasted_iota(jnp.int32, score.shape, 0) >= SEL_BLOCK).astype(jnp.int32)
        forced = (jcol == 0) | (jcol == cur) | (jcol == cur - 1)
        score = jnp.where(forced, jnp.inf, score)
        scores.append(jnp.where(jcol <= cur, score, NEG_INF))
        lhs_g.append(lhs)

        s_w = _dot_nt(lhs, kall_ref[pl.ds(own, WINDOW + Q_BLOCK), 2 * KV_W:3 * KV_W]) + wtab_ref[g]
        wcol = lax.broadcasted_iota(jnp.int32, s_w.shape, 1)
        s_w = jnp.where(wcol >= WINDOW - own, s_w, NEG_INF)
        e_w, inv_w = _softmax_lane_tiles(_lane_tiles(s_w))
        ob_scr[g, 2] = _dot(jnp.concatenate(e_w, axis=1).astype(BF16),
                            kall_ref[pl.ds(own, WINDOW + Q_BLOCK), 3 * KV_W:4 * KV_W]) * inv_w

    sel_both = _top_k_mask(jnp.concatenate(scores, axis=0), TOP_K)
    n_plain = jnp.maximum(i - 1, 0)
    per_chunk = SEL_CHUNK // Q_BLOCK
    n_full = n_plain // per_chunk
    n_rest = n_plain % per_chunk
    ones_v = jnp.ones((4 * SEL_CHUNK, LANES), BF16)

    for g in range(N_KV_HEADS):
        lhs = lhs_g[g]
        sel = sel_both[g * Q_BLOCK:(g + 1) * Q_BLOCK]
        not_sel = jnp.concatenate([(sel - 1.0) * MASK_BIG] * HPG, axis=0).astype(BF16)
        lhs_sel = jnp.concatenate([lhs, not_sel], axis=1)

        def sel_logits(pos, width):
            keys = jnp.concatenate([kall_ref[pl.ds(WINDOW + pos, width), 0:KV_W],
                                    emat_ref[pl.ds(Q_BLOCK + pos, width), :]], axis=1)
            return _dot_nt(lhs_sel, keys)

        def sel_values(pos, width):
            return jnp.concatenate([kall_ref[pl.ds(WINDOW + pos, width), KV_W:2 * KV_W], ones_v[:width]], axis=1)

        def lane_tile_max(s):
            out = s[:, 0:LANES]
            for k in range(1, s.shape[1] // LANES):
                out = jnp.maximum(out, s[:, k * LANES:(k + 1) * LANES])
            return out

        def weights(s):
            mb = m_scr[...]
            return jnp.concatenate([jnp.exp(s[:, k * LANES:(k + 1) * LANES] - mb)
                                    for k in range(s.shape[1] // LANES)], axis=1).astype(BF16)

        def rest_logits():
            s = sel_logits(pl.multiple_of(n_full * SEL_CHUNK, SEL_CHUNK), SEL_CHUNK)
            col = lax.broadcasted_iota(jnp.int32, s.shape, 1)
            return jnp.where(col < n_rest * Q_BLOCK, s, NEG_INF)

        near = pl.multiple_of(own - Q_BLOCK, Q_BLOCK)
        s_near2 = sel_logits(near, 2 * Q_BLOCK) + d01_ref[g]
        ncol2 = lax.broadcasted_iota(jnp.int32, s_near2.shape, 1)
        s_near2 = jnp.where((i == 0) & (ncol2 < Q_BLOCK), NEG_INF, s_near2)

        def over_plain_keys(fn):
            quad = 4 * SEL_CHUNK

            def quad_body(c, carry):
                pos = pl.multiple_of(c * quad, quad)
                fn(sel_logits(pos, quad), pos, quad)
                return carry

            lax.fori_loop(0, n_full // 4, quad_body, 0)
            left = n_full % 4

            @pl.when(left >= 2)
            def _():
                pos = pl.multiple_of((n_full - left) * SEL_CHUNK, SEL_CHUNK)
                fn(sel_logits(pos, 2 * SEL_CHUNK), pos, 2 * SEL_CHUNK)

            @pl.when(left % 2 == 1)
            def _():
                pos = pl.multiple_of((n_full - 1) * SEL_CHUNK, SEL_CHUNK)
                fn(sel_logits(pos, SEL_CHUNK), pos, SEL_CHUNK)

            @pl.when(n_rest > 0)
            def _():
                fn(rest_logits(), pl.multiple_of(n_full * SEL_CHUNK, SEL_CHUNK), SEL_CHUNK)

        def take_max(s, pos, width):
            m_scr[...] = jnp.maximum(m_scr[...], lane_tile_max(s))

        def accumulate(s, pos, width):
            acc_scr[...] += _dot(weights(s), sel_values(pos, width))

        m_scr[...] = lane_tile_max(s_near2)
        over_plain_keys(take_max)
        m_scr[...] = jnp.broadcast_to(jnp.max(m_scr[...], axis=1, keepdims=True), m_scr.shape)
        acc_scr[...] = _dot(weights(s_near2), sel_values(near, 2 * Q_BLOCK))
        over_plain_keys(accumulate)
        ob_scr[g, 1] = acc_scr[:, 0:LANES] / acc_scr[:, LANES:2 * LANES]

    sig = jax.nn.sigmoid(ng_ref[...])
    for h in range(HPG):
        slab = None
        for j in range(3):
            part = []
            for g in range(N_KV_HEADS):
                c = j * N_Q_HEADS + g * HPG + h
                part.append(sig[:, c:c + 1] * ob_scr[g, j, h * Q_BLOCK:(h + 1) * Q_BLOCK, :])
            term = jnp.where(lane < HEAD_DIM, part[0], part[1])
            slab = term if slab is None else slab + term
        o_ref[:, h * LANES:(h + 1) * LANES] = slab.astype(BF16)


def _attn_prompt(q, ng, kc, vc, kall, emat, smap, d01, bc, wtab, b, t):
    nq = t // Q_BLOCK
    rows = HPG * Q_BLOCK
    full = lambda a: pl.BlockSpec(a.shape, lambda bi, i: (0,) * a.ndim)
    per_batch = lambda a: pl.BlockSpec((None,) + a.shape[1:], lambda bi, i: (bi, 0, 0))
    return pl.pallas_call(
        _attn_prompt_kernel,
        grid=(b, nq),
        in_specs=[
            pl.BlockSpec((Q_BLOCK, NSA_W), lambda bi, i: (bi * nq + i, 0)),
            pl.BlockSpec((Q_BLOCK, LANES), lambda bi, i: (bi * nq + i, 0)),
            per_batch(kc), per_batch(vc), per_batch(kall),
            full(emat), full(smap), full(d01), full(bc), full(wtab),
        ],
        out_specs=pl.BlockSpec((Q_BLOCK, NSA_W), lambda bi, i: (bi * nq + i, 0)),
        out_shape=jax.ShapeDtypeStruct((b * t, NSA_W), BF16),
        scratch_shapes=[pltpu.VMEM((rows, LANES), F32), pltpu.VMEM((rows, 2 * LANES), F32),
                        pltpu.VMEM((N_KV_HEADS, 3, rows, LANES), F32)],
        compiler_params=_params("parallel", "arbitrary"),
        name="attn_prompt",
    )(q, ng, kc, vc, kall, emat, smap, d01, bc, wtab)


def _pool_rows(ext_scr, base, u, pos):
    m, width = u.shape
    lane = lax.broadcasted_iota(jnp.int32, (m, width), 1)
    gc = width // len(POOL_WINDOWS)
    run = u
    pooled = None
    for k in range(1, max(POOL_WINDOWS)):
        run = run + ext_scr[base - k:base - k + m, :]
        w = k + 1
        if w in POOL_WINDOWS:
            mean = run / jnp.minimum(pos + 1, w).astype(F32)
            pooled = mean if pooled is None else jnp.where(lane >= POOL_WINDOWS.index(w) * gc, mean, pooled)
    return pooled - u


def _conv_rows(ext_scr, base, m, cw_ref):
    y = None
    for j in range(CONV_K):
        off = base - (CONV_K - 1) + j
        term = cw_ref[j:j + 1, :] * ext_scr[off:off + m, :]
        y = term if y is None else y + term
    return y


def _mix_prompt_kernel(u_ref, uh_ref, z_ref, zh_ref, cb_ref, cw_ref, d_ref, c_ref, ext_scr, *, tm):
    i = pl.program_id(1)
    has_hist = (i > 0).astype(F32)
    pos = i * tm + lax.broadcasted_iota(jnp.int32, u_ref.shape, 0)
    ext_scr[0:HALO, :] = uh_ref[...] * has_hist
    ext_scr[HALO:HALO + tm, :] = u_ref[...]
    d_ref[...] = _pool_rows(ext_scr, HALO, u_ref[...], pos).astype(BF16)
    ext_scr[0:HALO, :] = zh_ref[...] * has_hist
    ext_scr[HALO:HALO + tm, :] = z_ref[...]
    c_ref[...] = (cb_ref[...] * _conv_rows(ext_scr, HALO, tm, cw_ref)).astype(BF16)


def _mix_prompt(u, z, cb, conv_w, b, t, tm):
    nt = t // tm
    width = u.shape[1]
    row = pl.BlockSpec((tm, width), lambda bi, i: (bi * nt + i, 0))
    halo = pl.BlockSpec((HALO, width), lambda bi, i: (jnp.maximum((bi * nt + i) * (tm // HALO) - 1, 0), 0))
    out = jax.ShapeDtypeStruct((b * t, width), BF16)
    return pl.pallas_call(
        functools.partial(_mix_prompt_kernel, tm=tm),
        grid=(b, nt),
        in_specs=[row, halo, row, halo, row, pl.BlockSpec(conv_w.shape, lambda bi, i: (0, 0))],
        out_specs=[row, row],
        out_shape=[out, out],
        scratch_shapes=[pltpu.VMEM((HALO + tm, width), F32)],
        compiler_params=_params("parallel", "parallel"),
        name="mix_prompt",
    )(u, u, z, z, cb, conv_w)


def _merge_kernel(x_ref, ap_ref, as_ref, dp_ref, ds_ref, cp_ref, cs_ref, g_ref, wg_ref, wa_ref, wp_ref, ps_ref, wc_ref,
                  wo_ref, o_ref, *, prompt_tiles):
    dm = x_ref.shape[1]
    is_prompt = pl.program_id(0) < prompt_tiles
    a = jnp.where(is_prompt, ap_ref[...], as_ref[...])
    d = jnp.where(is_prompt, dp_ref[...], ds_ref[...])
    c = jnp.where(is_prompt, cp_ref[...], cs_ref[...])
    gates = jax.nn.sigmoid(_dot(_row_rms(x_ref[...], g_ref[...]).astype(BF16), wg_ref[...]))
    m = gates[:, 0:dm] * _dot(a, wa_ref[...])
    m = m + gates[:, dm:2 * dm] * (_dot(d, wp_ref[...]) * ps_ref[...])
    m = m + gates[:, 2 * dm:3 * dm] * _dot(c, wc_ref[...])
    o_ref[...] = x_ref[...] + _dot(m.astype(BF16), wo_ref[...])


def _merge(x, attn_p, attn_s, d_p, d_s, c_p, c_s, gain, w_gate, wa, wp, ps, wc, wo, tm):
    n, dm = x.shape
    gain = gain.reshape(1, dm)
    n_p, n_s = attn_p.shape[0], attn_s.shape[0]
    assert n_p % tm == 0 and n_s % tm == 0 and n_p + n_s == n
    pt = n_p // tm
    row = lambda a: pl.BlockSpec((tm, a.shape[1]), lambda i: (i, 0))
    prompt = lambda a: pl.BlockSpec((tm, a.shape[1]), lambda i: (jnp.minimum(i, pt - 1), 0))
    sample = lambda a: pl.BlockSpec((tm, a.shape[1]), lambda i: (jnp.maximum(i - pt, 0), 0))
    full = lambda a: pl.BlockSpec(a.shape, lambda i: (0, 0))
    ps = ps.reshape(1, dm)
    return pl.pallas_call(
        functools.partial(_merge_kernel, prompt_tiles=pt),
        grid=(n // tm,),
        in_specs=[row(x), prompt(attn_p), sample(attn_s), prompt(d_p), sample(d_s), prompt(c_p), sample(c_s),
                  full(gain), full(w_gate), full(wa), full(wp), full(ps), full(wc), full(wo)],
        out_specs=row(x),
        out_shape=jax.ShapeDtypeStruct((n, dm), F32),
        compiler_params=_params("parallel"),
        name="merge",
    )(x, attn_p, attn_s, d_p, d_s, c_p, c_s, gain, w_gate, wa, wp, ps, wc, wo)


SAMPLE_ROWS = 8
SAMPLE_NB = 4


def _cmp_partial_kernel(xt_ref, pe_ref, w1_ref, o_ref, x_scr, *, n_ch):
    pages, _, page = xt_ref.shape
    n_part = 4 if pages % 4 == 0 else 1
    pp, pc = pages // n_part, n_ch // n_part
    for part in range(n_part):
        for p in range(part * pp, (part + 1) * pp):
            x_scr[p * page:(p + 1) * page, :] = xt_ref[p].T
        first = jnp.zeros((pc, KV_W), F32)
        second = jnp.zeros((pc, KV_W), F32)
        for l in range(CMP_STRIDE):
            xl = x_scr[pl.ds(part * pp * page + l, pc, stride=CMP_STRIDE), :]
            first += _dot((xl + pe_ref[l:l + 1, :]).astype(BF16), w1_ref[l])
            second += _dot((xl + pe_ref[CMP_STRIDE + l:CMP_STRIDE + l + 1, :]).astype(BF16),
                           w1_ref[CMP_STRIDE + l])
        o_ref[part * pc:(part + 1) * pc, 0:KV_W] = first
        o_ref[part * pc:(part + 1) * pc, KV_W:2 * KV_W] = second


def _cmp_partial(cache_t, pe2, w1bd, layer, n_pool):
    page = cache_t.shape[2]
    ch = page // CMP_STRIDE
    pp = next(p for p in (64, 32, 16, 8, 4, 2, 1) if n_pool % p == 0)
    nb = n_pool // pp
    return pl.pallas_call(
        functools.partial(_cmp_partial_kernel, n_ch=pp * ch),
        grid=(nb, 2),
        in_specs=[pl.BlockSpec((pp, KV_W, page), lambda i, s: (layer * nb + i, s, 0)),
                  pl.BlockSpec((None,) + pe2.shape[1:], lambda i, s: (s, 0, 0)),
                  pl.BlockSpec((None,) + w1bd.shape[1:], lambda i, s: (s, 0, 0, 0))],
        out_specs=pl.BlockSpec((pp * ch, 2 * KV_W), lambda i, s: (i, s)),
        out_shape=jax.ShapeDtypeStruct((n_pool * ch, 4 * KV_W), F32),
        scratch_shapes=[pltpu.VMEM((pp * page, KV_W), F32)],
        compiler_params=_params("parallel", "arbitrary"),
        name="cmp_partial",
    )(cache_t, pe2, w1bd)


def _softmax_pair(s_a, vt_a, s_b, v_b):
    m = jnp.maximum(jnp.max(s_a, axis=1, keepdims=True), jnp.max(s_b, axis=1, keepdims=True))
    e_a = jnp.exp(s_a - m)
    e_b = jnp.exp(s_b - m)
    den = jnp.sum(e_a, axis=1, keepdims=True) + jnp.sum(e_b, axis=1, keepdims=True)
    return (_dot_nt(e_a.astype(BF16), vt_a) + _dot(e_b.astype(BF16), v_b)) / den


def _sample_kernel(pt_ref, q_ref, ng_ref, kv_ref, win_ref, wnt_ref, u_ref, z_ref, cb_ref, swin_ref, spool_ref,
                   sconv_ref, cache_ref, fs_ref, w2_ref, kg_ref, cw_ref, smap_ref, emat_ref, bc_ref, dsel_ref, dwin_ref,
                   dnew_ref, attn_ref, d_ref, c_ref, owin_ref, opool_ref, oconv_ref,
                   kvbuf, fsbuf, sems, ext_scr, *, layer_base, n_pages, page, s_len, past):
    i = pl.program_id(0)
    slot = i % 2
    ch = page // CMP_STRIDE
    sr = SAMPLE_ROWS

    def copies(step, slot_):
        out = []
        for sb in range(SAMPLE_NB):
            for pg in range(n_pages):
                pid = pt_ref[step * SAMPLE_NB + sb, pg]
                out.append(pltpu.make_async_copy(
                    cache_ref.at[layer_base + pid, pl.ds(2 * KV_W, 2 * KV_W), :],
                    kvbuf.at[slot_, sb, :, pl.ds(pg * page, page)], sems.at[slot_, 0]))
                out.append(pltpu.make_async_copy(
                    fs_ref.at[pid], fsbuf.at[slot_, sb, pl.ds(pg * ch, ch), :], sems.at[slot_, 1]))
        return out

    @pl.when(i == 0)
    def _():
        for c in copies(0, 0):
            c.start()

    @pl.when(i + 1 < pl.num_programs(0))
    def _():
        for c in copies(i + 1, 1 - slot):
            c.start()

    for c in copies(i, slot):
        c.wait()

    lane = lax.broadcasted_iota(jnp.int32, (sr, LANES), 1)
    grp_rows = N_KV_HEADS * sr
    q_all = q_ref[...].astype(F32)
    sig_all = jax.nn.sigmoid(ng_ref[...])
    kvn = kv_ref[...]
    winn = win_ref[...]
    pad_rows = jnp.zeros((LANES - sr, KV_W), F32)
    w_keep = swin_ref.shape[2]
    wlane = lax.broadcasted_iota(jnp.int32, swin_ref.shape[1:], 1)
    hist = spool_ref.shape[1]
    ck = sconv_ref.shape[1]
    attn_rows, d_rows, c_rows = [], [], []

    every = range(SAMPLE_NB)

    def new_rows(x8):
        return jnp.concatenate([x8, pad_rows], axis=0).astype(BF16)

    lhs, ks_new, vs_new, kw_new, vw_new = [], [], [], [], []
    for sb in every:
        r0 = sb * sr
        q8 = q_all[r0:r0 + sr]
        pieces = []
        for h in range(HPG):
            for g in range(N_KV_HEADS):
                pieces.append(jnp.where((lane >= HEAD_DIM) == (g == 1), q8[:, h * LANES:(h + 1) * LANES], 0.0))
        lhs.append(jnp.concatenate(pieces, axis=0).astype(BF16))
        ks_new.append(new_rows(kvn[r0:r0 + sr, 2 * KV_W:3 * KV_W]))
        vs_new.append(new_rows(kvn[r0:r0 + sr, 3 * KV_W:4 * KV_W]))
        kw_new.append(new_rows(winn[r0:r0 + sr, 0:KV_W]))
        vw_new.append(new_rows(winn[r0:r0 + sr, KV_W:2 * KV_W]))

    n_ch = fsbuf.shape[2]
    cmp = []
    for s_ in range(2):
        hid = jnp.concatenate(
            [fsbuf[slot, sb, :, 2 * s_ * KV_W:(2 * s_ + 1) * KV_W]
             + pltpu.roll(fsbuf[slot, sb, :, (2 * s_ + 1) * KV_W:(2 * s_ + 2) * KV_W], n_ch - 1, 0) for sb in every],
            axis=0)
        cmp.append(_dot((hid * jax.nn.sigmoid(hid)).astype(BF16), w2_ref[s_]))
    kc_all = _head_rms(cmp[0], kg_ref[...]).astype(BF16)
    vc_all = cmp[1].astype(BF16)
    p_c, o_c = [], []
    for sb in every:
        s_c = _dot_nt(lhs[sb], kc_all[sb * n_ch:(sb + 1) * n_ch]) + bc_ref[...]
        e_c = jnp.exp(s_c - jnp.max(s_c, axis=1, keepdims=True))
        p_c.append(e_c / jnp.sum(e_c, axis=1, keepdims=True))
        o_c.append(_dot(p_c[sb].astype(BF16), vc_all[sb * n_ch:(sb + 1) * n_ch]))

    o_w = [_softmax_pair(_dot(lhs[sb], swin_ref[sb, 0:KV_W, :].astype(BF16)) + dwin_ref[...],
                         swin_ref[sb, KV_W:2 * KV_W, :].astype(BF16),
                         _dot_nt(lhs[sb], kw_new[sb]) + dnew_ref[...], vw_new[sb]) for sb in every]

    imp = jnp.concatenate([sum(p_c[sb][h * grp_rows:(h + 1) * grp_rows] for h in range(HPG)) for sb in every], axis=0)
    score = _dot_split(imp, smap_ref[...])
    jcol = lax.broadcasted_iota(jnp.int32, score.shape, 1)
    cur = (past + lax.broadcasted_iota(jnp.int32, score.shape, 0) % sr) // SEL_BLOCK
    forced = (jcol == 0) | (jcol == cur) | (jcol == cur - 1)
    score = jnp.where(forced, jnp.inf, score)
    score = jnp.where(jcol <= cur, score, NEG_INF)
    not_sel_all = (_top_k_mask(score, TOP_K) - 1.0) * MASK_BIG

    o_s = []
    for sb in every:
        not_sel = jnp.concatenate([not_sel_all[sb * grp_rows:(sb + 1) * grp_rows]] * HPG, axis=0).astype(BF16)
        lhs_sel = jnp.concatenate([lhs[sb], not_sel], axis=1)
        k_past = jnp.concatenate([kvbuf[slot, sb, 0:KV_W, :].astype(BF16), emat_ref[...]], axis=0)
        o_s.append(_softmax_pair(_dot(lhs_sel, k_past) + dsel_ref[...],
                                 kvbuf[slot, sb, KV_W:2 * KV_W, :].astype(BF16),
                                 _dot_nt(lhs[sb], ks_new[sb]) + dnew_ref[...], vs_new[sb]))

    for sb in every:
        r0 = sb * sr
        branch = (o_c[sb], o_s[sb], o_w[sb])
        sig8 = sig_all[r0:r0 + sr]
        slabs = []
        for h in range(HPG):
            slab = None
            for j in range(3):
                part = []
                for g in range(N_KV_HEADS):
                    c = j * N_Q_HEADS + g * HPG + h
                    lo = h * grp_rows + g * sr
                    part.append(sig8[:, c:c + 1] * branch[j][lo:lo + sr])
                term = jnp.where(lane < HEAD_DIM, part[0], part[1])
                slab = term if slab is None else slab + term
            slabs.append(slab)
        attn_rows.append(jnp.concatenate(slabs, axis=1))

        state = pltpu.roll(swin_ref[sb], w_keep - s_len, 1)
        for tt in range(s_len):
            state = jnp.where(wlane == w_keep - s_len + tt, wnt_ref[sb, :, tt:tt + 1], state)
        owin_ref[sb] = state

        u8 = u_ref[r0:r0 + sr, :]
        ext_scr[0:hist, :] = spool_ref[sb]
        ext_scr[hist:hist + sr, :] = u8
        d_rows.append(_pool_rows(ext_scr, hist, u8, hist + lax.broadcasted_iota(jnp.int32, u8.shape, 0)))
        opool_ref[sb] = ext_scr[s_len:s_len + hist, :]
        ext_scr[0:ck, :] = sconv_ref[sb]
        ext_scr[ck:ck + sr, :] = z_ref[r0:r0 + sr, :]
        c_rows.append(cb_ref[r0:r0 + sr, :] * _conv_rows(ext_scr, ck, sr, cw_ref))
        oconv_ref[sb] = ext_scr[s_len:s_len + ck, :]

    attn_ref[...] = jnp.concatenate(attn_rows, axis=0).astype(BF16)
    d_ref[...] = jnp.concatenate(d_rows, axis=0).astype(BF16)
    c_ref[...] = jnp.concatenate(c_rows, axis=0).astype(BF16)


def _sample_layer(layer, page_table, q, ng, kv4, win, wnt, u, z, cb, swin3, spool3, sconv3, cache3, fs3, w2bd, kgain,
                  conv_w, smap_s, emat, bc_s, dsel, dwin, dnew, n_p, db, s_len, n_pool):
    nsteps = db // SAMPLE_NB
    rb = SAMPLE_NB * SAMPLE_ROWS
    r0 = n_p // rb
    page = cache3.shape[2]
    n_pages = page_table.shape[1]
    past = n_pages * page
    tok = lambda a: pl.BlockSpec((rb, a.shape[1]), lambda i, pt: (r0 + i, 0))
    new_t = pl.BlockSpec((SAMPLE_NB,) + wnt.shape[1:], lambda i, pt: (i, 0, 0))
    state = lambda a: pl.BlockSpec((SAMPLE_NB,) + a.shape[1:], lambda i, pt: (layer * nsteps + i, 0, 0))
    full = lambda a: pl.BlockSpec(a.shape, lambda i, pt: (0,) * a.ndim)
    hbm = pl.BlockSpec(memory_space=pl.ANY)
    out_tok = lambda w_: pl.BlockSpec((rb, w_), lambda i, pt: (i, 0))
    out_state = lambda a: pl.BlockSpec((SAMPLE_NB,) + a.shape[1:], lambda i, pt: (i, 0, 0))
    width = u.shape[1]
    n_tok = db * SAMPLE_ROWS
    return pl.pallas_call(
        functools.partial(_sample_kernel, layer_base=layer * n_pool, n_pages=n_pages, page=page, s_len=s_len, past=past),
        grid_spec=pltpu.PrefetchScalarGridSpec(
            num_scalar_prefetch=1,
            grid=(nsteps,),
            in_specs=[tok(q), tok(ng), tok(kv4), tok(win), new_t, tok(u), tok(z), tok(cb), state(swin3), state(spool3),
                      state(sconv3), hbm, hbm, full(w2bd), full(kgain), full(conv_w), full(smap_s), full(emat),
                      full(bc_s), full(dsel), full(dwin), full(dnew)],
            out_specs=[out_tok(NSA_W), out_tok(width), out_tok(width), out_state(swin3), out_state(spool3),
                       out_state(sconv3)],
            scratch_shapes=[pltpu.VMEM((2, SAMPLE_NB, 2 * KV_W, past), F32),
                            pltpu.VMEM((2, SAMPLE_NB, past // CMP_STRIDE, 4 * KV_W), F32),
                            pltpu.SemaphoreType.DMA((2, 2)),
                            pltpu.VMEM((4 * SAMPLE_ROWS, width), F32)]),
        out_shape=[jax.ShapeDtypeStruct((n_tok, NSA_W), BF16), jax.ShapeDtypeStruct((n_tok, width), BF16),
                   jax.ShapeDtypeStruct((n_tok, width), BF16), jax.ShapeDtypeStruct((db,) + swin3.shape[1:], F32),
                   jax.ShapeDtypeStruct((db,) + spool3.shape[1:], F32),
                   jax.ShapeDtypeStruct((db,) + sconv3.shape[1:], F32)],
        compiler_params=_params("arbitrary"),
        name="sample_mix",
    )(page_table, q, ng, kv4, win, wnt, u, z, cb, swin3, spool3, sconv3, cache3, fs3, w2bd, kgain, conv_w, smap_s,
      emat, bc_s, dsel, dwin, dnew)


def _bucket_table(n):
    d = np.arange(n)
    nf = np.maximum(d, 1).astype(np.float32)
    large = MAX_EXACT + (np.log(nf / np.float32(MAX_EXACT)) / np.float32(math.log(MAX_DISTANCE / MAX_EXACT))
                         * np.float32(N_BUCKETS - MAX_EXACT)).astype(np.int32)
    return np.where(d < MAX_EXACT, d, np.minimum(large, N_BUCKETS - 1)).astype(np.int32)


def _bias_values(rel_bias, dist, visible=None):
    r, c = dist.shape
    offset = int(dist[0, 0])
    stride = int(dist[0, 0] - dist[0, 1]) if c > 1 else 1
    assert stride >= 1 and np.array_equal(dist, offset + np.arange(r)[:, None] - stride * np.arange(c)[None, :])
    visible = (dist >= 0) if visible is None else (visible & (dist >= 0))
    cw = stride * (c - 1) + 1
    period = r + cw
    m = np.arange(period)
    by_wrap = offset + np.where(m < cw, -m, period - m)
    bkt = _bucket_table(max(int(by_wrap.max()) + 1, MAX_DISTANCE))[np.clip(by_wrap, 0, None)]
    tab = rel_bias - rel_bias[N_BUCKETS - 1:N_BUCKETS]
    per_dist = tab[jnp.asarray(bkt)].T
    vals = jnp.tile(per_dist, (1, r))[:, :r * (period - 1)].reshape(-1, r, period - 1)[:, :, :cw:stride]
    vals = jnp.where(jnp.asarray(visible)[None], vals, NEG_INF)
    return vals.reshape((N_KV_HEADS, HPG) + dist.shape)


def _bias_table(rel_bias, dist):
    r, c = dist.shape
    return _bias_values(rel_bias, dist).reshape(N_KV_HEADS, HPG * r, c)


def _bias_table_sample(rel_bias, dist, visible=None):
    r, c = dist.shape
    return _bias_values(rel_bias, dist, visible).transpose(1, 0, 2, 3).reshape(HPG * N_KV_HEADS * r, c)


def _sel_map(n_cmp, n_rows, n_cols, pad):
    r_sel = SEL_BLOCK // CMP_STRIDE
    r_cmp = CMP_BLOCK // CMP_STRIDE
    out = np.zeros((n_rows, n_cols), np.float32)
    for j in range(n_cols):
        for m in range(r_sel):
            for n in range(r_cmp):
                c = r_sel * j + m - n
                if 0 <= c < n_cmp:
                    out[pad + c, j] += 1.0
    return out


def _slab_order(w, axis):
    shape = w.shape
    w = w.reshape(shape[:axis] + (N_KV_HEADS, HPG, HEAD_DIM) + shape[axis + 1:])
    return jnp.swapaxes(w, axis, axis + 1).reshape(shape)


def _proj_weight(w_in):
    kv0, ng0 = NSA_W, NSA_W + 6 * KV_W
    rest0 = ng0 + 3 * N_Q_HEADS
    slot = lambda s: w_in[:, :, kv0 + s * KV_W:kv0 + (s + 1) * KV_W]
    ng = w_in[:, :, ng0:rest0].reshape(w_in.shape[:2] + (N_Q_HEADS, 3))
    ng = jnp.swapaxes(ng, 2, 3).reshape(w_in.shape[:2] + (3 * N_Q_HEADS,))
    ng = jnp.pad(ng, ((0, 0), (0, 0), (0, C_END - C_NG - 3 * N_Q_HEADS)))
    parts = [_slab_order(w_in[:, :, :NSA_W], 2), slot(2), slot(4), slot(0), slot(1), slot(3), slot(5),
             w_in[:, :, rest0:], ng]
    out = jnp.concatenate(parts, axis=2)
    assert out.shape[2] == C_END
    return out


def _block_diag2(w):
    z = jnp.zeros_like(w)
    return jnp.concatenate([jnp.concatenate([w, z], axis=-1), jnp.concatenate([z, w], axis=-1)], axis=-2)


def _token_tile(n):
    for tm in (512, 256, 128):
        if n % tm == 0:
            return tm
    raise ValueError(f"token count {n} is not a multiple of 128")


def kernel(x_prompt, x_sample, cache_kv, state_win, state_pool, state_conv, page_table, rel_bias, ln_ffn1, w_ffn1_gate, w_ffn1_up, w_ffn1_down, ln_mix, w_in, q_norm, k_norm, cmp_pe, cmp_w1, cmp_w2, w_nsa_out, w_pool, pool_scale, conv_w, w_conv_out, w_o, ln_ffn2, w_ffn2_gate, w_ffn2_up, w_ffn2_down):
    b, t, dm = x_prompt.shape
    db, s, _ = x_sample.shape
    depth, n_pool, page = cache_kv.shape[:3]
    n_pages = page_table.shape[1]
    past = n_pages * page
    w_keep = state_win.shape[2]
    sr = SAMPLE_ROWS
    n_p, n_s = b * t, db * sr
    assert t % Q_BLOCK == 0 and t >= WINDOW and t // SEL_BLOCK <= LANES
    assert s <= sr and db % SAMPLE_NB == 0 and n_p % (SAMPLE_NB * sr) == 0
    assert (past + s) // CMP_STRIDE == past // CMP_STRIDE == LANES and w_keep <= past and s <= w_keep
    assert -(-(past + s) // SEL_BLOCK) <= LANES
    tm = _token_tile(n_p + n_s)
    tm_mix = _token_tile(t)

    w_proj = _proj_weight(w_in).astype(BF16)
    w_gate = w_proj[:, :, C_MG:C_NG]
    head_gain = jnp.concatenate([jnp.tile(q_norm, (1, N_Q_HEADS)), jnp.tile(k_norm[:, 1], (1, N_KV_HEADS)),
                                 jnp.tile(k_norm[:, 2], (1, N_KV_HEADS))], axis=1).reshape(depth, 1, N_NORMED)
    cmp_gain = jnp.tile(k_norm[:, 0], (1, N_KV_HEADS)).reshape(depth, 1, KV_W)
    pe2 = jnp.tile(cmp_pe, (1, 1, 1, N_KV_HEADS))
    w1bd = _block_diag2(cmp_w1).astype(BF16)
    w2bd = _block_diag2(cmp_w2).astype(BF16)
    wa = _slab_order(w_nsa_out, axis=1).astype(BF16)
    n_grp, gc, ge = w_pool.shape[1:]
    wp = jnp.zeros((depth, n_grp * gc, n_grp * ge), F32)
    for gi in range(n_grp):
        wp = wp.at[:, gi * gc:(gi + 1) * gc, gi * ge:(gi + 1) * ge].set(w_pool[:, gi])
    wp = wp.astype(BF16)
    wc = w_conv_out.astype(BF16)
    wo = w_o.astype(BF16)
    ffn1 = [w.astype(BF16) for w in (w_ffn1_gate, w_ffn1_up, w_ffn1_down)]
    ffn2 = [w.astype(BF16) for w in (w_ffn2_gate, w_ffn2_up, w_ffn2_down)]

    ti = np.arange(Q_BLOCK)
    d01 = _bias_table(rel_bias, Q_BLOCK + ti[:, None] - np.arange(2 * Q_BLOCK)[None, :])
    near_c = np.arange(LANES)
    bc = _bias_table(rel_bias, ti[:, None] - CMP_STRIDE * (near_c[None, :] - CMP_PAD) - (CMP_BLOCK - 1))
    n_cmp = t // CMP_STRIDE - CMP_BLOCK // CMP_STRIDE + 1
    smap = jnp.asarray(_sel_map(n_cmp, _cmp_rows(t), LANES, CMP_PAD))
    wj = np.arange(WINDOW + Q_BLOCK)[None, :]
    dist_pw = WINDOW + ti[:, None] - wj
    wtab = _bias_values(rel_bias, dist_pw, dist_pw < WINDOW).reshape(N_KV_HEADS, HPG * Q_BLOCK, WINDOW + Q_BLOCK)
    key_pos = np.arange(-Q_BLOCK, t + SEL_CHUNK)[:, None]
    emat_p = jnp.asarray((key_pos >= 0) & (key_pos < t)
                         & (key_pos // SEL_BLOCK == np.arange(LANES)[None, :])).astype(BF16)

    tq = np.arange(sr)[:, None]
    n_chunks = past // CMP_STRIDE
    n_cmp_s = n_chunks - CMP_BLOCK // CMP_STRIDE + 1
    cn = np.arange(n_chunks)[None, :]
    bc_s = _bias_table_sample(rel_bias, past + tq - (CMP_STRIDE * cn + CMP_BLOCK - 1), cn < n_cmp_s)
    dsel = _bias_table_sample(rel_bias, past + tq - np.arange(past)[None, :])
    tn = np.arange(LANES)[None, :]
    dnew = _bias_table_sample(rel_bias, tq - tn, tn < s)
    dist_w = w_keep + tq - np.arange(w_keep)[None, :]
    dwin = _bias_table_sample(rel_bias, dist_w, dist_w < WINDOW)
    smap_s = jnp.asarray(_sel_map(n_cmp_s, n_chunks, LANES, 0)).astype(BF16)
    emat = jnp.asarray(np.arange(LANES)[:, None] == np.arange(past)[None, :] // SEL_BLOCK).astype(BF16)

    cache3 = cache_kv.transpose(0, 1, 3, 4, 5, 2).reshape(depth * n_pool, -1, page)
    swin3 = state_win.transpose(0, 1, 3, 4, 5, 2).reshape(depth * db, -1, w_keep)
    spool3 = state_pool.reshape((depth * db,) + state_pool.shape[2:])
    sconv3 = state_conv.reshape((depth * db,) + state_conv.shape[2:])

    xs = jnp.pad(x_sample, ((0, 0), (0, sr - s), (0, 0)))
    x = jnp.concatenate([x_prompt.reshape(n_p, dm), xs.reshape(n_s, dm)], axis=0)
    outs = [[] for _ in range(8)]
    for l in range(depth):
        x = _ffn(x, ln_ffn1[l], ffn1[0][l], ffn1[1][l], ffn1[2][l], tm)
        q, kv4, win, ksv, u, cb, z, ng = _proj(x, ln_mix[l], w_proj[l], head_gain[l], tm)

        kc, vc = _compress_prompt(kv4, pe2[l], w1bd[l], w2bd[l], cmp_gain[l], b, t)
        kall = jnp.pad(ksv[:n_p].reshape(b, t, 4 * KV_W), ((0, 0), (WINDOW, SEL_CHUNK), (0, 0)))
        attn_p = _attn_prompt(q, ng, kc, vc, kall, emat_p, smap, d01, bc, wtab, b, t)
        d_p, c_p = _mix_prompt(u, z, cb, conv_w[l], b, t, tm_mix)

        fs3 = _cmp_partial(cache3, pe2[l], w1bd[l], l, n_pool).reshape(n_pool, page // CMP_STRIDE, -1)
        wnt = win[n_p:].reshape(db, sr, -1).transpose(0, 2, 1)
        attn_s, d_s, c_s, s_win, s_pool, s_conv = _sample_layer(
            l, page_table, q, ng, kv4, win, wnt, u, z, cb, swin3, spool3, sconv3, cache3, fs3, w2bd[l], cmp_gain[l],
            conv_w[l], smap_s, emat, bc_s, dsel, dwin, dnew, n_p, db, s, n_pool)

        x = _merge(x, attn_p, attn_s, d_p, d_s, c_p, c_s, ln_mix[l], w_gate[l], wa[l], wp[l], pool_scale[l], wc[l],
                   wo[l], tm)
        x = _ffn(x, ln_ffn2[l], ffn2[0][l], ffn2[1][l], ffn2[2][l], tm)

        p_keep = min(WINDOW, t)
        outs[0].append(kv4[:n_p].reshape(b, t, 4, N_KV_HEADS, HEAD_DIM))
        outs[1].append(win[:n_p].reshape(b, t, 2, N_KV_HEADS, HEAD_DIM)[:, t - p_keep:])
        outs[2].append(u[:n_p].reshape(b, t, -1)[:, t - POOL_STATE:])
        outs[3].append(z[:n_p].reshape(b, t, -1)[:, t - (CONV_K - 1):])
        outs[4].append(kv4[n_p:].reshape(db, sr, 4, N_KV_HEADS, HEAD_DIM)[:, :s])
        outs[5].append(s_win.reshape(db, 2, N_KV_HEADS, HEAD_DIM, w_keep).transpose(0, 4, 1, 2, 3))
        outs[6].append(s_pool)
        outs[7].append(s_conv)

    y_prompt = x[:n_p].reshape(b, t, dm)
    y_sample = x[n_p:].reshape(db, sr, dm)[:, :s]
    return (y_prompt, y_sample) + tuple(jnp.stack(o) for o in outs)
```
